```python
import jax, jax.numpy as jnp
from jax import lax
import numpy as np

D_MODEL = 1024
BATCH = 8
SEQ = 2048
DEPTH = 1
DEC_BATCH = 128
DEC_SEQ = 8
PAST_LEN = 16384
PAGE_SIZE = 128

D_MIX = D_MODEL
D_POOL = D_MIX // 2
D_CONV = D_MIX - D_POOL
POOL_WINDOWS = (2, 4, 8, 16)
N_POOL_GROUPS = len(POOL_WINDOWS)
POOL_GROUP = D_POOL // N_POOL_GROUPS
POOL_HIST = max(POOL_WINDOWS) - 1
CONV_WIDTH = 3
CONV_HIST = CONV_WIDTH - 1
N_CONV_HEADS = 4
D_FF = ((8 * D_MODEL // 3 + 127) // 128) * 128
N_MEM = 256
N_XHEADS = 4
XHEAD_DIM = D_MODEL // N_XHEADS
EPS = 1e-6

kernel_name = "hybrid_pool_conv_macaron_xattn_step"


def _rms(x, g):
    xf = x.astype(jnp.float32)
    y = xf * lax.rsqrt(jnp.mean(xf * xf, axis=-1, keepdims=True) + EPS)
    return (y * g.astype(jnp.float32)).astype(x.dtype)


def _swiglu_half(x, g, w_in, w_out):
    h = _rms(x, g)
    gate, up = jnp.split(h @ w_in, 2, axis=-1)
    return x + 0.5 * ((jax.nn.silu(gate) * up) @ w_out)


def _pool_mix(a, hist, pos0, pool_w, pool_scale):
    b, t, _ = a.shape
    ext = jnp.concatenate([hist, a], axis=1)
    cs = jnp.cumsum(ext.astype(jnp.float32), axis=1)
    cs = jnp.concatenate([jnp.zeros((b, 1, D_POOL), jnp.float32), cs], axis=1)
    end = cs[:, POOL_HIST + 1:]
    pos = pos0 + jnp.arange(t, dtype=jnp.int32)
    af = a.astype(jnp.float32)
    outs = []
    for gi, w in enumerate(POOL_WINDOWS):
        sl = slice(gi * POOL_GROUP, (gi + 1) * POOL_GROUP)
        start = cs[:, POOL_HIST + 1 - w: POOL_HIST + 1 - w + t, sl]
        cnt = jnp.minimum(pos + 1, w).astype(jnp.float32)[None, :, None]
        mean = (end[..., sl] - start) / cnt
        outs.append(mean - af[..., sl])
    d = jnp.stack(outs, axis=2).astype(a.dtype)
    y = jnp.einsum('btgc,gcd->btgd', d, pool_w).reshape(b, t, D_POOL)
    return y * pool_scale, ext[:, -POOL_HIST:]


def _short_conv(u, hist, conv_w):
    t = u.shape[1]
    ext = jnp.concatenate([hist, u], axis=1)
    y = sum(conv_w[k] * ext[:, k:k + t] for k in range(CONV_WIDTH))
    return y, ext[:, -CONV_HIST:]


def _cross_attn(x, g_xq, w_xq, w_xo, mem_k, mem_v):
    b, t, _ = x.shape
    q = (_rms(x, g_xq) @ w_xq).reshape(b, t, N_XHEADS, XHEAD_DIM)
    s = jnp.einsum('bthd,bmhd->bhtm', q.astype(jnp.float32), mem_k.astype(jnp.float32)) * (XHEAD_DIM ** -0.5)
    p = jax.nn.softmax(s, axis=-1).astype(x.dtype)
    o = jnp.einsum('bhtm,bmhd->bthd', p, mem_v).reshape(b, t, N_XHEADS * XHEAD_DIM)
    return x + o @ w_xo


def _layer(x, pos0, pool_hist, conv_hist, mem_k, mem_v, lw):
    (g_ffn1, w_ffn1_in, w_ffn1_out, g_mix, w_mix_in, pool_w, pool_scale, conv_w,
     w_mix_out, g_xq, w_xq, w_xo, g_ffn2, w_ffn2_in, w_ffn2_out) = lw
    x = _swiglu_half(x, g_ffn1, w_ffn1_in, w_ffn1_out)
    z = _rms(x, g_mix) @ w_mix_in
    a = z[..., :D_POOL]
    cb, cc, ch = jnp.split(z[..., D_POOL:], 3, axis=-1)
    p, new_pool = _pool_mix(a, pool_hist, pos0, pool_w, pool_scale)
    yc, new_conv = _short_conv(cc * ch, conv_hist, conv_w)
    x = x + jnp.concatenate([p, cb * yc], axis=-1) @ w_mix_out
    x = _cross_attn(x, g_xq, w_xq, w_xo, mem_k, mem_v)
    x = _swiglu_half(x, g_ffn2, w_ffn2_in, w_ffn2_out)
    return x, new_pool, new_conv


def setup_inputs(seed: int = 0) -> dict:
    key = jax.random.key(seed)
    ks = iter(jax.random.split(key, 40))
    f32 = jnp.float32

    def nrm(shape, fan_in):
        return jax.random.normal(next(ks), shape, f32) * (fan_in ** -0.5)

    def gain(shape):
        return 1.0 + 0.05 * jax.random.normal(next(ks), shape, f32)

    L = DEPTH
    return {
        "x_prompt": jax.random.normal(next(ks), (BATCH, SEQ, D_MODEL), f32),
        "x_sample": jax.random.normal(next(ks), (DEC_BATCH, DEC_SEQ, D_MODEL), f32),
        "mem_prompt": jax.random.normal(next(ks), (BATCH, N_MEM, D_MODEL), f32),
        "state_pool": jax.random.normal(next(ks), (L, DEC_BATCH, POOL_HIST, D_POOL), f32),
        "state_conv": jax.random.normal(next(ks), (L, DEC_BATCH, CONV_HIST, D_CONV), f32),
        "cache_mem_k": jax.random.normal(next(ks), (L, DEC_BATCH, N_MEM, N_XHEADS, XHEAD_DIM), f32),
        "cache_mem_v": jax.random.normal(next(ks), (L, DEC_BATCH, N_MEM, N_XHEADS, XHEAD_DIM), f32),
        "g_ffn1": gain((L, D_MODEL)),
        "w_ffn1_in": nrm((L, D_MODEL, 2 * D_FF), D_MODEL),
        "w_ffn1_out": nrm((L, D_FF, D_MODEL), D_FF),
        "g_mix": gain((L, D_MODEL)),
        "w_mix_in": nrm((L, D_MODEL, D_POOL + 3 * D_CONV), D_MODEL),
        "pool_w": nrm((L, N_POOL_GROUPS, POOL_GROUP, POOL_GROUP), POOL_GROUP),
        "pool_scale": gain((L, D_POOL)),
        "conv_w": nrm((L, CONV_WIDTH, D_CONV), CONV_WIDTH),
        "w_mix_out": nrm((L, D_MIX, D_MODEL), D_MIX),
        "g_xq": gain((L, D_MODEL)),
        "g_mem": gain((L, D_MODEL)),
        "w_xq": nrm((L, D_MODEL, N_XHEADS * XHEAD_DIM), D_MODEL),
        "w_xk": nrm((L, D_MODEL, N_XHEADS * XHEAD_DIM), D_MODEL),
        "w_xv": nrm((L, D_MODEL, N_XHEADS * XHEAD_DIM), D_MODEL),
        "w_xo": nrm((L, N_XHEADS * XHEAD_DIM, D_MODEL), N_XHEADS * XHEAD_DIM),
        "g_ffn2": gain((L, D_MODEL)),
        "w_ffn2_in": nrm((L, D_MODEL, 2 * D_FF), D_MODEL),
        "w_ffn2_out": nrm((L, D_FF, D_MODEL), D_FF),
        "g_final": gain((D_MODEL,)),
    }


def reference(x_prompt, x_sample, mem_prompt, state_pool, state_conv, cache_mem_k, cache_mem_v,
              g_ffn1, w_ffn1_in, w_ffn1_out, g_mix, w_mix_in, pool_w, pool_scale, conv_w, w_mix_out,
              g_xq, g_mem, w_xq, w_xk, w_xv, w_xo, g_ffn2, w_ffn2_in, w_ffn2_out, g_final):
    b_p = x_prompt.shape[0]
    yp, ys = x_prompt, x_sample
    pp, pc, pk, pv, sp, sc = [], [], [], [], [], []
    for l in range(DEPTH):
        lw = (g_ffn1[l], w_ffn1_in[l], w_ffn1_out[l], g_mix[l], w_mix_in[l], pool_w[l], pool_scale[l],
              conv_w[l], w_mix_out[l], g_xq[l], w_xq[l], w_xo[l], g_ffn2[l], w_ffn2_in[l], w_ffn2_out[l])
        mn = _rms(mem_prompt, g_mem[l])
        mk = (mn @ w_xk[l]).reshape(b_p, N_MEM, N_XHEADS, XHEAD_DIM)
        mv = (mn @ w_xv[l]).reshape(b_p, N_MEM, N_XHEADS, XHEAD_DIM)
        zp_pool = jnp.zeros((b_p, POOL_HIST, D_POOL), yp.dtype)
        zp_conv = jnp.zeros((b_p, CONV_HIST, D_CONV), yp.dtype)
        yp, npool, nconv = _layer(yp, 0, zp_pool, zp_conv, mk, mv, lw)
        pp.append(npool); pc.append(nconv); pk.append(mk); pv.append(mv)
        ys, spool, sconv = _layer(ys, PAST_LEN, state_pool[l], state_conv[l], cache_mem_k[l], cache_mem_v[l], lw)
        sp.append(spool); sc.append(sconv)
    y_prompt = _rms(yp, g_final)
    y_sample = _rms(ys, g_final)
    return (y_prompt, y_sample, jnp.stack(pp), jnp.stack(pc), jnp.stack(pk), jnp.stack(pv), jnp.stack(sp), jnp.stack(sc))
```

```python
import functools

import jax
import jax.numpy as jnp
from jax import lax
from jax.experimental import pallas as pl
from jax.experimental.pallas import tpu as pltpu

D_MODEL = 1024
D_POOL = 512
D_CONV = 512
POOL_WINDOWS = (2, 4, 8, 16)
POOL_GROUP = 128
POOL_HIST = 15
CONV_HIST = 2
D_FF = 2816
N_MEM = 256
N_XHEADS = 4
XHEAD_DIM = 256
EPS = 1e-6
PAST_LEN = 16384

LANES = 128
SUBLANES = 8
FF_CHUNK = 512
N_FF_CHUNKS = 2 * D_FF // FF_CHUNK
POOL_PAD = 16
CONV_PAD = 8
VMEM_LIMIT_BYTES = 60 * 1024 * 1024

TM_PROMPT = 512
TM_SAMPLE = 256
SEQS_PER_ATTN_STEP = 4

F32 = jnp.float32
BF16 = jnp.bfloat16


def _rms(x, g):
    y = x * lax.rsqrt(jnp.mean(x * x, axis=-1, keepdims=True) + EPS)
    return y * g


def _dot(a, b):
    return jnp.dot(a, b, preferred_element_type=F32)


def _ffn_half(x, g_ref, win_ref, wout_ref, act_ref):
    m = x.shape[0]
    h = _rms(x, g_ref[...]).astype(BF16)
    for c in range(N_FF_CHUNKS):
        gu = _dot(h, win_ref[:, c * FF_CHUNK:(c + 1) * FF_CHUNK])
        for t in range(FF_CHUNK // (2 * LANES)):
            gate = gu[:, 2 * t * LANES:(2 * t + 1) * LANES]
            up = gu[:, (2 * t + 1) * LANES:(2 * t + 2) * LANES]
            silu = gate * (1.0 / (1.0 + jnp.exp(-gate)))
            col = (c * (FF_CHUNK // (2 * LANES)) + t) * LANES
            act_ref[0:m, col:col + LANES] = (silu * up).astype(BF16)
    y = _dot(act_ref[0:m, :], wout_ref[...])
    return x + 0.5 * y


def _window_sum(x, w):
    s = x
    sh = 1
    while sh < w:
        s = s + pltpu.roll(s, sh, 0)
        sh *= 2
    return s


def _pool_group(ext, a, pos, gi, poolw_ref, pscale_ref, extract):
    w = POOL_WINDOWS[gi]
    ssum = extract(_window_sum(ext, w))
    cnt = jnp.minimum(pos + 1, w).astype(F32)
    d = (ssum / cnt - a).astype(BF16)
    y = _dot(d, poolw_ref[gi])
    return y * pscale_ref[:, gi * POOL_GROUP:(gi + 1) * POOL_GROUP]


def _conv3(ext, convw_ref, extract):
    u0 = extract(pltpu.roll(ext, 2, 0))
    u1 = extract(pltpu.roll(ext, 1, 0))
    u2 = extract(ext)
    return convw_ref[0:1, :] * u0 + convw_ref[1:2, :] * u1 + convw_ref[2:3, :] * u2


def _memkv_kernel(mem_ref, g_ref, wk_ref, wv_ref, mk_ref, mv_ref, kt_ref, vb_ref):
    mn = _rms(mem_ref[...], g_ref[...]).astype(BF16)
    k = _dot(mn, wk_ref[...])
    v = _dot(mn, wv_ref[...])
    mk_ref[...] = k
    mv_ref[...] = v
    kt_ref[...] = k.T.astype(BF16)
    vb_ref[...] = v.astype(BF16)


def _mix_prompt_kernel(x_ref, g1_ref, w1in_ref, w1out_ref, gmix_ref, wmixin_ref, poolw_ref,
                       pscale_ref, convw_ref, wmixout_ref,
                       x2_ref, npool_ref, nconv_ref,
                       act_ref, epool_ref, econv_ref, mixed_ref):
    tm = x_ref.shape[0]
    j = pl.program_id(1)

    @pl.when(j == 0)
    def _():
        epool_ref[0:POOL_PAD, :] = jnp.zeros((POOL_PAD, D_POOL), F32)
        econv_ref[0:CONV_PAD, :] = jnp.zeros((CONV_PAD, D_CONV), F32)

    x1 = _ffn_half(x_ref[...], g1_ref, w1in_ref, w1out_ref, act_ref)
    z = _dot(_rms(x1, gmix_ref[...]).astype(BF16), wmixin_ref[...])
    a = z[:, 0:D_POOL]
    cb = z[:, D_POOL:D_POOL + D_CONV]
    u = z[:, D_POOL + D_CONV:D_POOL + 2 * D_CONV] * z[:, D_POOL + 2 * D_CONV:]
    epool_ref[POOL_PAD:, :] = a
    econv_ref[CONV_PAD:, :] = u

    pos = j * tm + lax.broadcasted_iota(jnp.int32, (tm, 1), 0)
    for gi in range(len(POOL_WINDOWS)):
        sl = slice(gi * POOL_GROUP, (gi + 1) * POOL_GROUP)
        p = _pool_group(epool_ref[:, sl], a[:, sl], pos, gi, poolw_ref, pscale_ref,
                        lambda s: s[POOL_PAD:, :])
        mixed_ref[:, sl] = p.astype(BF16)
    yc = _conv3(econv_ref[...], convw_ref, lambda s: s[CONV_PAD:, :])
    mixed_ref[:, D_POOL:] = (cb * yc).astype(BF16)
    x2_ref[...] = x1 + _dot(mixed_ref[...], wmixout_ref[...])

    epool_ref[0:POOL_PAD, :] = epool_ref[tm:tm + POOL_PAD, :]
    econv_ref[0:CONV_PAD, :] = econv_ref[tm:tm + CONV_PAD, :]

    @pl.when(j == pl.num_programs(1) - 1)
    def _():
        npool_ref[...] = epool_ref[POOL_PAD - POOL_HIST:POOL_PAD, :]
        nconv_ref[...] = econv_ref[CONV_PAD - CONV_HIST:CONV_PAD, :]


def _mix_sample_kernel(x_ref, hpool_ref, hconv_ref, g1_ref, w1in_ref, w1out_ref, gmix_ref,
                       wmixin_ref, poolw_ref, pscale_ref, convw_ref, wmixout_ref,
                       x2_ref, npool_ref, nconv_ref,
                       act_ref, epool_ref, econv_ref, mixed_ref):
    tm = x_ref.shape[0]
    ns = hpool_ref.shape[0]
    t = tm // ns
    prow = POOL_PAD + t
    crow = CONV_PAD + t

    x1 = _ffn_half(x_ref[...], g1_ref, w1in_ref, w1out_ref, act_ref)
    z = _dot(_rms(x1, gmix_ref[...]).astype(BF16), wmixin_ref[...])
    a = z[:, 0:D_POOL]
    cb = z[:, D_POOL:D_POOL + D_CONV]
    u = z[:, D_POOL + D_CONV:D_POOL + 2 * D_CONV] * z[:, D_POOL + 2 * D_CONV:]

    epool_ref[:, 0:POOL_PAD - POOL_HIST, :] = jnp.zeros((ns, POOL_PAD - POOL_HIST, D_POOL), F32)
    epool_ref[:, POOL_PAD - POOL_HIST:POOL_PAD, :] = hpool_ref[...]
    epool_ref[:, POOL_PAD:, :] = a.reshape(ns, t, D_POOL)
    econv_ref[:, 0:CONV_PAD - CONV_HIST, :] = jnp.zeros((ns, CONV_PAD - CONV_HIST, D_CONV), F32)
    econv_ref[:, CONV_PAD - CONV_HIST:CONV_PAD, :] = hconv_ref[...]
    econv_ref[:, CONV_PAD:, :] = u.reshape(ns, t, D_CONV)

    pos = PAST_LEN + lax.rem(lax.broadcasted_iota(jnp.int32, (tm, 1), 0), t)
    for gi in range(len(POOL_WINDOWS)):
        sl = slice(gi * POOL_GROUP, (gi + 1) * POOL_GROUP)
        ext = epool_ref[:, :, sl].reshape(ns * prow, POOL_GROUP)
        p = _pool_group(ext, a[:, sl], pos, gi, poolw_ref, pscale_ref,
                        lambda s: s.reshape(ns, prow, POOL_GROUP)[:, POOL_PAD:, :].reshape(tm, POOL_GROUP))
        mixed_ref[:, sl] = p.astype(BF16)
    yc = _conv3(econv_ref[...].reshape(ns * crow, D_CONV), convw_ref,
                lambda s: s.reshape(ns, crow, D_CONV)[:, CONV_PAD:, :].reshape(tm, D_CONV))
    mixed_ref[:, D_POOL:] = (cb * yc).astype(BF16)
    x2_ref[...] = x1 + _dot(mixed_ref[...], wmixout_ref[...])

    npool_ref[...] = epool_ref[:, prow - POOL_HIST:prow, :]
    nconv_ref[...] = econv_ref[:, crow - CONV_HIST:crow, :]


def _softmax(s):
    e = jnp.exp(s - jnp.max(s, axis=-1, keepdims=True))
    return e / jnp.sum(e, axis=-1, keepdims=True)


def _queries(x, gq_ref, wq_ref):
    q = _dot(_rms(x, gq_ref[...]).astype(BF16), wq_ref[...])
    return (q * (XHEAD_DIM ** -0.5)).astype(BF16)


def _tail_prompt_kernel(x_ref, kt_ref, v_ref, gq_ref, wq_ref, wo_ref, g2_ref, w2in_ref, w2out_ref,
                        gf_ref, y_ref, act_ref, o_ref):
    x2 = x_ref[...]
    q = _queries(x2, gq_ref, wq_ref)
    for h in range(N_XHEADS):
        sl = slice(h * XHEAD_DIM, (h + 1) * XHEAD_DIM)
        p = _softmax(_dot(q[:, sl], kt_ref[sl, :])).astype(BF16)
        o_ref[:, sl] = _dot(p, v_ref[:, sl]).astype(BF16)
    x3 = x2 + _dot(o_ref[...], wo_ref[...])
    x4 = _ffn_half(x3, g2_ref, w2in_ref, w2out_ref, act_ref)
    y_ref[...] = _rms(x4, gf_ref[...])


def _attn_sample_kernel(x_ref, k_ref, v_ref, gq_ref, wq_ref, wo_ref, x3_ref, o_ref):
    ns = k_ref.shape[0]
    t = x_ref.shape[0] // ns
    x2 = x_ref[...]
    q = _queries(x2, gq_ref, wq_ref)
    for s in range(ns):
        rows = slice(s * t, (s + 1) * t)
        for h in range(N_XHEADS):
            sl = slice(h * XHEAD_DIM, (h + 1) * XHEAD_DIM)
            kh = k_ref[s, :, sl].astype(BF16)
            sc = lax.dot_general(q[rows, sl], kh, (((1,), (1,)), ((), ())),
                                 preferred_element_type=F32)
            p = _softmax(sc).astype(BF16)
            o_ref[rows, sl] = _dot(p, v_ref[s, :, sl].astype(BF16)).astype(BF16)
    x3_ref[...] = x2 + _dot(o_ref[...], wo_ref[...])


def _ffn_final_kernel(x_ref, g2_ref, w2in_ref, w2out_ref, gf_ref, y_ref, act_ref):
    x4 = _ffn_half(x_ref[...], g2_ref, w2in_ref, w2out_ref, act_ref)
    y_ref[...] = _rms(x4, gf_ref[...])


def _resident(arr):
    nd = arr.ndim
    return pl.BlockSpec(arr.shape, lambda *_: (0,) * nd, pipeline_mode=pl.Buffered(1))


def _params(n_grid):
    return pltpu.CompilerParams(dimension_semantics=("arbitrary",) * n_grid,
                                vmem_limit_bytes=VMEM_LIMIT_BYTES)


def _interleave_gate_up(w):
    d = w.shape[0]
    gate = w[:, :D_FF].reshape(d, D_FF // LANES, 1, LANES)
    up = w[:, D_FF:].reshape(d, D_FF // LANES, 1, LANES)
    return jnp.concatenate([gate, up], axis=2).reshape(d, 2 * D_FF).astype(BF16)


def kernel(x_prompt, x_sample, mem_prompt, state_pool, state_conv, cache_mem_k, cache_mem_v, g_ffn1, w_ffn1_in, w_ffn1_out, g_mix, w_mix_in, pool_w, pool_scale, conv_w, w_mix_out, g_xq, g_mem, w_xq, w_xk, w_xv, w_xo, g_ffn2, w_ffn2_in, w_ffn2_out, g_final):
    depth = g_ffn1.shape[0]
    assert depth == 1
    bp, seq, d = x_prompt.shape
    bs, tdec, _ = x_sample.shape
    l = 0

    w1in = _interleave_gate_up(w_ffn1_in[l])
    w1out = w_ffn1_out[l].astype(BF16)
    w2in = _interleave_gate_up(w_ffn2_in[l])
    w2out = w_ffn2_out[l].astype(BF16)
    wmixin = w_mix_in[l].astype(BF16)
    wmixout = w_mix_out[l].astype(BF16)
    poolw = pool_w[l].astype(BF16)
    wq = w_xq[l].astype(BF16)
    wk = w_xk[l].astype(BF16)
    wv = w_xv[l].astype(BF16)
    wo = w_xo[l].astype(BF16)
    g1, gmix, gq, gmem, g2 = g_ffn1[l:l + 1], g_mix[l:l + 1], g_xq[l:l + 1], g_mem[l:l + 1], g_ffn2[l:l + 1]
    gf = g_final.reshape(1, d)
    pscale = pool_scale[l:l + 1]
    convw = conv_w[l]

    mk, mv, kt, vb = pl.pallas_call(
        _memkv_kernel,
        grid=(bp,),
        in_specs=[pl.BlockSpec((None, N_MEM, d), lambda b: (b, 0, 0)),
                  _resident(gmem), _resident(wk), _resident(wv)],
        out_specs=[pl.BlockSpec((None, N_MEM, d), lambda b: (b, 0, 0)),
                   pl.BlockSpec((None, N_MEM, d), lambda b: (b, 0, 0)),
                   pl.BlockSpec((None, d, N_MEM), lambda b: (b, 0, 0)),
                   pl.BlockSpec((None, N_MEM, d), lambda b: (b, 0, 0))],
        out_shape=[jax.ShapeDtypeStruct((bp, N_MEM, d), F32),
                   jax.ShapeDtypeStruct((bp, N_MEM, d), F32),
                   jax.ShapeDtypeStruct((bp, d, N_MEM), BF16),
                   jax.ShapeDtypeStruct((bp, N_MEM, d), BF16)],
        compiler_params=_params(1),
        name="mem_kv",
    )(mem_prompt, gmem, wk, wv)

    tm = TM_PROMPT
    mix_weights = (g1, w1in, w1out, gmix, wmixin, poolw, pscale, convw, wmixout)
    x2p, npool_p, nconv_p = pl.pallas_call(
        _mix_prompt_kernel,
        grid=(bp, seq // tm),
        in_specs=[pl.BlockSpec((None, tm, d), lambda b, j: (b, j, 0))]
                 + [_resident(w) for w in mix_weights],
        out_specs=[pl.BlockSpec((None, tm, d), lambda b, j: (b, j, 0)),
                   pl.BlockSpec((None, POOL_HIST, D_POOL), lambda b, j: (b, 0, 0)),
                   pl.BlockSpec((None, CONV_HIST, D_CONV), lambda b, j: (b, 0, 0))],
        out_shape=[jax.ShapeDtypeStruct((bp, seq, d), F32),
                   jax.ShapeDtypeStruct((bp, POOL_HIST, D_POOL), F32),
                   jax.ShapeDtypeStruct((bp, CONV_HIST, D_CONV), F32)],
        scratch_shapes=[pltpu.VMEM((tm, D_FF), BF16),
                        pltpu.VMEM((tm + POOL_PAD, D_POOL), F32),
                        pltpu.VMEM((tm + CONV_PAD, D_CONV), F32),
                        pltpu.VMEM((tm, d), BF16)],
        compiler_params=_params(2),
        name="mix_prompt",
    )(x_prompt, *mix_weights)

    tail_weights = (gq, wq, wo, g2, w2in, w2out, gf)
    y_prompt = pl.pallas_call(
        _tail_prompt_kernel,
        grid=(bp, seq // tm),
        in_specs=[pl.BlockSpec((None, tm, d), lambda b, j: (b, j, 0)),
                  pl.BlockSpec((None, d, N_MEM), lambda b, j: (b, 0, 0)),
                  pl.BlockSpec((None, N_MEM, d), lambda b, j: (b, 0, 0))]
                 + [_resident(w) for w in tail_weights],
        out_specs=pl.BlockSpec((None, tm, d), lambda b, j: (b, j, 0)),
        out_shape=jax.ShapeDtypeStruct((bp, seq, d), F32),
        scratch_shapes=[pltpu.VMEM((tm, D_FF), BF16),
                        pltpu.VMEM((tm, d), BF16)],
        compiler_params=_params(2),
        name="tail_prompt",
    )(x2p, kt, vb, *tail_weights)

    tms = TM_SAMPLE
    ns = tms // tdec
    xs = x_sample.reshape(bs * tdec, d)
    x2s, npool_s, nconv_s = pl.pallas_call(
        _mix_sample_kernel,
        grid=(bs * tdec // tms,),
        in_specs=[pl.BlockSpec((tms, d), lambda i: (i, 0)),
                  pl.BlockSpec((None, ns, POOL_HIST, D_POOL), lambda i: (0, i, 0, 0)),
                  pl.BlockSpec((None, ns, CONV_HIST, D_CONV), lambda i: (0, i, 0, 0))]
                 + [_resident(w) for w in mix_weights],
        out_specs=[pl.BlockSpec((tms, d), lambda i: (i, 0)),
                   pl.BlockSpec((ns, POOL_HIST, D_POOL), lambda i: (i, 0, 0)),
                   pl.BlockSpec((ns, CONV_HIST, D_CONV), lambda i: (i, 0, 0))],
        out_shape=[jax.ShapeDtypeStruct((bs * tdec, d), F32),
                   jax.ShapeDtypeStruct((bs, POOL_HIST, D_POOL), F32),
                   jax.ShapeDtypeStruct((bs, CONV_HIST, D_CONV), F32)],
        scratch_shapes=[pltpu.VMEM((tms, D_FF), BF16),
                        pltpu.VMEM((ns, POOL_PAD + tdec, D_POOL), F32),
                        pltpu.VMEM((ns, CONV_PAD + tdec, D_CONV), F32),
                        pltpu.VMEM((tms, d), BF16)],
        compiler_params=_params(1),
        name="mix_sample",
    )(xs, state_pool, state_conv, *mix_weights)

    nsa = SEQS_PER_ATTN_STEP
    ck = cache_mem_k[l].reshape(bs, N_MEM, d)
    cv = cache_mem_v[l].reshape(bs, N_MEM, d)
    attn_weights = (gq, wq, wo)
    x3s = pl.pallas_call(
        _attn_sample_kernel,
        grid=(bs // nsa,),
        in_specs=[pl.BlockSpec((nsa * tdec, d), lambda i: (i, 0)),
                  pl.BlockSpec((nsa, N_MEM, d), lambda i: (i, 0, 0)),
                  pl.BlockSpec((nsa, N_MEM, d), lambda i: (i, 0, 0))]
                 + [_resident(w) for w in attn_weights],
        out_specs=pl.BlockSpec((nsa * tdec, d), lambda i: (i, 0)),
        out_shape=jax.ShapeDtypeStruct((bs * tdec, d), F32),
        scratch_shapes=[pltpu.VMEM((nsa * tdec, d), BF16)],
        compiler_params=_params(1),
        name="attn_sample",
    )(x2s, ck, cv, *attn_weights)

    ffn_weights = (g2, w2in, w2out, gf)
    y_sample = pl.pallas_call(
        _ffn_final_kernel,
        grid=(bs * tdec // tms,),
        in_specs=[pl.BlockSpec((tms, d), lambda i: (i, 0))] + [_resident(w) for w in ffn_weights],
        out_specs=pl.BlockSpec((tms, d), lambda i: (i, 0)),
        out_shape=jax.ShapeDtypeStruct((bs * tdec, d), F32),
        scratch_shapes=[pltpu.VMEM((tms, D_FF), BF16)],
        compiler_params=_params(1),
        name="ffn_final_sample",
    )(x3s, *ffn_weights)

    return (y_prompt,
            y_sample.reshape(bs, tdec, d),
            npool_p[None],
            nconv_p[None],
            mk.reshape(1, bp, N_MEM, N_XHEADS, XHEAD_DIM),
            mv.reshape(1, bp, N_MEM, N_XHEADS, XHEAD_DIM),
            npool_s[None],
            nconv_s[None])
```

```python
import functools

import jax
import jax.numpy as jnp
from jax import lax
from jax.experimental import pallas as pl
from jax.experimental.pallas import tpu as pltpu

D_MODEL = 1024
D_POOL = 512
D_CONV = 512
POOL_WINDOWS = (2, 4, 8, 16)
POOL_GROUP = 128
POOL_HIST = 15
CONV_HIST = 2
D_FF = 2816
N_MEM = 256
N_XHEADS = 4
XHEAD_DIM = 256
EPS = 1e-6
PAST_LEN = 16384

LANES = 128
SUBLANES = 8
FF_CHUNK = 512
FF_CHUNKS = tuple((s, min(FF_CHUNK, D_FF - s)) for s in range(0, D_FF, FF_CHUNK))
POOL_PAD = 16
CONV_PAD = 8
VMEM_LIMIT_BYTES = 60 * 1024 * 1024

TM_PROMPT = 512
TM_SAMPLE = 256
SEQS_PER_ATTN_STEP = 8

F32 = jnp.float32
BF16 = jnp.bfloat16


def _rms(x, g):
    y = x * lax.rsqrt(jnp.mean(x * x, axis=-1, keepdims=True) + EPS)
    return y * g


def _dot(a, b):
    return jnp.dot(a, b, preferred_element_type=F32)


def _ffn_half(x, g_ref, win_ref, wout_ref, act_ref):
    m = x.shape[0]
    h = _rms(x, g_ref[...]).astype(BF16)
    for s0, n in FF_CHUNKS:
        gate = _dot(h, win_ref[:, s0:s0 + n])
        up = _dot(h, win_ref[:, D_FF + s0:D_FF + s0 + n])
        silu = gate * (1.0 / (1.0 + jnp.exp(-gate)))
        act_ref[0:m, s0:s0 + n] = (silu * up).astype(BF16)
    y = _dot(act_ref[0:m, :], wout_ref[...])
    return x + 0.5 * y


def _window_sum(x, w):
    s = x
    sh = 1
    while sh < w:
        s = s + pltpu.roll(s, sh, 0)
        sh *= 2
    return s


def _pool_group(ext, a, pos, gi, poolw_ref, pscale_ref, extract):
    w = POOL_WINDOWS[gi]
    ssum = extract(_window_sum(ext, w))
    cnt = jnp.minimum(pos + 1, w).astype(F32)
    d = (ssum / cnt - a).astype(BF16)
    y = _dot(d, poolw_ref[gi])
    return y * pscale_ref[:, gi * POOL_GROUP:(gi + 1) * POOL_GROUP]


def _conv3(ext, convw_ref, extract):
    u0 = extract(pltpu.roll(ext, 2, 0))
    u1 = extract(pltpu.roll(ext, 1, 0))
    u2 = extract(ext)
    return convw_ref[0:1, :] * u0 + convw_ref[1:2, :] * u1 + convw_ref[2:3, :] * u2


def _memkv_kernel(mem_ref, g_ref, wk_ref, wv_ref, mk_ref, mv_ref, kt_ref, vb_ref):
    mn = _rms(mem_ref[...], g_ref[...]).astype(BF16)
    k = _dot(mn, wk_ref[...])
    v = _dot(mn, wv_ref[...])
    mk_ref[...] = k
    mv_ref[...] = v
    kt_ref[...] = k.T.astype(BF16)
    vb_ref[...] = v.astype(BF16)


def _mix_prompt_kernel(x_ref, g1_ref, w1in_ref, w1out_ref, gmix_ref, wmixin_ref, poolw_ref,
                       pscale_ref, convw_ref, wmixout_ref,
                       x2_ref, npool_ref, nconv_ref,
                       act_ref, epool_ref, econv_ref, mixed_ref):
    tm = x_ref.shape[0]
    j = pl.program_id(1)

    @pl.when(j == 0)
    def _():
        epool_ref[0:POOL_PAD, :] = jnp.zeros((POOL_PAD, D_POOL), F32)
        econv_ref[0:CONV_PAD, :] = jnp.zeros((CONV_PAD, D_CONV), F32)

    x1 = _ffn_half(x_ref[...], g1_ref, w1in_ref, w1out_ref, act_ref)
    z = _dot(_rms(x1, gmix_ref[...]).astype(BF16), wmixin_ref[...])
    a = z[:, 0:D_POOL]
    cb = z[:, D_POOL:D_POOL + D_CONV]
    u = z[:, D_POOL + D_CONV:D_POOL + 2 * D_CONV] * z[:, D_POOL + 2 * D_CONV:]
    epool_ref[POOL_PAD:, :] = a
    econv_ref[CONV_PAD:, :] = u

    pos = j * tm + lax.broadcasted_iota(jnp.int32, (tm, 1), 0)
    for gi in range(len(POOL_WINDOWS)):
        sl = slice(gi * POOL_GROUP, (gi + 1) * POOL_GROUP)
        p = _pool_group(epool_ref[:, sl], a[:, sl], pos, gi, poolw_ref, pscale_ref,
                        lambda s: s[POOL_PAD:, :])
        mixed_ref[:, sl] = p.astype(BF16)
    yc = _conv3(econv_ref[...], convw_ref, lambda s: s[CONV_PAD:, :])
    mixed_ref[:, D_POOL:] = (cb * yc).astype(BF16)
    x2_ref[...] = x1 + _dot(mixed_ref[...], wmixout_ref[...])

    epool_ref[0:POOL_PAD, :] = epool_ref[tm:tm + POOL_PAD, :]
    econv_ref[0:CONV_PAD, :] = econv_ref[tm:tm + CONV_PAD, :]

    @pl.when(j == pl.num_programs(1) - 1)
    def _():
        npool_ref[...] = epool_ref[POOL_PAD - POOL_HIST:POOL_PAD, :]
        nconv_ref[...] = econv_ref[CONV_PAD - CONV_HIST:CONV_PAD, :]


def _mix_sample_kernel(x_ref, hpool_ref, hconv_ref, g1_ref, w1in_ref, w1out_ref, gmix_ref,
                       wmixin_ref, poolw_ref, pscale_ref, convw_ref, wmixout_ref, gq_ref, wq_ref,
                       x2_ref, q_ref, npool_ref, nconv_ref,
                       act_ref, epool_ref, econv_ref, mixed_ref):
    tm = x_ref.shape[0]
    ns = hpool_ref.shape[0]
    t = tm // ns
    prow = POOL_PAD + t
    crow = CONV_PAD + t

    x1 = _ffn_half(x_ref[...], g1_ref, w1in_ref, w1out_ref, act_ref)
    z = _dot(_rms(x1, gmix_ref[...]).astype(BF16), wmixin_ref[...])
    a = z[:, 0:D_POOL]
    cb = z[:, D_POOL:D_POOL + D_CONV]
    u = z[:, D_POOL + D_CONV:D_POOL + 2 * D_CONV] * z[:, D_POOL + 2 * D_CONV:]

    epool_ref[:, 0:POOL_PAD - POOL_HIST, :] = jnp.zeros((ns, POOL_PAD - POOL_HIST, D_POOL), F32)
    epool_ref[:, POOL_PAD - POOL_HIST:POOL_PAD, :] = hpool_ref[...]
    epool_ref[:, POOL_PAD:, :] = a.reshape(ns, t, D_POOL)
    econv_ref[:, 0:CONV_PAD - CONV_HIST, :] = jnp.zeros((ns, CONV_PAD - CONV_HIST, D_CONV), F32)
    econv_ref[:, CONV_PAD - CONV_HIST:CONV_PAD, :] = hconv_ref[...]
    econv_ref[:, CONV_PAD:, :] = u.reshape(ns, t, D_CONV)

    pos = PAST_LEN + lax.rem(lax.broadcasted_iota(jnp.int32, (tm, 1), 0), t)
    for gi in range(len(POOL_WINDOWS)):
        sl = slice(gi * POOL_GROUP, (gi + 1) * POOL_GROUP)
        ext = epool_ref[:, :, sl].reshape(ns * prow, POOL_GROUP)
        p = _pool_group(ext, a[:, sl], pos, gi, poolw_ref, pscale_ref,
                        lambda s: s.reshape(ns, prow, POOL_GROUP)[:, POOL_PAD:, :].reshape(tm, POOL_GROUP))
        mixed_ref[:, sl] = p.astype(BF16)
    yc = _conv3(econv_ref[...].reshape(ns * crow, D_CONV), convw_ref,
                lambda s: s.reshape(ns, crow, D_CONV)[:, CONV_PAD:, :].reshape(tm, D_CONV))
    mixed_ref[:, D_POOL:] = (cb * yc).astype(BF16)
    x2 = x1 + _dot(mixed_ref[...], wmixout_ref[...])
    x2_ref[...] = x2
    q_ref[...] = _queries(x2, gq_ref, wq_ref)

    npool_ref[...] = epool_ref[:, prow - POOL_HIST:prow, :]
    nconv_ref[...] = econv_ref[:, crow - CONV_HIST:crow, :]


def _softmax(s):
    e = jnp.exp(s - jnp.max(s, axis=-1, keepdims=True))
    return e / jnp.sum(e, axis=-1, keepdims=True)


def _queries(x, gq_ref, wq_ref):
    return _dot(_rms(x, gq_ref[...]).astype(BF16), wq_ref[...]) * (XHEAD_DIM ** -0.5)


def _tail_prompt_kernel(x_ref, kt_ref, v_ref, gq_ref, wq_ref, wo_ref, g2_ref, w2in_ref, w2out_ref,
                        gf_ref, y_ref, act_ref, o_ref):
    x2 = x_ref[...]
    q = _queries(x2, gq_ref, wq_ref).astype(BF16)
    for h in range(N_XHEADS):
        sl = slice(h * XHEAD_DIM, (h + 1) * XHEAD_DIM)
        p = _softmax(_dot(q[:, sl], kt_ref[sl, :])).astype(BF16)
        o_ref[:, sl] = _dot(p, v_ref[:, sl]).astype(BF16)
    x3 = x2 + _dot(o_ref[...], wo_ref[...])
    x4 = _ffn_half(x3, g2_ref, w2in_ref, w2out_ref, act_ref)
    y_ref[...] = _rms(x4, gf_ref[...])


def _kv_copies(k_hbm, v_hbm, kbuf, vbuf, sem, step, slot):
    ns = kbuf.shape[1]
    seqs = pl.ds(step * ns, ns)
    copies = []
    for h in range(N_XHEADS):
        cols = pl.ds(h * XHEAD_DIM, XHEAD_DIM)
        copies.append(pltpu.make_async_copy(k_hbm.at[0, seqs, :, h, :], kbuf.at[slot, :, :, cols],
                                            sem.at[slot, 0, h]))
        copies.append(pltpu.make_async_copy(v_hbm.at[0, seqs, :, h, :], vbuf.at[slot, :, :, cols],
                                            sem.at[slot, 1, h]))
    return copies


def _attn_sample_kernel(q_ref, k_hbm, v_hbm, o_ref, kbuf, vbuf, sem):
    i = pl.program_id(0)
    ns = kbuf.shape[1]
    t = q_ref.shape[0] // ns
    d = q_ref.shape[1]
    slot = lax.rem(i, 2)

    @pl.when(i == 0)
    def _():
        for c in _kv_copies(k_hbm, v_hbm, kbuf, vbuf, sem, 0, 0):
            c.start()

    @pl.when(i + 1 < pl.num_programs(0))
    def _():
        for c in _kv_copies(k_hbm, v_hbm, kbuf, vbuf, sem, i + 1, 1 - slot):
            c.start()

    for c in _kv_copies(k_hbm, v_hbm, kbuf, vbuf, sem, i, slot):
        c.wait()

    rows_per_seq = N_XHEADS * t
    row_head = lax.broadcasted_iota(jnp.int32, (rows_per_seq, d), 0) // t
    col_head = lax.broadcasted_iota(jnp.int32, (rows_per_seq, d), 1) // XHEAD_DIM
    own_head = row_head == col_head
    scores = []
    for s in range(ns):
        qs = q_ref[s * t:(s + 1) * t, :]
        qe = jnp.where(own_head, jnp.concatenate([qs] * N_XHEADS, axis=0), 0.0).astype(BF16)
        scores.append(lax.dot_general(qe, kbuf[slot, s].astype(BF16), (((1,), (1,)), ((), ())),
                                      preferred_element_type=F32))
    p = _softmax(jnp.concatenate(scores, axis=0)).astype(BF16)
    for s in range(ns):
        oa = _dot(p[s * rows_per_seq:(s + 1) * rows_per_seq, :], vbuf[slot, s].astype(BF16))
        for h in range(N_XHEADS):
            sl = slice(h * XHEAD_DIM, (h + 1) * XHEAD_DIM)
            o_ref[s * t:(s + 1) * t, sl] = oa[h * t:(h + 1) * t, sl]


def _tail_sample_kernel(x_ref, o_ref, wo_ref, g2_ref, w2in_ref, w2out_ref, gf_ref, y_ref, act_ref):
    x3 = x_ref[...] + _dot(o_ref[...].astype(BF16), wo_ref[...])
    x4 = _ffn_half(x3, g2_ref, w2in_ref, w2out_ref, act_ref)
    y_ref[...] = _rms(x4, gf_ref[...])


def _resident(arr):
    nd = arr.ndim
    return pl.BlockSpec(arr.shape, lambda *_: (0,) * nd, pipeline_mode=pl.Buffered(1))


def _params(n_grid):
    return pltpu.CompilerParams(dimension_semantics=("arbitrary",) * n_grid,
                                vmem_limit_bytes=VMEM_LIMIT_BYTES)


def kernel(x_prompt, x_sample, mem_prompt, state_pool, state_conv, cache_mem_k, cache_mem_v, g_ffn1, w_ffn1_in, w_ffn1_out, g_mix, w_mix_in, pool_w, pool_scale, conv_w, w_mix_out, g_xq, g_mem, w_xq, w_xk, w_xv, w_xo, g_ffn2, w_ffn2_in, w_ffn2_out, g_final):
    depth = g_ffn1.shape[0]
    assert depth == 1
    bp, seq, d = x_prompt.shape
    bs, tdec, _ = x_sample.shape
    l = 0

    w1in = w_ffn1_in[l].astype(BF16)
    w1out = w_ffn1_out[l].astype(BF16)
    w2in = w_ffn2_in[l].astype(BF16)
    w2out = w_ffn2_out[l].astype(BF16)
    wmixin = w_mix_in[l].astype(BF16)
    wmixout = w_mix_out[l].astype(BF16)
    poolw = pool_w[l].astype(BF16)
    wq = w_xq[l].astype(BF16)
    wk = w_xk[l].astype(BF16)
    wv = w_xv[l].astype(BF16)
    wo = w_xo[l].astype(BF16)
    g1, gmix, gq, gmem, g2 = g_ffn1[l:l + 1], g_mix[l:l + 1], g_xq[l:l + 1], g_mem[l:l + 1], g_ffn2[l:l + 1]
    gf = g_final.reshape(1, d)
    pscale = pool_scale[l:l + 1]
    convw = conv_w[l]

    mk, mv, kt, vb = pl.pallas_call(
        _memkv_kernel,
        grid=(bp,),
        in_specs=[pl.BlockSpec((None, N_MEM, d), lambda b: (b, 0, 0)),
                  _resident(gmem), _resident(wk), _resident(wv)],
        out_specs=[pl.BlockSpec((None, N_MEM, d), lambda b: (b, 0, 0)),
                   pl.BlockSpec((None, N_MEM, d), lambda b: (b, 0, 0)),
                   pl.BlockSpec((None, d, N_MEM), lambda b: (b, 0, 0)),
                   pl.BlockSpec((None, N_MEM, d), lambda b: (b, 0, 0))],
        out_shape=[jax.ShapeDtypeStruct((bp, N_MEM, d), F32),
                   jax.ShapeDtypeStruct((bp, N_MEM, d), F32),
                   jax.ShapeDtypeStruct((bp, d, N_MEM), BF16),
                   jax.ShapeDtypeStruct((bp, N_MEM, d), BF16)],
        compiler_params=_params(1),
        name="mem_kv",
    )(mem_prompt, gmem, wk, wv)

    tm = TM_PROMPT
    mix_weights = (g1, w1in, w1out, gmix, wmixin, poolw, pscale, convw, wmixout)
    x2p, npool_p, nconv_p = pl.pallas_call(
        _mix_prompt_kernel,
        grid=(bp, seq // tm),
        in_specs=[pl.BlockSpec((None, tm, d), lambda b, j: (b, j, 0))]
                 + [_resident(w) for w in mix_weights],
        out_specs=[pl.BlockSpec((None, tm, d), lambda b, j: (b, j, 0)),
                   pl.BlockSpec((None, POOL_HIST, D_POOL), lambda b, j: (b, 0, 0)),
                   pl.BlockSpec((None, CONV_HIST, D_CONV), lambda b, j: (b, 0, 0))],
        out_shape=[jax.ShapeDtypeStruct((bp, seq, d), F32),
                   jax.ShapeDtypeStruct((bp, POOL_HIST, D_POOL), F32),
                   jax.ShapeDtypeStruct((bp, CONV_HIST, D_CONV), F32)],
        scratch_shapes=[pltpu.VMEM((tm, D_FF), BF16),
                        pltpu.VMEM((tm + POOL_PAD, D_POOL), F32),
                        pltpu.VMEM((tm + CONV_PAD, D_CONV), F32),
                        pltpu.VMEM((tm, d), BF16)],
        compiler_params=_params(2),
        name="mix_prompt",
    )(x_prompt, *mix_weights)

    tail_weights = (gq, wq, wo, g2, w2in, w2out, gf)
    y_prompt = pl.pallas_call(
        _tail_prompt_kernel,
        grid=(bp, seq // tm),
        in_specs=[pl.BlockSpec((None, tm, d), lambda b, j: (b, j, 0)),
                  pl.BlockSpec((None, d, N_MEM), lambda b, j: (b, 0, 0)),
                  pl.BlockSpec((None, N_MEM, d), lambda b, j: (b, 0, 0))]
                 + [_resident(w) for w in tail_weights],
        out_specs=pl.BlockSpec((None, tm, d), lambda b, j: (b, j, 0)),
        out_shape=jax.ShapeDtypeStruct((bp, seq, d), F32),
        scratch_shapes=[pltpu.VMEM((tm, D_FF), BF16),
                        pltpu.VMEM((tm, d), BF16)],
        compiler_params=_params(2),
        name="tail_prompt",
    )(x2p, kt, vb, *tail_weights)

    tms = TM_SAMPLE
    ns = tms // tdec
    xs = x_sample.reshape(bs * tdec, d)
    mix_sample_weights = mix_weights + (gq, wq)
    x2s, qs, npool_s, nconv_s = pl.pallas_call(
        _mix_sample_kernel,
        grid=(bs * tdec // tms,),
        in_specs=[pl.BlockSpec((tms, d), lambda i: (i, 0)),
                  pl.BlockSpec((None, ns, POOL_HIST, D_POOL), lambda i: (0, i, 0, 0)),
                  pl.BlockSpec((None, ns, CONV_HIST, D_CONV), lambda i: (0, i, 0, 0))]
                 + [_resident(w) for w in mix_sample_weights],
        out_specs=[pl.BlockSpec((tms, d), lambda i: (i, 0)),
                   pl.BlockSpec((tms, d), lambda i: (i, 0)),
                   pl.BlockSpec((ns, POOL_HIST, D_POOL), lambda i: (i, 0, 0)),
                   pl.BlockSpec((ns, CONV_HIST, D_CONV), lambda i: (i, 0, 0))],
        out_shape=[jax.ShapeDtypeStruct((bs * tdec, d), F32),
                   jax.ShapeDtypeStruct((bs * tdec, d), F32),
                   jax.ShapeDtypeStruct((bs, POOL_HIST, D_POOL), F32),
                   jax.ShapeDtypeStruct((bs, CONV_HIST, D_CONV), F32)],
        scratch_shapes=[pltpu.VMEM((tms, D_FF), BF16),
                        pltpu.VMEM((ns, POOL_PAD + tdec, D_POOL), F32),
                        pltpu.VMEM((ns, CONV_PAD + tdec, D_CONV), F32),
                        pltpu.VMEM((tms, d), BF16)],
        compiler_params=_params(1),
        name="mix_sample",
    )(xs, state_pool, state_conv, *mix_sample_weights)

    nsa = SEQS_PER_ATTN_STEP
    os_ = pl.pallas_call(
        _attn_sample_kernel,
        grid=(bs // nsa,),
        in_specs=[pl.BlockSpec((nsa * tdec, d), lambda i: (i, 0)),
                  pl.BlockSpec(memory_space=pl.ANY),
                  pl.BlockSpec(memory_space=pl.ANY)],
        out_specs=pl.BlockSpec((nsa * tdec, d), lambda i: (i, 0)),
        out_shape=jax.ShapeDtypeStruct((bs * tdec, d), F32),
        scratch_shapes=[pltpu.VMEM((2, nsa, N_MEM, d), F32),
                        pltpu.VMEM((2, nsa, N_MEM, d), F32),
                        pltpu.SemaphoreType.DMA((2, 2, N_XHEADS))],
        compiler_params=_params(1),
        name="attn_sample",
    )(qs, cache_mem_k, cache_mem_v)

    tail_sample_weights = (wo, g2, w2in, w2out, gf)
    y_sample = pl.pallas_call(
        _tail_sample_kernel,
        grid=(bs * tdec // tms,),
        in_specs=[pl.BlockSpec((tms, d), lambda i: (i, 0)),
                  pl.BlockSpec((tms, d), lambda i: (i, 0))]
                 + [_resident(w) for w in tail_sample_weights],
        out_specs=pl.BlockSpec((tms, d), lambda i: (i, 0)),
        out_shape=jax.ShapeDtypeStruct((bs * tdec, d), F32),
        scratch_shapes=[pltpu.VMEM((tms, D_FF), BF16)],
        compiler_params=_params(1),
        name="tail_sample",
    )(x2s, os_, *tail_sample_weights)

    return (y_prompt,
            y_sample.reshape(bs, tdec, d),
            npool_p[None],
            nconv_p[None],
            mk.reshape(1, bp, N_MEM, N_XHEADS, XHEAD_DIM),
            mv.reshape(1, bp, N_MEM, N_XHEADS, XHEAD_DIM),
            npool_s[None],
            nconv_s[None])
```

```python
import functools

import jax
import jax.numpy as jnp
from jax import lax
from jax.experimental import pallas as pl
from jax.experimental.pallas import tpu as pltpu

D_MODEL = 1024
D_POOL = 512
D_CONV = 512
POOL_WINDOWS = (2, 4, 8, 16)
POOL_GROUP = 128
POOL_HIST = 15
CONV_HIST = 2
D_FF = 2816
N_MEM = 256
N_XHEADS = 4
XHEAD_DIM = 256
EPS = 1e-6
PAST_LEN = 16384

LANES = 128
SUBLANES = 8
FF_CHUNK = 512
FF_CHUNKS = tuple((s, min(FF_CHUNK, D_FF - s)) for s in range(0, D_FF, FF_CHUNK))
POOL_PAD = 16
CONV_PAD = 8
VMEM_LIMIT_BYTES = 60 * 1024 * 1024

TM_PROMPT = 512
TM_SAMPLE = 256
SEQS_PER_ATTN_STEP = 8

F32 = jnp.float32
BF16 = jnp.bfloat16


def _rms(x, g):
    y = x * lax.rsqrt(jnp.mean(x * x, axis=-1, keepdims=True) + EPS)
    return y * g


def _dot(a, b):
    return jnp.dot(a, b, preferred_element_type=F32)


def _ffn_half(x, g_ref, win_ref, wout_ref, act_ref):
    m = x.shape[0]
    h = _rms(x, g_ref[...]).astype(BF16)
    for s0, n in FF_CHUNKS:
        gate = _dot(h, win_ref[:, s0:s0 + n])
        up = _dot(h, win_ref[:, D_FF + s0:D_FF + s0 + n])
        silu = gate * (1.0 / (1.0 + jnp.exp(-gate)))
        act_ref[0:m, s0:s0 + n] = (silu * up).astype(BF16)
    y = _dot(act_ref[0:m, :], wout_ref[...])
    return x + 0.5 * y


def _window_sum(x, w):
    s = x
    sh = 1
    while sh < w:
        s = s + pltpu.roll(s, sh, 0)
        sh *= 2
    return s


def _pool_group(ext, a, pos, gi, poolw_ref, pscale_ref, extract):
    w = POOL_WINDOWS[gi]
    ssum = extract(_window_sum(ext, w))
    cnt = jnp.minimum(pos + 1, w).astype(F32)
    d = (ssum / cnt - a).astype(BF16)
    y = _dot(d, poolw_ref[gi])
    return y * pscale_ref[:, gi * POOL_GROUP:(gi + 1) * POOL_GROUP]


def _conv3(ext, convw_ref, extract):
    u0 = extract(pltpu.roll(ext, 2, 0))
    u1 = extract(pltpu.roll(ext, 1, 0))
    u2 = extract(ext)
    return convw_ref[0:1, :] * u0 + convw_ref[1:2, :] * u1 + convw_ref[2:3, :] * u2


def _memkv_copies(kscr, vscr, mk_hbm, mv_hbm, sem, b, slot):
    copies = []
    for h in range(N_XHEADS):
        cols = pl.ds(h * XHEAD_DIM, XHEAD_DIM)
        copies.append(pltpu.make_async_copy(kscr.at[slot, :, cols], mk_hbm.at[0, b, :, h, :], sem.at[slot, 0, h]))
        copies.append(pltpu.make_async_copy(vscr.at[slot, :, cols], mv_hbm.at[0, b, :, h, :], sem.at[slot, 1, h]))
    return copies


def _memkv_kernel(mem_ref, g_ref, wk_ref, wv_ref, mk_hbm, mv_hbm, kt_ref, vb_ref, kscr, vscr, sem):
    b = pl.program_id(0)
    nb = pl.num_programs(0)
    slot = lax.rem(b, 2)

    @pl.when(b >= 2)
    def _():
        for c in _memkv_copies(kscr, vscr, mk_hbm, mv_hbm, sem, b - 2, slot):
            c.wait()

    mn = _rms(mem_ref[...], g_ref[...]).astype(BF16)
    k = _dot(mn, wk_ref[...])
    v = _dot(mn, wv_ref[...])
    kscr[slot] = k
    vscr[slot] = v
    kt_ref[...] = k.T.astype(BF16)
    vb_ref[...] = v.astype(BF16)
    for c in _memkv_copies(kscr, vscr, mk_hbm, mv_hbm, sem, b, slot):
        c.start()

    @pl.when(b == nb - 1)
    def _():
        for c in _memkv_copies(kscr, vscr, mk_hbm, mv_hbm, sem, b - 1, 1 - slot):
            c.wait()
        for c in _memkv_copies(kscr, vscr, mk_hbm, mv_hbm, sem, b, slot):
            c.wait()


def _mix_prompt_kernel(x_ref, g1_ref, w1in_ref, w1out_ref, gmix_ref, wmixin_ref, poolw_ref,
                       pscale_ref, convw_ref, wmixout_ref,
                       x2_ref, npool_ref, nconv_ref,
                       act_ref, epool_ref, econv_ref, mixed_ref):
    tm = x_ref.shape[0]
    b = pl.program_id(0)
    j = pl.program_id(1)

    @pl.when(j == 0)
    def _():
        epool_ref[0:POOL_PAD, :] = jnp.zeros((POOL_PAD, D_POOL), F32)
        econv_ref[0:CONV_PAD, :] = jnp.zeros((CONV_PAD, D_CONV), F32)

    x1 = _ffn_half(x_ref[...], g1_ref, w1in_ref, w1out_ref, act_ref)
    z = _dot(_rms(x1, gmix_ref[...]).astype(BF16), wmixin_ref[...])
    a = z[:, 0:D_POOL]
    cb = z[:, D_POOL:D_POOL + D_CONV]
    u = z[:, D_POOL + D_CONV:D_POOL + 2 * D_CONV] * z[:, D_POOL + 2 * D_CONV:]
    epool_ref[POOL_PAD:, :] = a
    econv_ref[CONV_PAD:, :] = u

    pos = j * tm + lax.broadcasted_iota(jnp.int32, (tm, 1), 0)
    for gi in range(len(POOL_WINDOWS)):
        sl = slice(gi * POOL_GROUP, (gi + 1) * POOL_GROUP)
        p = _pool_group(epool_ref[:, sl], a[:, sl], pos, gi, poolw_ref, pscale_ref,
                        lambda s: s[POOL_PAD:, :])
        mixed_ref[:, sl] = p.astype(BF16)
    yc = _conv3(econv_ref[...], convw_ref, lambda s: s[CONV_PAD:, :])
    mixed_ref[:, D_POOL:] = (cb * yc).astype(BF16)
    x2_ref[...] = x1 + _dot(mixed_ref[...], wmixout_ref[...])

    epool_ref[0:POOL_PAD, :] = epool_ref[tm:tm + POOL_PAD, :]
    econv_ref[0:CONV_PAD, :] = econv_ref[tm:tm + CONV_PAD, :]

    @pl.when(j == pl.num_programs(1) - 1)
    def _():
        for r in range(POOL_HIST):
            row = POOL_PAD - POOL_HIST + r
            npool_ref[r, pl.ds(b, 1), :] = epool_ref[row:row + 1, :]
        nconv_ref[...] = econv_ref[CONV_PAD - CONV_HIST:CONV_PAD, :]


def _mix_sample_kernel(x_ref, hpool_ref, hconv_ref, g1_ref, w1in_ref, w1out_ref, gmix_ref,
                       wmixin_ref, poolw_ref, pscale_ref, convw_ref, wmixout_ref, gq_ref, wq_ref,
                       x2_ref, q_ref, npool_ref, nconv_ref,
                       act_ref, a_ref, d_ref, econv_ref, mixed_ref):
    tm = x_ref.shape[0]
    ns = hpool_ref.shape[1]
    t = tm // ns
    crow = CONV_PAD + t

    x1 = _ffn_half(x_ref[...], g1_ref, w1in_ref, w1out_ref, act_ref)
    z = _dot(_rms(x1, gmix_ref[...]).astype(BF16), wmixin_ref[...])
    cb = z[:, D_POOL:D_POOL + D_CONV]
    u = z[:, D_POOL + D_CONV:D_POOL + 2 * D_CONV] * z[:, D_POOL + 2 * D_CONV:]

    n_ext = POOL_HIST + t
    for gi, w in enumerate(POOL_WINDOWS):
        sl = slice(gi * POOL_GROUP, (gi + 1) * POOL_GROUP)
        a_ref[gi] = z[:, sl]
        new = [a_ref[gi, pl.ds(tt, ns, stride=t), :] for tt in range(t)]
        sums = {i: hpool_ref[i, :, sl] for i in range(POOL_HIST)}
        sums.update({POOL_HIST + tt: new[tt] for tt in range(t)})
        for r in range(POOL_HIST):
            npool_ref[r, :, sl] = sums[n_ext - POOL_HIST + r]
        sh = 1
        while sh < w:
            sums = {i: sums[i] + sums[i - sh] for i in sums if (i - sh) in sums}
            sh *= 2
        for tt in range(t):
            cnt = float(min(PAST_LEN + tt + 1, w))
            d_ref[gi, pl.ds(tt, ns, stride=t), :] = sums[POOL_HIST + tt] / cnt - new[tt]
        y = _dot(d_ref[gi].astype(BF16), poolw_ref[gi])
        mixed_ref[:, sl] = (y * pscale_ref[:, sl]).astype(BF16)

    econv_ref[:, 0:CONV_PAD - CONV_HIST, :] = jnp.zeros((ns, CONV_PAD - CONV_HIST, D_CONV), F32)
    econv_ref[:, CONV_PAD - CONV_HIST:CONV_PAD, :] = hconv_ref[...]
    econv_ref[:, CONV_PAD:, :] = u.reshape(ns, t, D_CONV)
    yc = _conv3(econv_ref[...].reshape(ns * crow, D_CONV), convw_ref,
                lambda s: s.reshape(ns, crow, D_CONV)[:, CONV_PAD:, :].reshape(tm, D_CONV))
    mixed_ref[:, D_POOL:] = (cb * yc).astype(BF16)
    x2 = x1 + _dot(mixed_ref[...], wmixout_ref[...])
    x2_ref[...] = x2
    q_ref[...] = _queries(x2, gq_ref, wq_ref)
    nconv_ref[...] = econv_ref[:, crow - CONV_HIST:crow, :]


def _softmax(s):
    e = jnp.exp(s - jnp.max(s, axis=-1, keepdims=True))
    return e / jnp.sum(e, axis=-1, keepdims=True)


def _queries(x, gq_ref, wq_ref):
    return _dot(_rms(x, gq_ref[...]).astype(BF16), wq_ref[...]) * (XHEAD_DIM ** -0.5)


def _tail_prompt_kernel(x_ref, kt_ref, v_ref, gq_ref, wq_ref, wo_ref, g2_ref, w2in_ref, w2out_ref,
                        gf_ref, y_ref, act_ref, o_ref):
    x2 = x_ref[...]
    q = _queries(x2, gq_ref, wq_ref).astype(BF16)
    for h in range(N_XHEADS):
        sl = slice(h * XHEAD_DIM, (h + 1) * XHEAD_DIM)
        p = _softmax(_dot(q[:, sl], kt_ref[sl, :])).astype(BF16)
        o_ref[:, sl] = _dot(p, v_ref[:, sl]).astype(BF16)
    x3 = x2 + _dot(o_ref[...], wo_ref[...])
    x4 = _ffn_half(x3, g2_ref, w2in_ref, w2out_ref, act_ref)
    y_ref[...] = _rms(x4, gf_ref[...])


def _kv_copies(k_hbm, v_hbm, kbuf, vbuf, sem, step, slot):
    ns = kbuf.shape[1]
    seqs = pl.ds(step * ns, ns)
    copies = []
    for h in range(N_XHEADS):
        cols = pl.ds(h * XHEAD_DIM, XHEAD_DIM)
        copies.append(pltpu.make_async_copy(k_hbm.at[0, seqs, :, h, :], kbuf.at[slot, :, :, cols],
                                            sem.at[slot, 0, h]))
        copies.append(pltpu.make_async_copy(v_hbm.at[0, seqs, :, h, :], vbuf.at[slot, :, :, cols],
                                            sem.at[slot, 1, h]))
    return copies


def _attn_sample_kernel(q_ref, k_hbm, v_hbm, o_ref, kbuf, vbuf, sem):
    i = pl.program_id(0)
    ns = kbuf.shape[1]
    t = q_ref.shape[0] // ns
    d = q_ref.shape[1]
    slot = lax.rem(i, 2)

    @pl.when(i == 0)
    def _():
        for c in _kv_copies(k_hbm, v_hbm, kbuf, vbuf, sem, 0, 0):
            c.start()

    @pl.when(i + 1 < pl.num_programs(0))
    def _():
        for c in _kv_copies(k_hbm, v_hbm, kbuf, vbuf, sem, i + 1, 1 - slot):
            c.start()

    for c in _kv_copies(k_hbm, v_hbm, kbuf, vbuf, sem, i, slot):
        c.wait()

    rows_per_seq = N_XHEADS * t
    row_head = lax.broadcasted_iota(jnp.int32, (rows_per_seq, d), 0) // t
    col_head = lax.broadcasted_iota(jnp.int32, (rows_per_seq, d), 1) // XHEAD_DIM
    own_head = row_head == col_head
    scores = []
    for s in range(ns):
        qs = q_ref[s * t:(s + 1) * t, :]
        qe = jnp.where(own_head, jnp.concatenate([qs] * N_XHEADS, axis=0), 0.0).astype(BF16)
        scores.append(lax.dot_general(qe, kbuf[slot, s].astype(BF16), (((1,), (1,)), ((), ())),
                                      preferred_element_type=F32))
    p = _softmax(jnp.concatenate(scores, axis=0)).astype(BF16)
    for s in range(ns):
        oa = _dot(p[s * rows_per_seq:(s + 1) * rows_per_seq, :], vbuf[slot, s].astype(BF16))
        for h in range(N_XHEADS):
            sl = slice(h * XHEAD_DIM, (h + 1) * XHEAD_DIM)
            o_ref[s * t:(s + 1) * t, sl] = oa[h * t:(h + 1) * t, sl]


def _tail_sample_kernel(x_ref, o_ref, wo_ref, g2_ref, w2in_ref, w2out_ref, gf_ref, y_ref, act_ref):
    x3 = x_ref[...] + _dot(o_ref[...].astype(BF16), wo_ref[...])
    x4 = _ffn_half(x3, g2_ref, w2in_ref, w2out_ref, act_ref)
    y_ref[...] = _rms(x4, gf_ref[...])


def _resident(arr):
    nd = arr.ndim
    return pl.BlockSpec(arr.shape, lambda *_: (0,) * nd, pipeline_mode=pl.Buffered(1))


def _params(n_grid):
    return pltpu.CompilerParams(dimension_semantics=("arbitrary",) * n_grid,
                                vmem_limit_bytes=VMEM_LIMIT_BYTES)


def kernel(x_prompt, x_sample, mem_prompt, state_pool, state_conv, cache_mem_k, cache_mem_v, g_ffn1, w_ffn1_in, w_ffn1_out, g_mix, w_mix_in, pool_w, pool_scale, conv_w, w_mix_out, g_xq, g_mem, w_xq, w_xk, w_xv, w_xo, g_ffn2, w_ffn2_in, w_ffn2_out, g_final):
    depth = g_ffn1.shape[0]
    assert depth == 1
    bp, seq, d = x_prompt.shape
    bs, tdec, _ = x_sample.shape
    l = 0

    w1in = w_ffn1_in[l].astype(BF16)
    w1out = w_ffn1_out[l].astype(BF16)
    w2in = w_ffn2_in[l].astype(BF16)
    w2out = w_ffn2_out[l].astype(BF16)
    wmixin = w_mix_in[l].astype(BF16)
    wmixout = w_mix_out[l].astype(BF16)
    poolw = pool_w[l].astype(BF16)
    wq = w_xq[l].astype(BF16)
    wk = w_xk[l].astype(BF16)
    wv = w_xv[l].astype(BF16)
    wo = w_xo[l].astype(BF16)
    g1, gmix, gq, gmem, g2 = g_ffn1[l:l + 1], g_mix[l:l + 1], g_xq[l:l + 1], g_mem[l:l + 1], g_ffn2[l:l + 1]
    gf = g_final.reshape(1, d)
    pscale = pool_scale[l:l + 1]
    convw = conv_w[l]

    mk, mv, kt, vb = pl.pallas_call(
        _memkv_kernel,
        grid=(bp,),
        in_specs=[pl.BlockSpec((None, N_MEM, d), lambda b: (b, 0, 0)),
                  _resident(gmem), _resident(wk), _resident(wv)],
        out_specs=[pl.BlockSpec(memory_space=pl.ANY),
                   pl.BlockSpec(memory_space=pl.ANY),
                   pl.BlockSpec((None, d, N_MEM), lambda b: (b, 0, 0)),
                   pl.BlockSpec((None, N_MEM, d), lambda b: (b, 0, 0))],
        out_shape=[jax.ShapeDtypeStruct((1, bp, N_MEM, N_XHEADS, XHEAD_DIM), F32),
                   jax.ShapeDtypeStruct((1, bp, N_MEM, N_XHEADS, XHEAD_DIM), F32),
                   jax.ShapeDtypeStruct((bp, d, N_MEM), BF16),
                   jax.ShapeDtypeStruct((bp, N_MEM, d), BF16)],
        scratch_shapes=[pltpu.VMEM((2, N_MEM, d), F32),
                        pltpu.VMEM((2, N_MEM, d), F32),
                        pltpu.SemaphoreType.DMA((2, 2, N_XHEADS))],
        compiler_params=_params(1),
        name="mem_kv",
    )(mem_prompt, gmem, wk, wv)

    tm = TM_PROMPT
    mix_weights = (g1, w1in, w1out, gmix, wmixin, poolw, pscale, convw, wmixout)
    x2p, npool_p, nconv_p = pl.pallas_call(
        _mix_prompt_kernel,
        grid=(bp, seq // tm),
        in_specs=[pl.BlockSpec((None, tm, d), lambda b, j: (b, j, 0))]
                 + [_resident(w) for w in mix_weights],
        out_specs=[pl.BlockSpec((None, tm, d), lambda b, j: (b, j, 0)),
                   pl.BlockSpec((POOL_HIST, bp, D_POOL), lambda b, j: (0, 0, 0)),
                   pl.BlockSpec((None, CONV_HIST, D_CONV), lambda b, j: (b, 0, 0))],
        out_shape=[jax.ShapeDtypeStruct((bp, seq, d), F32),
                   jax.ShapeDtypeStruct((POOL_HIST, bp, D_POOL), F32),
                   jax.ShapeDtypeStruct((bp, CONV_HIST, D_CONV), F32)],
        scratch_shapes=[pltpu.VMEM((tm, D_FF), BF16),
                        pltpu.VMEM((tm + POOL_PAD, D_POOL), F32),
                        pltpu.VMEM((tm + CONV_PAD, D_CONV), F32),
                        pltpu.VMEM((tm, d), BF16)],
        compiler_params=_params(2),
        name="mix_prompt",
    )(x_prompt, *mix_weights)

    tail_weights = (gq, wq, wo, g2, w2in, w2out, gf)
    y_prompt = pl.pallas_call(
        _tail_prompt_kernel,
        grid=(bp, seq // tm),
        in_specs=[pl.BlockSpec((None, tm, d), lambda b, j: (b, j, 0)),
                  pl.BlockSpec((None, d, N_MEM), lambda b, j: (b, 0, 0)),
                  pl.BlockSpec((None, N_MEM, d), lambda b, j: (b, 0, 0))]
                 + [_resident(w) for w in tail_weights],
        out_specs=pl.BlockSpec((None, tm, d), lambda b, j: (b, j, 0)),
        out_shape=jax.ShapeDtypeStruct((bp, seq, d), F32),
        scratch_shapes=[pltpu.VMEM((tm, D_FF), BF16),
                        pltpu.VMEM((tm, d), BF16)],
        compiler_params=_params(2),
        name="tail_prompt",
    )(x2p, kt, vb, *tail_weights)

    tms = TM_SAMPLE
    ns = tms // tdec
    xs = x_sample.reshape(bs * tdec, d)
    hist_pool = jnp.transpose(state_pool[l], (1, 0, 2))
    mix_sample_weights = mix_weights + (gq, wq)
    x2s, qs, npool_s, nconv_s = pl.pallas_call(
        _mix_sample_kernel,
        grid=(bs * tdec // tms,),
        in_specs=[pl.BlockSpec((tms, d), lambda i: (i, 0)),
                  pl.BlockSpec((POOL_HIST, ns, D_POOL), lambda i: (0, i, 0)),
                  pl.BlockSpec((None, ns, CONV_HIST, D_CONV), lambda i: (0, i, 0, 0))]
                 + [_resident(w) for w in mix_sample_weights],
        out_specs=[pl.BlockSpec((tms, d), lambda i: (i, 0)),
                   pl.BlockSpec((tms, d), lambda i: (i, 0)),
                   pl.BlockSpec((POOL_HIST, ns, D_POOL), lambda i: (0, i, 0)),
                   pl.BlockSpec((ns, CONV_HIST, D_CONV), lambda i: (i, 0, 0))],
        out_shape=[jax.ShapeDtypeStruct((bs * tdec, d), F32),
                   jax.ShapeDtypeStruct((bs * tdec, d), F32),
                   jax.ShapeDtypeStruct((POOL_HIST, bs, D_POOL), F32),
                   jax.ShapeDtypeStruct((bs, CONV_HIST, D_CONV), F32)],
        scratch_shapes=[pltpu.VMEM((tms, D_FF), BF16),
                        pltpu.VMEM((len(POOL_WINDOWS), tms, POOL_GROUP), F32),
                        pltpu.VMEM((len(POOL_WINDOWS), tms, POOL_GROUP), F32),
                        pltpu.VMEM((ns, CONV_PAD + tdec, D_CONV), F32),
                        pltpu.VMEM((tms, d), BF16)],
        compiler_params=_params(1),
        name="mix_sample",
    )(xs, hist_pool, state_conv, *mix_sample_weights)

    nsa = SEQS_PER_ATTN_STEP
    os_ = pl.pallas_call(
        _attn_sample_kernel,
        grid=(bs // nsa,),
        in_specs=[pl.BlockSpec((nsa * tdec, d), lambda i: (i, 0)),
                  pl.BlockSpec(memory_space=pl.ANY),
                  pl.BlockSpec(memory_space=pl.ANY)],
        out_specs=pl.BlockSpec((nsa * tdec, d), lambda i: (i, 0)),
        out_shape=jax.ShapeDtypeStruct((bs * tdec, d), F32),
        scratch_shapes=[pltpu.VMEM((2, nsa, N_MEM, d), F32),
                        pltpu.VMEM((2, nsa, N_MEM, d), F32),
                        pltpu.SemaphoreType.DMA((2, 2, N_XHEADS))],
        compiler_params=_params(1),
        name="attn_sample",
    )(qs, cache_mem_k, cache_mem_v)

    tail_sample_weights = (wo, g2, w2in, w2out, gf)
    y_sample = pl.pallas_call(
        _tail_sample_kernel,
        grid=(bs * tdec // tms,),
        in_specs=[pl.BlockSpec((tms, d), lambda i: (i, 0)),
                  pl.BlockSpec((tms, d), lambda i: (i, 0))]
                 + [_resident(w) for w in tail_sample_weights],
        out_specs=pl.BlockSpec((tms, d), lambda i: (i, 0)),
        out_shape=jax.ShapeDtypeStruct((bs * tdec, d), F32),
        scratch_shapes=[pltpu.VMEM((tms, D_FF), BF16)],
        compiler_params=_params(1),
        name="tail_sample",
    )(x2s, os_, *tail_sample_weights)

    return (y_prompt,
            y_sample.reshape(bs, tdec, d),
            jnp.transpose(npool_p, (1, 0, 2))[None],
            nconv_p[None],
            mk,
            mv,
            jnp.transpose(npool_s, (1, 0, 2))[None],
            nconv_s[None])
```

```python
import functools

import jax
import jax.numpy as jnp
from jax import lax
from jax.experimental import pallas as pl
from jax.experimental.pallas import tpu as pltpu

D_MODEL = 1024
D_POOL = 512
D_CONV = 512
POOL_WINDOWS = (2, 4, 8, 16)
POOL_GROUP = 128
POOL_HIST = 15
CONV_HIST = 2
D_FF = 2816
N_MEM = 256
N_XHEADS = 4
XHEAD_DIM = 256
EPS = 1e-6
PAST_LEN = 16384

LANES = 128
SUBLANES = 8
FF_CHUNK = 512
FF_CHUNKS = tuple((s, min(FF_CHUNK, D_FF - s)) for s in range(0, D_FF, FF_CHUNK))
POOL_PAD = 16
CONV_PAD = 8
VMEM_LIMIT_BYTES = 60 * 1024 * 1024

TM_PROMPT = 1024
TM_SAMPLE = 256
SEQS_PER_ATTN_STEP = 8

F32 = jnp.float32
BF16 = jnp.bfloat16


def _rms(x, g):
    y = x * lax.rsqrt(jnp.mean(x * x, axis=-1, keepdims=True) + EPS)
    return y * g


def _dot(a, b):
    return jnp.dot(a, b, preferred_element_type=F32)


def _ffn_half(x, g_ref, win_ref, wout_ref, act_ref):
    m = x.shape[0]
    h = _rms(x, g_ref[...]).astype(BF16)
    for s0, n in FF_CHUNKS:
        gate = _dot(h, win_ref[:, s0:s0 + n])
        up = _dot(h, win_ref[:, D_FF + s0:D_FF + s0 + n])
        silu = gate * (1.0 / (1.0 + jnp.exp(-gate)))
        act_ref[0:m, s0:s0 + n] = (silu * up).astype(BF16)
    y = _dot(act_ref[0:m, :], wout_ref[...])
    return x + 0.5 * y


def _window_sum(x, w):
    s = x
    sh = 1
    while sh < w:
        s = s + pltpu.roll(s, sh, 0)
        sh *= 2
    return s


def _pool_group(ext, a, pos, gi, poolw_ref, pscale_ref, extract):
    w = POOL_WINDOWS[gi]
    ssum = extract(_window_sum(ext, w))
    cnt = jnp.minimum(pos + 1, w).astype(F32)
    d = (ssum / cnt - a).astype(BF16)
    y = _dot(d, poolw_ref[gi])
    return y * pscale_ref[:, gi * POOL_GROUP:(gi + 1) * POOL_GROUP]


def _conv3(ext, convw_ref, extract):
    u0 = extract(pltpu.roll(ext, 2, 0))
    u1 = extract(pltpu.roll(ext, 1, 0))
    u2 = extract(ext)
    return convw_ref[0:1, :] * u0 + convw_ref[1:2, :] * u1 + convw_ref[2:3, :] * u2


def _memkv_copies(kscr, vscr, mk_hbm, mv_hbm, sem, b, slot):
    copies = []
    for h in range(N_XHEADS):
        cols = pl.ds(h * XHEAD_DIM, XHEAD_DIM)
        copies.append(pltpu.make_async_copy(kscr.at[slot, :, cols], mk_hbm.at[0, b, :, h, :], sem.at[slot, 0, h]))
        copies.append(pltpu.make_async_copy(vscr.at[slot, :, cols], mv_hbm.at[0, b, :, h, :], sem.at[slot, 1, h]))
    return copies


def _memkv_kernel(mem_ref, g_ref, wk_ref, wv_ref, mk_hbm, mv_hbm, kt_ref, vb_ref, kscr, vscr, sem):
    b = pl.program_id(0)
    nb = pl.num_programs(0)
    slot = lax.rem(b, 2)

    @pl.when(b >= 2)
    def _():
        for c in _memkv_copies(kscr, vscr, mk_hbm, mv_hbm, sem, b - 2, slot):
            c.wait()

    mn = _rms(mem_ref[...], g_ref[...]).astype(BF16)
    k = _dot(mn, wk_ref[...])
    v = _dot(mn, wv_ref[...])
    kscr[slot] = k
    vscr[slot] = v
    kt_ref[...] = k.T.astype(BF16)
    vb_ref[...] = v.astype(BF16)
    for c in _memkv_copies(kscr, vscr, mk_hbm, mv_hbm, sem, b, slot):
        c.start()

    @pl.when(b == nb - 1)
    def _():
        for c in _memkv_copies(kscr, vscr, mk_hbm, mv_hbm, sem, b - 1, 1 - slot):
            c.wait()
        for c in _memkv_copies(kscr, vscr, mk_hbm, mv_hbm, sem, b, slot):
            c.wait()


def _mix_prompt_kernel(x_ref, g1_ref, w1in_ref, w1out_ref, gmix_ref, wmixin_ref, poolw_ref,
                       pscale_ref, convw_ref, wmixout_ref,
                       x2_ref, npool_ref, nconv_ref,
                       act_ref, epool_ref, econv_ref, mixed_ref):
    tm = x_ref.shape[0]
    b = pl.program_id(0)
    j = pl.program_id(1)

    @pl.when(j == 0)
    def _():
        epool_ref[0:POOL_PAD, :] = jnp.zeros((POOL_PAD, D_POOL), F32)
        econv_ref[0:CONV_PAD, :] = jnp.zeros((CONV_PAD, D_CONV), F32)

    x1 = _ffn_half(x_ref[...], g1_ref, w1in_ref, w1out_ref, act_ref)
    z = _dot(_rms(x1, gmix_ref[...]).astype(BF16), wmixin_ref[...])
    a = z[:, 0:D_POOL]
    cb = z[:, D_POOL:D_POOL + D_CONV]
    u = z[:, D_POOL + D_CONV:D_POOL + 2 * D_CONV] * z[:, D_POOL + 2 * D_CONV:]
    epool_ref[POOL_PAD:, :] = a
    econv_ref[CONV_PAD:, :] = u

    pos = j * tm + lax.broadcasted_iota(jnp.int32, (tm, 1), 0)
    for gi in range(len(POOL_WINDOWS)):
        sl = slice(gi * POOL_GROUP, (gi + 1) * POOL_GROUP)
        p = _pool_group(epool_ref[:, sl], a[:, sl], pos, gi, poolw_ref, pscale_ref,
                        lambda s: s[POOL_PAD:, :])
        mixed_ref[:, sl] = p.astype(BF16)
    yc = _conv3(econv_ref[...], convw_ref, lambda s: s[CONV_PAD:, :])
    mixed_ref[:, D_POOL:] = (cb * yc).astype(BF16)
    x2_ref[...] = x1 + _dot(mixed_ref[...], wmixout_ref[...])

    epool_ref[0:POOL_PAD, :] = epool_ref[tm:tm + POOL_PAD, :]
    econv_ref[0:CONV_PAD, :] = econv_ref[tm:tm + CONV_PAD, :]

    @pl.when(j == pl.num_programs(1) - 1)
    def _():
        for r in range(POOL_HIST):
            row = POOL_PAD - POOL_HIST + r
            npool_ref[r, pl.ds(b, 1), :] = epool_ref[row:row + 1, :]
        nconv_ref[...] = econv_ref[CONV_PAD - CONV_HIST:CONV_PAD, :]


def _mix_sample_kernel(x_ref, hpool_ref, hconv_ref, g1_ref, w1in_ref, w1out_ref, gmix_ref,
                       wmixin_ref, poolw_ref, pscale_ref, convw_ref, wmixout_ref, gq_ref, wq_ref,
                       x2_ref, q_ref, npool_ref, nconv_ref,
                       act_ref, a_ref, d_ref, econv_ref, mixed_ref):
    tm = x_ref.shape[0]
    ns = hpool_ref.shape[1]
    t = tm // ns
    crow = CONV_PAD + t

    x1 = _ffn_half(x_ref[...], g1_ref, w1in_ref, w1out_ref, act_ref)
    z = _dot(_rms(x1, gmix_ref[...]).astype(BF16), wmixin_ref[...])
    cb = z[:, D_POOL:D_POOL + D_CONV]
    u = z[:, D_POOL + D_CONV:D_POOL + 2 * D_CONV] * z[:, D_POOL + 2 * D_CONV:]

    n_ext = POOL_HIST + t
    for gi, w in enumerate(POOL_WINDOWS):
        sl = slice(gi * POOL_GROUP, (gi + 1) * POOL_GROUP)
        a_ref[gi] = z[:, sl]
        new = [a_ref[gi, pl.ds(tt, ns, stride=t), :] for tt in range(t)]
        sums = {i: hpool_ref[i, :, sl] for i in range(POOL_HIST)}
        sums.update({POOL_HIST + tt: new[tt] for tt in range(t)})
        for r in range(POOL_HIST):
            npool_ref[r, :, sl] = sums[n_ext - POOL_HIST + r]
        sh = 1
        while sh < w:
            sums = {i: sums[i] + sums[i - sh] for i in sums if (i - sh) in sums}
            sh *= 2
        for tt in range(t):
            cnt = float(min(PAST_LEN + tt + 1, w))
            d_ref[gi, pl.ds(tt, ns, stride=t), :] = sums[POOL_HIST + tt] / cnt - new[tt]
        y = _dot(d_ref[gi].astype(BF16), poolw_ref[gi])
        mixed_ref[:, sl] = (y * pscale_ref[:, sl]).astype(BF16)

    econv_ref[:, 0:CONV_PAD - CONV_HIST, :] = jnp.zeros((ns, CONV_PAD - CONV_HIST, D_CONV), F32)
    econv_ref[:, CONV_PAD - CONV_HIST:CONV_PAD, :] = hconv_ref[...]
    econv_ref[:, CONV_PAD:, :] = u.reshape(ns, t, D_CONV)
    yc = _conv3(econv_ref[...].reshape(ns * crow, D_CONV), convw_ref,
                lambda s: s.reshape(ns, crow, D_CONV)[:, CONV_PAD:, :].reshape(tm, D_CONV))
    mixed_ref[:, D_POOL:] = (cb * yc).astype(BF16)
    x2 = x1 + _dot(mixed_ref[...], wmixout_ref[...])
    x2_ref[...] = x2
    q_ref[...] = _queries(x2, gq_ref, wq_ref)
    nconv_ref[...] = econv_ref[:, crow - CONV_HIST:crow, :]


def _softmax(s):
    e = jnp.exp(s - jnp.max(s, axis=-1, keepdims=True))
    return e / jnp.sum(e, axis=-1, keepdims=True)


def _queries(x, gq_ref, wq_ref):
    return _dot(_rms(x, gq_ref[...]).astype(BF16), wq_ref[...]) * (XHEAD_DIM ** -0.5)


def _tail_prompt_kernel(x_ref, kt_ref, v_ref, gq_ref, wq_ref, wo_ref, g2_ref, w2in_ref, w2out_ref,
                        gf_ref, y_ref, act_ref, o_ref):
    x2 = x_ref[...]
    q = _queries(x2, gq_ref, wq_ref).astype(BF16)
    for h in range(N_XHEADS):
        sl = slice(h * XHEAD_DIM, (h + 1) * XHEAD_DIM)
        p = _softmax(_dot(q[:, sl], kt_ref[sl, :])).astype(BF16)
        o_ref[:, sl] = _dot(p, v_ref[:, sl]).astype(BF16)
    x3 = x2 + _dot(o_ref[...], wo_ref[...])
    x4 = _ffn_half(x3, g2_ref, w2in_ref, w2out_ref, act_ref)
    y_ref[...] = _rms(x4, gf_ref[...])


def _kv_copies(k_hbm, v_hbm, kbuf, vbuf, sem, step, slot):
    ns = kbuf.shape[1]
    seqs = pl.ds(step * ns, ns)
    copies = []
    for h in range(N_XHEADS):
        cols = pl.ds(h * XHEAD_DIM, XHEAD_DIM)
        copies.append(pltpu.make_async_copy(k_hbm.at[0, seqs, :, h, :], kbuf.at[slot, :, :, cols],
                                            sem.at[slot, 0, h]))
        copies.append(pltpu.make_async_copy(v_hbm.at[0, seqs, :, h, :], vbuf.at[slot, :, :, cols],
                                            sem.at[slot, 1, h]))
    return copies


def _attn_sample_kernel(q_ref, k_hbm, v_hbm, o_ref, kbuf, vbuf, sem):
    i = pl.program_id(0)
    ns = kbuf.shape[1]
    t = q_ref.shape[0] // ns
    d = q_ref.shape[1]
    slot = lax.rem(i, 2)

    @pl.when(i == 0)
    def _():
        for c in _kv_copies(k_hbm, v_hbm, kbuf, vbuf, sem, 0, 0):
            c.start()

    @pl.when(i + 1 < pl.num_programs(0))
    def _():
        for c in _kv_copies(k_hbm, v_hbm, kbuf, vbuf, sem, i + 1, 1 - slot):
            c.start()

    for c in _kv_copies(k_hbm, v_hbm, kbuf, vbuf, sem, i, slot):
        c.wait()

    rows_per_seq = N_XHEADS * t
    row_head = lax.broadcasted_iota(jnp.int32, (rows_per_seq, d), 0) // t
    col_head = lax.broadcasted_iota(jnp.int32, (rows_per_seq, d), 1) // XHEAD_DIM
    own_head = row_head == col_head
    scores = []
    for s in range(ns):
        qs = q_ref[s * t:(s + 1) * t, :]
        qe = jnp.where(own_head, jnp.concatenate([qs] * N_XHEADS, axis=0), 0.0).astype(BF16)
        scores.append(lax.dot_general(qe, kbuf[slot, s].astype(BF16), (((1,), (1,)), ((), ())),
                                      preferred_element_type=F32))
    p = _softmax(jnp.concatenate(scores, axis=0)).astype(BF16)
    for s in range(ns):
        oa = _dot(p[s * rows_per_seq:(s + 1) * rows_per_seq, :], vbuf[slot, s].astype(BF16))
        for h in range(N_XHEADS):
            sl = slice(h * XHEAD_DIM, (h + 1) * XHEAD_DIM)
            o_ref[s * t:(s + 1) * t, sl] = oa[h * t:(h + 1) * t, sl]


def _tail_sample_kernel(x_ref, o_ref, wo_ref, g2_ref, w2in_ref, w2out_ref, gf_ref, y_ref, act_ref):
    x3 = x_ref[...] + _dot(o_ref[...].astype(BF16), wo_ref[...])
    x4 = _ffn_half(x3, g2_ref, w2in_ref, w2out_ref, act_ref)
    y_ref[...] = _rms(x4, gf_ref[...])


def _resident(arr):
    nd = arr.ndim
    return pl.BlockSpec(arr.shape, lambda *_: (0,) * nd, pipeline_mode=pl.Buffered(1))


def _params(n_grid):
    return pltpu.CompilerParams(dimension_semantics=("arbitrary",) * n_grid,
                                vmem_limit_bytes=VMEM_LIMIT_BYTES)


def kernel(x_prompt, x_sample, mem_prompt, state_pool, state_conv, cache_mem_k, cache_mem_v, g_ffn1, w_ffn1_in, w_ffn1_out, g_mix, w_mix_in, pool_w, pool_scale, conv_w, w_mix_out, g_xq, g_mem, w_xq, w_xk, w_xv, w_xo, g_ffn2, w_ffn2_in, w_ffn2_out, g_final):
    depth = g_ffn1.shape[0]
    assert depth == 1
    bp, seq, d = x_prompt.shape
    bs, tdec, _ = x_sample.shape
    l = 0

    w1in = w_ffn1_in[l].astype(BF16)
    w1out = w_ffn1_out[l].astype(BF16)
    w2in = w_ffn2_in[l].astype(BF16)
    w2out = w_ffn2_out[l].astype(BF16)
    wmixin = w_mix_in[l].astype(BF16)
    wmixout = w_mix_out[l].astype(BF16)
    poolw = pool_w[l].astype(BF16)
    wq = w_xq[l].astype(BF16)
    wk = w_xk[l].astype(BF16)
    wv = w_xv[l].astype(BF16)
    wo = w_xo[l].astype(BF16)
    g1, gmix, gq, gmem, g2 = g_ffn1[l:l + 1], g_mix[l:l + 1], g_xq[l:l + 1], g_mem[l:l + 1], g_ffn2[l:l + 1]
    gf = g_final.reshape(1, d)
    pscale = pool_scale[l:l + 1]
    convw = conv_w[l]

    mk, mv, kt, vb = pl.pallas_call(
        _memkv_kernel,
        grid=(bp,),
        in_specs=[pl.BlockSpec((None, N_MEM, d), lambda b: (b, 0, 0)),
                  _resident(gmem), _resident(wk), _resident(wv)],
        out_specs=[pl.BlockSpec(memory_space=pl.ANY),
                   pl.BlockSpec(memory_space=pl.ANY),
                   pl.BlockSpec((None, d, N_MEM), lambda b: (b, 0, 0)),
                   pl.BlockSpec((None, N_MEM, d), lambda b: (b, 0, 0))],
        out_shape=[jax.ShapeDtypeStruct((1, bp, N_MEM, N_XHEADS, XHEAD_DIM), F32),
                   jax.ShapeDtypeStruct((1, bp, N_MEM, N_XHEADS, XHEAD_DIM), F32),
                   jax.ShapeDtypeStruct((bp, d, N_MEM), BF16),
                   jax.ShapeDtypeStruct((bp, N_MEM, d), BF16)],
        scratch_shapes=[pltpu.VMEM((2, N_MEM, d), F32),
                        pltpu.VMEM((2, N_MEM, d), F32),
                        pltpu.SemaphoreType.DMA((2, 2, N_XHEADS))],
        compiler_params=_params(1),
        name="mem_kv",
    )(mem_prompt, gmem, wk, wv)

    tm = TM_PROMPT
    mix_weights = (g1, w1in, w1out, gmix, wmixin, poolw, pscale, convw, wmixout)
    x2p, npool_p, nconv_p = pl.pallas_call(
        _mix_prompt_kernel,
        grid=(bp, seq // tm),
        in_specs=[pl.BlockSpec((None, tm, d), lambda b, j: (b, j, 0))]
                 + [_resident(w) for w in mix_weights],
        out_specs=[pl.BlockSpec((None, tm, d), lambda b, j: (b, j, 0)),
                   pl.BlockSpec((POOL_HIST, bp, D_POOL), lambda b, j: (0, 0, 0)),
                   pl.BlockSpec((None, CONV_HIST, D_CONV), lambda b, j: (b, 0, 0))],
        out_shape=[jax.ShapeDtypeStruct((bp, seq, d), F32),
                   jax.ShapeDtypeStruct((POOL_HIST, bp, D_POOL), F32),
                   jax.ShapeDtypeStruct((bp, CONV_HIST, D_CONV), F32)],
        scratch_shapes=[pltpu.VMEM((tm, D_FF), BF16),
                        pltpu.VMEM((tm + POOL_PAD, D_POOL), F32),
                        pltpu.VMEM((tm + CONV_PAD, D_CONV), F32),
                        pltpu.VMEM((tm, d), BF16)],
        compiler_params=_params(2),
        name="mix_prompt",
    )(x_prompt, *mix_weights)

    tail_weights = (gq, wq, wo, g2, w2in, w2out, gf)
    y_prompt = pl.pallas_call(
        _tail_prompt_kernel,
        grid=(bp, seq // tm),
        in_specs=[pl.BlockSpec((None, tm, d), lambda b, j: (b, j, 0)),
                  pl.BlockSpec((None, d, N_MEM), lambda b, j: (b, 0, 0)),
                  pl.BlockSpec((None, N_MEM, d), lambda b, j: (b, 0, 0))]
                 + [_resident(w) for w in tail_weights],
        out_specs=pl.BlockSpec((None, tm, d), lambda b, j: (b, j, 0)),
        out_shape=jax.ShapeDtypeStruct((bp, seq, d), F32),
        scratch_shapes=[pltpu.VMEM((tm, D_FF), BF16),
                        pltpu.VMEM((tm, d), BF16)],
        compiler_params=_params(2),
        name="tail_prompt",
    )(x2p, kt, vb, *tail_weights)

    tms = TM_SAMPLE
    ns = tms // tdec
    xs = x_sample.reshape(bs * tdec, d)
    hist_pool = jnp.transpose(state_pool[l], (1, 0, 2))
    mix_sample_weights = mix_weights + (gq, wq)
    x2s, qs, npool_s, nconv_s = pl.pallas_call(
        _mix_sample_kernel,
        grid=(bs * tdec // tms,),
        in_specs=[pl.BlockSpec((tms, d), lambda i: (i, 0)),
                  pl.BlockSpec((POOL_HIST, ns, D_POOL), lambda i: (0, i, 0)),
                  pl.BlockSpec((None, ns, CONV_HIST, D_CONV), lambda i: (0, i, 0, 0))]
                 + [_resident(w) for w in mix_sample_weights],
        out_specs=[pl.BlockSpec((tms, d), lambda i: (i, 0)),
                   pl.BlockSpec((tms, d), lambda i: (i, 0)),
                   pl.BlockSpec((POOL_HIST, ns, D_POOL), lambda i: (0, i, 0)),
                   pl.BlockSpec((ns, CONV_HIST, D_CONV), lambda i: (i, 0, 0))],
        out_shape=[jax.ShapeDtypeStruct((bs * tdec, d), F32),
                   jax.ShapeDtypeStruct((bs * tdec, d), F32),
                   jax.ShapeDtypeStruct((POOL_HIST, bs, D_POOL), F32),
                   jax.ShapeDtypeStruct((bs, CONV_HIST, D_CONV), F32)],
        scratch_shapes=[pltpu.VMEM((tms, D_FF), BF16),
                        pltpu.VMEM((len(POOL_WINDOWS), tms, POOL_GROUP), F32),
                        pltpu.VMEM((len(POOL_WINDOWS), tms, POOL_GROUP), F32),
                        pltpu.VMEM((ns, CONV_PAD + tdec, D_CONV), F32),
                        pltpu.VMEM((tms, d), BF16)],
        compiler_params=_params(1),
        name="mix_sample",
    )(xs, hist_pool, state_conv, *mix_sample_weights)

    nsa = SEQS_PER_ATTN_STEP
    os_ = pl.pallas_call(
        _attn_sample_kernel,
        grid=(bs // nsa,),
        in_specs=[pl.BlockSpec((nsa * tdec, d), lambda i: (i, 0)),
                  pl.BlockSpec(memory_space=pl.ANY),
                  pl.BlockSpec(memory_space=pl.ANY)],
        out_specs=pl.BlockSpec((nsa * tdec, d), lambda i: (i, 0)),
        out_shape=jax.ShapeDtypeStruct((bs * tdec, d), F32),
        scratch_shapes=[pltpu.VMEM((2, nsa, N_MEM, d), F32),
                        pltpu.VMEM((2, nsa, N_MEM, d), F32),
                        pltpu.SemaphoreType.DMA((2, 2, N_XHEADS))],
        compiler_params=_params(1),
        name="attn_sample",
    )(qs, cache_mem_k, cache_mem_v)

    tail_sample_weights = (wo, g2, w2in, w2out, gf)
    y_sample = pl.pallas_call(
        _tail_sample_kernel,
        grid=(bs * tdec // tms,),
        in_specs=[pl.BlockSpec((tms, d), lambda i: (i, 0)),
                  pl.BlockSpec((tms, d), lambda i: (i, 0))]
                 + [_resident(w) for w in tail_sample_weights],
        out_specs=pl.BlockSpec((tms, d), lambda i: (i, 0)),
        out_shape=jax.ShapeDtypeStruct((bs * tdec, d), F32),
        scratch_shapes=[pltpu.VMEM((tms, D_FF), BF16)],
        compiler_params=_params(1),
        name="tail_sample",
    )(x2s, os_, *tail_sample_weights)

    return (y_prompt,
            y_sample.reshape(bs, tdec, d),
            jnp.transpose(npool_p, (1, 0, 2))[None],
            nconv_p[None],
            mk,
            mv,
            jnp.transpose(npool_s, (1, 0, 2))[None],
            nconv_s[None])
```

```python
import functools

import jax
import jax.numpy as jnp
from jax import lax
from jax.experimental import pallas as pl
from jax.experimental.pallas import tpu as pltpu

D_MODEL = 1024
D_POOL = 512
D_CONV = 512
POOL_WINDOWS = (2, 4, 8, 16)
POOL_GROUP = 128
POOL_HIST = 15
CONV_HIST = 2
D_FF = 2816
N_MEM = 256
N_XHEADS = 4
XHEAD_DIM = 256
EPS = 1e-6
PAST_LEN = 16384

LANES = 128
SUBLANES = 8
FF_CHUNK = 512
FF_CHUNKS = tuple((s, min(FF_CHUNK, D_FF - s)) for s in range(0, D_FF, FF_CHUNK))
POOL_PAD = 16
CONV_PAD = 8
VMEM_LIMIT_BYTES = 60 * 1024 * 1024

TM_PROMPT = 512
TM_SAMPLE = 256

F32 = jnp.float32
BF16 = jnp.bfloat16


def _rms(x, g):
    y = x * lax.rsqrt(jnp.mean(x * x, axis=-1, keepdims=True) + EPS)
    return y * g


def _dot(a, b):
    return jnp.dot(a, b, preferred_element_type=F32)


def _ffn_half(x, g_ref, win_ref, wout_ref, act_ref, between_phases=None):
    m = x.shape[0]
    h = _rms(x, g_ref[...]).astype(BF16)
    for s0, n in FF_CHUNKS:
        gate = _dot(h, win_ref[:, s0:s0 + n])
        up = _dot(h, win_ref[:, D_FF + s0:D_FF + s0 + n])
        silu = gate * (1.0 / (1.0 + jnp.exp(-gate)))
        act_ref[0:m, s0:s0 + n] = (silu * up).astype(BF16)
    if between_phases is not None:
        between_phases()
    y = _dot(act_ref[0:m, :], wout_ref[...])
    return x + 0.5 * y


def _window_sum(x, w):
    s = x
    sh = 1
    while sh < w:
        s = s + pltpu.roll(s, sh, 0)
        sh *= 2
    return s


def _pool_group(ext, a, pos, gi, poolw_ref, pscale_ref, extract):
    w = POOL_WINDOWS[gi]
    ssum = extract(_window_sum(ext, w))
    cnt = jnp.minimum(pos + 1, w).astype(F32)
    d = (ssum / cnt - a).astype(BF16)
    y = _dot(d, poolw_ref[gi])
    return y * pscale_ref[:, gi * POOL_GROUP:(gi + 1) * POOL_GROUP]


def _conv3(ext, convw_ref, extract):
    u0 = extract(pltpu.roll(ext, 2, 0))
    u1 = extract(pltpu.roll(ext, 1, 0))
    u2 = extract(ext)
    return convw_ref[0:1, :] * u0 + convw_ref[1:2, :] * u1 + convw_ref[2:3, :] * u2


def _memkv_copies(kscr, vscr, mk_hbm, mv_hbm, sem, b, slot):
    copies = []
    for h in range(N_XHEADS):
        cols = pl.ds(h * XHEAD_DIM, XHEAD_DIM)
        copies.append(pltpu.make_async_copy(kscr.at[slot, :, cols], mk_hbm.at[0, b, :, h, :], sem.at[slot, 0, h]))
        copies.append(pltpu.make_async_copy(vscr.at[slot, :, cols], mv_hbm.at[0, b, :, h, :], sem.at[slot, 1, h]))
    return copies


def _memkv_kernel(mem_ref, g_ref, wk_ref, wv_ref, mk_hbm, mv_hbm, kt_ref, vb_ref, kscr, vscr, sem):
    b = pl.program_id(0)
    nb = pl.num_programs(0)
    slot = lax.rem(b, 2)

    @pl.when(b >= 2)
    def _():
        for c in _memkv_copies(kscr, vscr, mk_hbm, mv_hbm, sem, b - 2, slot):
            c.wait()

    mn = _rms(mem_ref[...], g_ref[...]).astype(BF16)
    k = _dot(mn, wk_ref[...])
    v = _dot(mn, wv_ref[...])
    kscr[slot] = k
    vscr[slot] = v
    kt_ref[...] = k.T.astype(BF16)
    vb_ref[...] = v.astype(BF16)
    for c in _memkv_copies(kscr, vscr, mk_hbm, mv_hbm, sem, b, slot):
        c.start()

    @pl.when(b == nb - 1)
    def _():
        for c in _memkv_copies(kscr, vscr, mk_hbm, mv_hbm, sem, b - 1, 1 - slot):
            c.wait()
        for c in _memkv_copies(kscr, vscr, mk_hbm, mv_hbm, sem, b, slot):
            c.wait()


def _mix_prompt_kernel(x_ref, g1_ref, w1in_ref, w1out_ref, gmix_ref, wmixin_ref, poolw_ref,
                       pscale_ref, convw_ref, wmixout_ref,
                       x2_ref, npool_ref, nconv_ref,
                       act_ref, epool_ref, econv_ref, mixed_ref):
    tm = x_ref.shape[0]
    b = pl.program_id(0)
    j = pl.program_id(1)

    @pl.when(j == 0)
    def _():
        epool_ref[0:POOL_PAD, :] = jnp.zeros((POOL_PAD, D_POOL), F32)
        econv_ref[0:CONV_PAD, :] = jnp.zeros((CONV_PAD, D_CONV), F32)

    x1 = _ffn_half(x_ref[...], g1_ref, w1in_ref, w1out_ref, act_ref)
    z = _dot(_rms(x1, gmix_ref[...]).astype(BF16), wmixin_ref[...])
    a = z[:, 0:D_POOL]
    cb = z[:, D_POOL:D_POOL + D_CONV]
    u = z[:, D_POOL + D_CONV:D_POOL + 2 * D_CONV] * z[:, D_POOL + 2 * D_CONV:]
    epool_ref[POOL_PAD:, :] = a
    econv_ref[CONV_PAD:, :] = u

    pos = j * tm + lax.broadcasted_iota(jnp.int32, (tm, 1), 0)
    for gi in range(len(POOL_WINDOWS)):
        sl = slice(gi * POOL_GROUP, (gi + 1) * POOL_GROUP)
        p = _pool_group(epool_ref[:, sl], a[:, sl], pos, gi, poolw_ref, pscale_ref,
                        lambda s: s[POOL_PAD:, :])
        mixed_ref[:, sl] = p.astype(BF16)
    yc = _conv3(econv_ref[...], convw_ref, lambda s: s[CONV_PAD:, :])
    mixed_ref[:, D_POOL:] = (cb * yc).astype(BF16)
    x2_ref[...] = x1 + _dot(mixed_ref[...], wmixout_ref[...])

    epool_ref[0:POOL_PAD, :] = epool_ref[tm:tm + POOL_PAD, :]
    econv_ref[0:CONV_PAD, :] = econv_ref[tm:tm + CONV_PAD, :]

    @pl.when(j == pl.num_programs(1) - 1)
    def _():
        for r in range(POOL_HIST):
            row = POOL_PAD - POOL_HIST + r
            npool_ref[r, pl.ds(b, 1), :] = epool_ref[row:row + 1, :]
        nconv_ref[...] = econv_ref[CONV_PAD - CONV_HIST:CONV_PAD, :]


def _mix_sample_kernel(x_ref, hpool_ref, hconv_ref, g1_ref, w1in_ref, w1out_ref, gmix_ref,
                       wmixin_ref, poolw_ref, pscale_ref, convw_ref, wmixout_ref, gq_ref, wq_ref,
                       x2_ref, q_ref, npool_ref, nconv_ref,
                       act_ref, a_ref, d_ref, econv_ref, mixed_ref):
    tm = x_ref.shape[0]
    ns = hpool_ref.shape[1]
    t = tm // ns
    crow = CONV_PAD + t

    x1 = _ffn_half(x_ref[...], g1_ref, w1in_ref, w1out_ref, act_ref)
    z = _dot(_rms(x1, gmix_ref[...]).astype(BF16), wmixin_ref[...])
    cb = z[:, D_POOL:D_POOL + D_CONV]
    u = z[:, D_POOL + D_CONV:D_POOL + 2 * D_CONV] * z[:, D_POOL + 2 * D_CONV:]

    n_ext = POOL_HIST + t
    for gi, w in enumerate(POOL_WINDOWS):
        sl = slice(gi * POOL_GROUP, (gi + 1) * POOL_GROUP)
        a_ref[gi] = z[:, sl]
        new = [a_ref[gi, pl.ds(tt, ns, stride=t), :] for tt in range(t)]
        sums = {i: hpool_ref[i, :, sl] for i in range(POOL_HIST)}
        sums.update({POOL_HIST + tt: new[tt] for tt in range(t)})
        for r in range(POOL_HIST):
            npool_ref[r, :, sl] = sums[n_ext - POOL_HIST + r]
        sh = 1
        while sh < w:
            sums = {i: sums[i] + sums[i - sh] for i in sums if (i - sh) in sums}
            sh *= 2
        for tt in range(t):
            cnt = float(min(PAST_LEN + tt + 1, w))
            d_ref[gi, pl.ds(tt, ns, stride=t), :] = sums[POOL_HIST + tt] / cnt - new[tt]
        y = _dot(d_ref[gi].astype(BF16), poolw_ref[gi])
        mixed_ref[:, sl] = (y * pscale_ref[:, sl]).astype(BF16)

    econv_ref[:, 0:CONV_PAD - CONV_HIST, :] = jnp.zeros((ns, CONV_PAD - CONV_HIST, D_CONV), F32)
    econv_ref[:, CONV_PAD - CONV_HIST:CONV_PAD, :] = hconv_ref[...]
    econv_ref[:, CONV_PAD:, :] = u.reshape(ns, t, D_CONV)
    yc = _conv3(econv_ref[...].reshape(ns * crow, D_CONV), convw_ref,
                lambda s: s.reshape(ns, crow, D_CONV)[:, CONV_PAD:, :].reshape(tm, D_CONV))
    mixed_ref[:, D_POOL:] = (cb * yc).astype(BF16)
    x2 = x1 + _dot(mixed_ref[...], wmixout_ref[...])
    x2_ref[...] = x2
    q_ref[...] = _queries(x2, gq_ref, wq_ref)
    nconv_ref[...] = econv_ref[:, crow - CONV_HIST:crow, :]


def _softmax(s):
    e = jnp.exp(s - jnp.max(s, axis=-1, keepdims=True))
    return e / jnp.sum(e, axis=-1, keepdims=True)


def _queries(x, gq_ref, wq_ref):
    return _dot(_rms(x, gq_ref[...]).astype(BF16), wq_ref[...]) * (XHEAD_DIM ** -0.5)


def _kv_copies(k_hbm, v_hbm, kbuf, vbuf, sem, chunk, slot):
    ns = kbuf.shape[1]
    seqs = pl.ds(chunk * ns, ns)
    copies = []
    for h in range(N_XHEADS):
        cols = pl.ds(h * XHEAD_DIM, XHEAD_DIM)
        copies.append(pltpu.make_async_copy(k_hbm.at[0, seqs, :, h, :], kbuf.at[slot, :, :, cols],
                                            sem.at[slot, 0, h]))
        copies.append(pltpu.make_async_copy(v_hbm.at[0, seqs, :, h, :], vbuf.at[slot, :, :, cols],
                                            sem.at[slot, 1, h]))
    return copies


def _attend_cached(q_ref, o_ref, row0, t, kbuf, vbuf, slot):
    ns = kbuf.shape[1]
    d = q_ref.shape[1]
    rows_per_seq = N_XHEADS * t
    row_head = lax.broadcasted_iota(jnp.int32, (rows_per_seq, d), 0) // t
    col_head = lax.broadcasted_iota(jnp.int32, (rows_per_seq, d), 1) // XHEAD_DIM
    own_head = row_head == col_head
    scores = []
    for s in range(ns):
        qs = q_ref[row0 + s * t:row0 + (s + 1) * t, :]
        qe = jnp.where(own_head, jnp.concatenate([qs] * N_XHEADS, axis=0), 0.0).astype(BF16)
        scores.append(lax.dot_general(qe, kbuf[slot, s].astype(BF16), (((1,), (1,)), ((), ())),
                                      preferred_element_type=F32))
    p = _softmax(jnp.concatenate(scores, axis=0)).astype(BF16)
    for s in range(ns):
        oa = _dot(p[s * rows_per_seq:(s + 1) * rows_per_seq, :], vbuf[slot, s].astype(BF16))
        for h in range(N_XHEADS):
            sl = slice(h * XHEAD_DIM, (h + 1) * XHEAD_DIM)
            o_ref[row0 + s * t:row0 + (s + 1) * t, sl] = oa[h * t:(h + 1) * t, sl]


def _tail_prompt_kernel(x_ref, kt_ref, v_ref, qs_ref, ck_hbm, cv_hbm, gq_ref, wq_ref, wo_ref, g2_ref,
                        w2in_ref, w2out_ref, gf_ref, y_ref, os_ref, act_ref, o_ref, kbuf, vbuf, sem):
    step = pl.program_id(0) * pl.num_programs(1) + pl.program_id(1)
    t = qs_ref.shape[0] // kbuf.shape[1]

    copies = _kv_copies(ck_hbm, cv_hbm, kbuf, vbuf, sem, step, 0)
    for c in copies:
        c.start()

    def attend_samples():
        for c in copies:
            c.wait()
        _attend_cached(qs_ref, os_ref, 0, t, kbuf, vbuf, 0)

    x2 = x_ref[...]
    q = _queries(x2, gq_ref, wq_ref).astype(BF16)
    for h in range(N_XHEADS):
        sl = slice(h * XHEAD_DIM, (h + 1) * XHEAD_DIM)
        p = _softmax(_dot(q[:, sl], kt_ref[sl, :])).astype(BF16)
        o_ref[:, sl] = _dot(p, v_ref[:, sl]).astype(BF16)
    x3 = x2 + _dot(o_ref[...], wo_ref[...])
    x4 = _ffn_half(x3, g2_ref, w2in_ref, w2out_ref, act_ref, between_phases=attend_samples)
    y_ref[...] = _rms(x4, gf_ref[...])


def _tail_sample_kernel(x_ref, o_ref, wo_ref, g2_ref, w2in_ref, w2out_ref, gf_ref, y_ref, act_ref):
    x3 = x_ref[...] + _dot(o_ref[...].astype(BF16), wo_ref[...])
    x4 = _ffn_half(x3, g2_ref, w2in_ref, w2out_ref, act_ref)
    y_ref[...] = _rms(x4, gf_ref[...])


def _resident(arr):
    nd = arr.ndim
    return pl.BlockSpec(arr.shape, lambda *_: (0,) * nd, pipeline_mode=pl.Buffered(1))


def _params(n_grid):
    return pltpu.CompilerParams(dimension_semantics=("arbitrary",) * n_grid,
                                vmem_limit_bytes=VMEM_LIMIT_BYTES)


def kernel(x_prompt, x_sample, mem_prompt, state_pool, state_conv, cache_mem_k, cache_mem_v, g_ffn1, w_ffn1_in, w_ffn1_out, g_mix, w_mix_in, pool_w, pool_scale, conv_w, w_mix_out, g_xq, g_mem, w_xq, w_xk, w_xv, w_xo, g_ffn2, w_ffn2_in, w_ffn2_out, g_final):
    depth = g_ffn1.shape[0]
    assert depth == 1
    bp, seq, d = x_prompt.shape
    bs, tdec, _ = x_sample.shape
    l = 0

    w1in = w_ffn1_in[l].astype(BF16)
    w1out = w_ffn1_out[l].astype(BF16)
    w2in = w_ffn2_in[l].astype(BF16)
    w2out = w_ffn2_out[l].astype(BF16)
    wmixin = w_mix_in[l].astype(BF16)
    wmixout = w_mix_out[l].astype(BF16)
    poolw = pool_w[l].astype(BF16)
    wq = w_xq[l].astype(BF16)
    wk = w_xk[l].astype(BF16)
    wv = w_xv[l].astype(BF16)
    wo = w_xo[l].astype(BF16)
    g1, gmix, gq, gmem, g2 = g_ffn1[l:l + 1], g_mix[l:l + 1], g_xq[l:l + 1], g_mem[l:l + 1], g_ffn2[l:l + 1]
    gf = g_final.reshape(1, d)
    pscale = pool_scale[l:l + 1]
    convw = conv_w[l]

    mk, mv, kt, vb = pl.pallas_call(
        _memkv_kernel,
        grid=(bp,),
        in_specs=[pl.BlockSpec((None, N_MEM, d), lambda b: (b, 0, 0)),
                  _resident(gmem), _resident(wk), _resident(wv)],
        out_specs=[pl.BlockSpec(memory_space=pl.ANY),
                   pl.BlockSpec(memory_space=pl.ANY),
                   pl.BlockSpec((None, d, N_MEM), lambda b: (b, 0, 0)),
                   pl.BlockSpec((None, N_MEM, d), lambda b: (b, 0, 0))],
        out_shape=[jax.ShapeDtypeStruct((1, bp, N_MEM, N_XHEADS, XHEAD_DIM), F32),
                   jax.ShapeDtypeStruct((1, bp, N_MEM, N_XHEADS, XHEAD_DIM), F32),
                   jax.ShapeDtypeStruct((bp, d, N_MEM), BF16),
                   jax.ShapeDtypeStruct((bp, N_MEM, d), BF16)],
        scratch_shapes=[pltpu.VMEM((2, N_MEM, d), F32),
                        pltpu.VMEM((2, N_MEM, d), F32),
                        pltpu.SemaphoreType.DMA((2, 2, N_XHEADS))],
        compiler_params=_params(1),
        name="mem_kv",
    )(mem_prompt, gmem, wk, wv)

    tm = TM_PROMPT
    mix_weights = (g1, w1in, w1out, gmix, wmixin, poolw, pscale, convw, wmixout)
    x2p, npool_p, nconv_p = pl.pallas_call(
        _mix_prompt_kernel,
        grid=(bp, seq // tm),
        in_specs=[pl.BlockSpec((None, tm, d), lambda b, j: (b, j, 0))]
                 + [_resident(w) for w in mix_weights],
        out_specs=[pl.BlockSpec((None, tm, d), lambda b, j: (b, j, 0)),
                   pl.BlockSpec((POOL_HIST, bp, D_POOL), lambda b, j: (0, 0, 0)),
                   pl.BlockSpec((None, CONV_HIST, D_CONV), lambda b, j: (b, 0, 0))],
        out_shape=[jax.ShapeDtypeStruct((bp, seq, d), F32),
                   jax.ShapeDtypeStruct((POOL_HIST, bp, D_POOL), F32),
                   jax.ShapeDtypeStruct((bp, CONV_HIST, D_CONV), F32)],
        scratch_shapes=[pltpu.VMEM((tm, D_FF), BF16),
                        pltpu.VMEM((tm + POOL_PAD, D_POOL), F32),
                        pltpu.VMEM((tm + CONV_PAD, D_CONV), F32),
                        pltpu.VMEM((tm, d), BF16)],
        compiler_params=_params(2),
        name="mix_prompt",
    )(x_prompt, *mix_weights)

    tms = TM_SAMPLE
    ns = tms // tdec
    xs = x_sample.reshape(bs * tdec, d)
    hist_pool = jnp.transpose(state_pool[l], (1, 0, 2))
    mix_sample_weights = mix_weights + (gq, wq)
    x2s, qs, npool_s, nconv_s = pl.pallas_call(
        _mix_sample_kernel,
        grid=(bs * tdec // tms,),
        in_specs=[pl.BlockSpec((tms, d), lambda i: (i, 0)),
                  pl.BlockSpec((POOL_HIST, ns, D_POOL), lambda i: (0, i, 0)),
                  pl.BlockSpec((None, ns, CONV_HIST, D_CONV), lambda i: (0, i, 0, 0))]
                 + [_resident(w) for w in mix_sample_weights],
        out_specs=[pl.BlockSpec((tms, d), lambda i: (i, 0)),
                   pl.BlockSpec((tms, d), lambda i: (i, 0)),
                   pl.BlockSpec((POOL_HIST, ns, D_POOL), lambda i: (0, i, 0)),
                   pl.BlockSpec((ns, CONV_HIST, D_CONV), lambda i: (i, 0, 0))],
        out_shape=[jax.ShapeDtypeStruct((bs * tdec, d), F32),
                   jax.ShapeDtypeStruct((bs * tdec, d), F32),
                   jax.ShapeDtypeStruct((POOL_HIST, bs, D_POOL), F32),
                   jax.ShapeDtypeStruct((bs, CONV_HIST, D_CONV), F32)],
        scratch_shapes=[pltpu.VMEM((tms, D_FF), BF16),
                        pltpu.VMEM((len(POOL_WINDOWS), tms, POOL_GROUP), F32),
                        pltpu.VMEM((len(POOL_WINDOWS), tms, POOL_GROUP), F32),
                        pltpu.VMEM((ns, CONV_PAD + tdec, D_CONV), F32),
                        pltpu.VMEM((tms, d), BF16)],
        compiler_params=_params(1),
        name="mix_sample",
    )(xs, hist_pool, state_conv, *mix_sample_weights)

    nj = seq // tm
    n_steps = bp * nj
    seqs_per_step = bs // n_steps
    assert seqs_per_step * n_steps == bs
    qrows = seqs_per_step * tdec
    tail_weights = (gq, wq, wo, g2, w2in, w2out, gf)
    y_prompt, os_ = pl.pallas_call(
        _tail_prompt_kernel,
        grid=(bp, nj),
        in_specs=[pl.BlockSpec((None, tm, d), lambda b, j: (b, j, 0)),
                  pl.BlockSpec((None, d, N_MEM), lambda b, j: (b, 0, 0)),
                  pl.BlockSpec((None, N_MEM, d), lambda b, j: (b, 0, 0)),
                  pl.BlockSpec((qrows, d), lambda b, j: (b * nj + j, 0)),
                  pl.BlockSpec(memory_space=pl.ANY),
                  pl.BlockSpec(memory_space=pl.ANY)]
                 + [_resident(w) for w in tail_weights],
        out_specs=[pl.BlockSpec((None, tm, d), lambda b, j: (b, j, 0)),
                   pl.BlockSpec((qrows, d), lambda b, j: (b * nj + j, 0))],
        out_shape=[jax.ShapeDtypeStruct((bp, seq, d), F32),
                   jax.ShapeDtypeStruct((bs * tdec, d), F32)],
        scratch_shapes=[pltpu.VMEM((tm, D_FF), BF16),
                        pltpu.VMEM((tm, d), BF16),
                        pltpu.VMEM((1, seqs_per_step, N_MEM, d), F32),
                        pltpu.VMEM((1, seqs_per_step, N_MEM, d), F32),
                        pltpu.SemaphoreType.DMA((1, 2, N_XHEADS))],
        compiler_params=_params(2),
        name="tail_prompt",
    )(x2p, kt, vb, qs, cache_mem_k, cache_mem_v, *tail_weights)

    tail_sample_weights = (wo, g2, w2in, w2out, gf)
    y_sample = pl.pallas_call(
        _tail_sample_kernel,
        grid=(bs * tdec // tms,),
        in_specs=[pl.BlockSpec((tms, d), lambda i: (i, 0)),
                  pl.BlockSpec((tms, d), lambda i: (i, 0))]
                 + [_resident(w) for w in tail_sample_weights],
        out_specs=pl.BlockSpec((tms, d), lambda i: (i, 0)),
        out_shape=jax.ShapeDtypeStruct((bs * tdec, d), F32),
        scratch_shapes=[pltpu.VMEM((tms, D_FF), BF16)],
        compiler_params=_params(1),
        name="tail_sample",
    )(x2s, os_, *tail_sample_weights)

    return (y_prompt,
            y_sample.reshape(bs, tdec, d),
            jnp.transpose(npool_p, (1, 0, 2))[None],
            nconv_p[None],
            mk,
            mv,
            jnp.transpose(npool_s, (1, 0, 2))[None],
            nconv_s[None])
```

```python
import functools

import jax
import jax.numpy as jnp
from jax import lax
from jax.experimental import pallas as pl
from jax.experimental.pallas import tpu as pltpu

D_MODEL = 1024
D_POOL = 512
D_CONV = 512
POOL_WINDOWS = (2, 4, 8, 16)
POOL_GROUP = 128
POOL_HIST = 15
CONV_HIST = 2
D_FF = 2816
N_MEM = 256
N_XHEADS = 4
XHEAD_DIM = 256
EPS = 1e-6
PAST_LEN = 16384

LANES = 128
SUBLANES = 8
FF_CHUNK = 512
FF_CHUNKS = tuple((s, min(FF_CHUNK, D_FF - s)) for s in range(0, D_FF, FF_CHUNK))
POOL_PAD = 16
CONV_PAD = 8
VMEM_LIMIT_BYTES = 60 * 1024 * 1024

CAST_SLOTS, CAST_ROWS, CAST_COLS = 8, 256, 1024

TM_PROMPT = 512
TM_SAMPLE = 256

F32 = jnp.float32
BF16 = jnp.bfloat16


def _rms(x, g):
    y = x * lax.rsqrt(jnp.mean(x * x, axis=-1, keepdims=True) + EPS)
    return y * g


def _dot(a, b):
    return jnp.dot(a, b, preferred_element_type=F32)


def _ffn_half(x, g_ref, win_ref, wout_ref, act_ref, between_phases=None):
    m = x.shape[0]
    h = _rms(x, g_ref[...]).astype(BF16)
    for s0, n in FF_CHUNKS:
        gate = _dot(h, win_ref[:, s0:s0 + n])
        up = _dot(h, win_ref[:, D_FF + s0:D_FF + s0 + n])
        silu = gate * (1.0 / (1.0 + jnp.exp(-gate)))
        act_ref[0:m, s0:s0 + n] = (silu * up).astype(BF16)
    if between_phases is not None:
        between_phases()
    y = _dot(act_ref[0:m, :], wout_ref[...])
    return x + 0.5 * y


def _cast_weights_to_vmem(weights, stage_ref, sem):
    n_slots, cast_rows, cast_cols = stage_ref.shape
    blocks = []
    for w_hbm, dst in weights:
        rows, cols = dst.shape
        for r0 in range(0, rows, cast_rows):
            for c0 in range(0, cols, cast_cols):
                blocks.append((w_hbm, dst, r0, min(cast_rows, rows - r0), c0, min(cast_cols, cols - c0)))

    def copy(i):
        w_hbm, _, r0, nr, c0, nc = blocks[i]
        slot = i % n_slots
        return pltpu.make_async_copy(w_hbm.at[0, pl.ds(r0, nr), pl.ds(c0, nc)],
                                     stage_ref.at[slot, pl.ds(0, nr), pl.ds(0, nc)], sem.at[slot])

    for i in range(min(n_slots, len(blocks))):
        copy(i).start()
    for i, (_, dst, r0, nr, c0, nc) in enumerate(blocks):
        copy(i).wait()
        dst[r0:r0 + nr, c0:c0 + nc] = stage_ref[i % n_slots, 0:nr, 0:nc].astype(BF16)
        if i + n_slots < len(blocks):
            copy(i + n_slots).start()


def _export_copies(pairs, sem):
    return [pltpu.make_async_copy(src, dst, sem.at[k]) for k, (src, dst) in enumerate(pairs)]


def _window_sum(x, w):
    s = x
    sh = 1
    while sh < w:
        s = s + pltpu.roll(s, sh, 0)
        sh *= 2
    return s


def _pool_group(ext, a, pos, gi, poolw_ref, pscale_ref, extract):
    w = POOL_WINDOWS[gi]
    ssum = extract(_window_sum(ext, w))
    cnt = jnp.minimum(pos + 1, w).astype(F32)
    d = (ssum / cnt - a).astype(BF16)
    y = _dot(d, poolw_ref[gi])
    return y * pscale_ref[:, gi * POOL_GROUP:(gi + 1) * POOL_GROUP]


def _conv3(ext, convw_ref, extract):
    u0 = extract(pltpu.roll(ext, 2, 0))
    u1 = extract(pltpu.roll(ext, 1, 0))
    u2 = extract(ext)
    return convw_ref[0:1, :] * u0 + convw_ref[1:2, :] * u1 + convw_ref[2:3, :] * u2


def _memkv_copies(kscr, vscr, mk_hbm, mv_hbm, sem, b, slot):
    copies = []
    for h in range(N_XHEADS):
        cols = pl.ds(h * XHEAD_DIM, XHEAD_DIM)
        copies.append(pltpu.make_async_copy(kscr.at[slot, :, cols], mk_hbm.at[0, b, :, h, :], sem.at[slot, 0, h]))
        copies.append(pltpu.make_async_copy(vscr.at[slot, :, cols], mv_hbm.at[0, b, :, h, :], sem.at[slot, 1, h]))
    return copies


def _memkv_kernel(mem_ref, g_ref, wk_ref, wv_ref, mk_hbm, mv_hbm, kt_ref, vb_ref, kscr, vscr, sem):
    b = pl.program_id(0)
    nb = pl.num_programs(0)
    slot = lax.rem(b, 2)

    @pl.when(b >= 2)
    def _():
        for c in _memkv_copies(kscr, vscr, mk_hbm, mv_hbm, sem, b - 2, slot):
            c.wait()

    mn = _rms(mem_ref[...], g_ref[...]).astype(BF16)
    k = _dot(mn, wk_ref[...])
    v = _dot(mn, wv_ref[...])
    kscr[slot] = k
    vscr[slot] = v
    kt_ref[...] = k.T.astype(BF16)
    vb_ref[...] = v.astype(BF16)
    for c in _memkv_copies(kscr, vscr, mk_hbm, mv_hbm, sem, b, slot):
        c.start()

    @pl.when(b == nb - 1)
    def _():
        for c in _memkv_copies(kscr, vscr, mk_hbm, mv_hbm, sem, b - 1, 1 - slot):
            c.wait()
        for c in _memkv_copies(kscr, vscr, mk_hbm, mv_hbm, sem, b, slot):
            c.wait()


def _mix_prompt_kernel(x_ref, g1_ref, w1in_hbm, w1out_hbm, gmix_ref, wmixin_hbm, poolw_ref,
                       pscale_ref, convw_ref, wmixout_hbm, wq_hbm,
                       x2_ref, npool_ref, nconv_ref, w1in_out, w1out_out, wmixin_out, wmixout_out, wq_out,
                       act_ref, epool_ref, econv_ref, mixed_ref,
                       w1in_ref, w1out_ref, wmixin_ref, wmixout_ref, wq_ref, stage_ref, stage_sem, export_sem):
    tm = x_ref.shape[0]
    b = pl.program_id(0)
    j = pl.program_id(1)
    first = jnp.logical_and(b == 0, j == 0)
    last = jnp.logical_and(b == pl.num_programs(0) - 1, j == pl.num_programs(1) - 1)
    exports = _export_copies([(w1in_ref, w1in_out), (w1out_ref, w1out_out), (wmixin_ref, wmixin_out),
                              (wmixout_ref, wmixout_out), (wq_ref, wq_out)], export_sem)

    @pl.when(first)
    def _():
        _cast_weights_to_vmem([(w1in_hbm, w1in_ref), (w1out_hbm, w1out_ref), (wmixin_hbm, wmixin_ref),
                               (wmixout_hbm, wmixout_ref), (wq_hbm, wq_ref)], stage_ref, stage_sem)
        for c in exports:
            c.start()

    @pl.when(j == 0)
    def _():
        epool_ref[0:POOL_PAD, :] = jnp.zeros((POOL_PAD, D_POOL), F32)
        econv_ref[0:CONV_PAD, :] = jnp.zeros((CONV_PAD, D_CONV), F32)

    x1 = _ffn_half(x_ref[...], g1_ref, w1in_ref, w1out_ref, act_ref)
    z = _dot(_rms(x1, gmix_ref[...]).astype(BF16), wmixin_ref[...])
    a = z[:, 0:D_POOL]
    cb = z[:, D_POOL:D_POOL + D_CONV]
    u = z[:, D_POOL + D_CONV:D_POOL + 2 * D_CONV] * z[:, D_POOL + 2 * D_CONV:]
    epool_ref[POOL_PAD:, :] = a
    econv_ref[CONV_PAD:, :] = u

    pos = j * tm + lax.broadcasted_iota(jnp.int32, (tm, 1), 0)
    for gi in range(len(POOL_WINDOWS)):
        sl = slice(gi * POOL_GROUP, (gi + 1) * POOL_GROUP)
        p = _pool_group(epool_ref[:, sl], a[:, sl], pos, gi, poolw_ref, pscale_ref,
                        lambda s: s[POOL_PAD:, :])
        mixed_ref[:, sl] = p.astype(BF16)
    yc = _conv3(econv_ref[...], convw_ref, lambda s: s[CONV_PAD:, :])
    mixed_ref[:, D_POOL:] = (cb * yc).astype(BF16)
    x2_ref[...] = x1 + _dot(mixed_ref[...], wmixout_ref[...])

    epool_ref[0:POOL_PAD, :] = epool_ref[tm:tm + POOL_PAD, :]
    econv_ref[0:CONV_PAD, :] = econv_ref[tm:tm + CONV_PAD, :]

    @pl.when(j == pl.num_programs(1) - 1)
    def _():
        for r in range(POOL_HIST):
            row = POOL_PAD - POOL_HIST + r
            npool_ref[r, pl.ds(b, 1), :] = epool_ref[row:row + 1, :]
        nconv_ref[...] = econv_ref[CONV_PAD - CONV_HIST:CONV_PAD, :]

    @pl.when(last)
    def _():
        for c in exports:
            c.wait()


def _mix_sample_kernel(x_ref, hpool_ref, hconv_ref, g1_ref, w1in_ref, w1out_ref, gmix_ref,
                       wmixin_ref, poolw_ref, pscale_ref, convw_ref, wmixout_ref, gq_ref, wq_ref,
                       x2_ref, q_ref, npool_ref, nconv_ref,
                       act_ref, a_ref, d_ref, econv_ref, mixed_ref):
    tm = x_ref.shape[0]
    ns = hpool_ref.shape[1]
    t = tm // ns
    crow = CONV_PAD + t

    x1 = _ffn_half(x_ref[...], g1_ref, w1in_ref, w1out_ref, act_ref)
    z = _dot(_rms(x1, gmix_ref[...]).astype(BF16), wmixin_ref[...])
    cb = z[:, D_POOL:D_POOL + D_CONV]
    u = z[:, D_POOL + D_CONV:D_POOL + 2 * D_CONV] * z[:, D_POOL + 2 * D_CONV:]

    n_ext = POOL_HIST + t
    for gi, w in enumerate(POOL_WINDOWS):
        sl = slice(gi * POOL_GROUP, (gi + 1) * POOL_GROUP)
        a_ref[gi] = z[:, sl]
        new = [a_ref[gi, pl.ds(tt, ns, stride=t), :] for tt in range(t)]
        sums = {i: hpool_ref[i, :, sl] for i in range(POOL_HIST)}
        sums.update({POOL_HIST + tt: new[tt] for tt in range(t)})
        for r in range(POOL_HIST):
            npool_ref[r, :, sl] = sums[n_ext - POOL_HIST + r]
        sh = 1
        while sh < w:
            sums = {i: sums[i] + sums[i - sh] for i in sums if (i - sh) in sums}
            sh *= 2
        for tt in range(t):
            cnt = float(min(PAST_LEN + tt + 1, w))
            d_ref[gi, pl.ds(tt, ns, stride=t), :] = sums[POOL_HIST + tt] / cnt - new[tt]
        y = _dot(d_ref[gi].astype(BF16), poolw_ref[gi])
        mixed_ref[:, sl] = (y * pscale_ref[:, sl]).astype(BF16)

    econv_ref[:, 0:CONV_PAD - CONV_HIST, :] = jnp.zeros((ns, CONV_PAD - CONV_HIST, D_CONV), F32)
    econv_ref[:, CONV_PAD - CONV_HIST:CONV_PAD, :] = hconv_ref[...]
    econv_ref[:, CONV_PAD:, :] = u.reshape(ns, t, D_CONV)
    yc = _conv3(econv_ref[...].reshape(ns * crow, D_CONV), convw_ref,
                lambda s: s.reshape(ns, crow, D_CONV)[:, CONV_PAD:, :].reshape(tm, D_CONV))
    mixed_ref[:, D_POOL:] = (cb * yc).astype(BF16)
    x2 = x1 + _dot(mixed_ref[...], wmixout_ref[...])
    x2_ref[...] = x2
    q_ref[...] = _queries(x2, gq_ref, wq_ref)
    nconv_ref[...] = econv_ref[:, crow - CONV_HIST:crow, :]


def _softmax(s):
    e = jnp.exp(s - jnp.max(s, axis=-1, keepdims=True))
    return e / jnp.sum(e, axis=-1, keepdims=True)


def _queries(x, gq_ref, wq_ref):
    return _dot(_rms(x, gq_ref[...]).astype(BF16), wq_ref[...]) * (XHEAD_DIM ** -0.5)


def _kv_copies(k_hbm, v_hbm, kv_ref, sem, step):
    ns = kv_ref.shape[0] // 2
    seqs = pl.ds(step * ns, ns)
    copies = []
    for h in range(N_XHEADS):
        cols = pl.ds(h * XHEAD_DIM, XHEAD_DIM)
        copies.append(pltpu.make_async_copy(k_hbm.at[0, seqs, :, h, :], kv_ref.at[pl.ds(0, ns), :, cols],
                                            sem.at[0, h]))
        copies.append(pltpu.make_async_copy(v_hbm.at[0, seqs, :, h, :], kv_ref.at[pl.ds(ns, ns), :, cols],
                                            sem.at[1, h]))
    return copies


def _attend_cached(q_ref, o_ref, kv_ref):
    ns = kv_ref.shape[0] // 2
    t = q_ref.shape[0] // ns
    d = q_ref.shape[1]
    rows_per_seq = N_XHEADS * t
    row_head = lax.broadcasted_iota(jnp.int32, (rows_per_seq, d), 0) // t
    col_head = lax.broadcasted_iota(jnp.int32, (rows_per_seq, d), 1) // XHEAD_DIM
    own_head = row_head == col_head
    scores = []
    for s in range(ns):
        qs = q_ref[s * t:(s + 1) * t, :]
        qe = jnp.where(own_head, jnp.concatenate([qs] * N_XHEADS, axis=0), 0.0).astype(BF16)
        scores.append(lax.dot_general(qe, kv_ref[s].astype(BF16), (((1,), (1,)), ((), ())),
                                      preferred_element_type=F32))
    p = _softmax(jnp.concatenate(scores, axis=0)).astype(BF16)
    for s in range(ns):
        oa = _dot(p[s * rows_per_seq:(s + 1) * rows_per_seq, :], kv_ref[ns + s].astype(BF16))
        for h in range(N_XHEADS):
            sl = slice(h * XHEAD_DIM, (h + 1) * XHEAD_DIM)
            o_ref[s * t:(s + 1) * t, sl] = oa[h * t:(h + 1) * t, sl]


def _tail_prompt_kernel(x_ref, kt_ref, v_ref, qs_ref, ck_hbm, cv_hbm, gq_ref, wq_ref, wo_hbm, g2_ref,
                        w2in_hbm, w2out_hbm, gf_ref,
                        y_ref, os_ref, wo_out, w2in_out, w2out_out,
                        act_ref, o_ref, wo_ref, w2in_ref, w2out_ref, kv_ref, kv_sem, stage_sem, export_sem):
    step = pl.program_id(0) * pl.num_programs(1) + pl.program_id(1)
    n_steps = pl.num_programs(0) * pl.num_programs(1)
    exports = _export_copies([(wo_ref, wo_out), (w2in_ref, w2in_out), (w2out_ref, w2out_out)], export_sem)

    @pl.when(step == 0)
    def _():
        _cast_weights_to_vmem([(wo_hbm, wo_ref), (w2in_hbm, w2in_ref), (w2out_hbm, w2out_ref)],
                              kv_ref, stage_sem)
        for c in exports:
            c.start()

    copies = _kv_copies(ck_hbm, cv_hbm, kv_ref, kv_sem, step)
    for c in copies:
        c.start()

    def attend_samples():
        for c in copies:
            c.wait()
        _attend_cached(qs_ref, os_ref, kv_ref)

    x2 = x_ref[...]
    q = _queries(x2, gq_ref, wq_ref).astype(BF16)
    for h in range(N_XHEADS):
        sl = slice(h * XHEAD_DIM, (h + 1) * XHEAD_DIM)
        p = _softmax(_dot(q[:, sl], kt_ref[sl, :])).astype(BF16)
        o_ref[:, sl] = _dot(p, v_ref[:, sl]).astype(BF16)
    x3 = x2 + _dot(o_ref[...], wo_ref[...])
    x4 = _ffn_half(x3, g2_ref, w2in_ref, w2out_ref, act_ref, between_phases=attend_samples)
    y_ref[...] = _rms(x4, gf_ref[...])

    @pl.when(step == n_steps - 1)
    def _():
        for c in exports:
            c.wait()


def _tail_sample_kernel(x_ref, o_ref, wo_ref, g2_ref, w2in_ref, w2out_ref, gf_ref, y_ref, act_ref):
    x3 = x_ref[...] + _dot(o_ref[...].astype(BF16), wo_ref[...])
    x4 = _ffn_half(x3, g2_ref, w2in_ref, w2out_ref, act_ref)
    y_ref[...] = _rms(x4, gf_ref[...])


def _resident(arr):
    nd = arr.ndim
    return pl.BlockSpec(arr.shape, lambda *_: (0,) * nd, pipeline_mode=pl.Buffered(1))


def _params(n_grid):
    return pltpu.CompilerParams(dimension_semantics=("arbitrary",) * n_grid,
                                vmem_limit_bytes=VMEM_LIMIT_BYTES)


def kernel(x_prompt, x_sample, mem_prompt, state_pool, state_conv, cache_mem_k, cache_mem_v, g_ffn1, w_ffn1_in, w_ffn1_out, g_mix, w_mix_in, pool_w, pool_scale, conv_w, w_mix_out, g_xq, g_mem, w_xq, w_xk, w_xv, w_xo, g_ffn2, w_ffn2_in, w_ffn2_out, g_final):
    depth = g_ffn1.shape[0]
    assert depth == 1
    bp, seq, d = x_prompt.shape
    bs, tdec, _ = x_sample.shape
    l = 0

    poolw = pool_w[l].astype(BF16)
    wk = w_xk[l].astype(BF16)
    wv = w_xv[l].astype(BF16)
    any_space = pl.BlockSpec(memory_space=pl.ANY)
    stage_shape = (CAST_SLOTS, CAST_ROWS, CAST_COLS)
    g1, gmix, gq, gmem, g2 = g_ffn1[l:l + 1], g_mix[l:l + 1], g_xq[l:l + 1], g_mem[l:l + 1], g_ffn2[l:l + 1]
    gf = g_final.reshape(1, d)
    pscale = pool_scale[l:l + 1]
    convw = conv_w[l]

    mk, mv, kt, vb = pl.pallas_call(
        _memkv_kernel,
        grid=(bp,),
        in_specs=[pl.BlockSpec((None, N_MEM, d), lambda b: (b, 0, 0)),
                  _resident(gmem), _resident(wk), _resident(wv)],
        out_specs=[pl.BlockSpec(memory_space=pl.ANY),
                   pl.BlockSpec(memory_space=pl.ANY),
                   pl.BlockSpec((None, d, N_MEM), lambda b: (b, 0, 0)),
                   pl.BlockSpec((None, N_MEM, d), lambda b: (b, 0, 0))],
        out_shape=[jax.ShapeDtypeStruct((1, bp, N_MEM, N_XHEADS, XHEAD_DIM), F32),
                   jax.ShapeDtypeStruct((1, bp, N_MEM, N_XHEADS, XHEAD_DIM), F32),
                   jax.ShapeDtypeStruct((bp, d, N_MEM), BF16),
                   jax.ShapeDtypeStruct((bp, N_MEM, d), BF16)],
        scratch_shapes=[pltpu.VMEM((2, N_MEM, d), F32),
                        pltpu.VMEM((2, N_MEM, d), F32),
                        pltpu.SemaphoreType.DMA((2, 2, N_XHEADS))],
        compiler_params=_params(1),
        name="mem_kv",
    )(mem_prompt, gmem, wk, wv)

    tm = TM_PROMPT
    mix_cast_shapes = [(d, 2 * D_FF), (D_FF, d), (d, D_POOL + 3 * D_CONV), (D_POOL + D_CONV, d), (d, d)]
    x2p, npool_p, nconv_p, w1in, w1out, wmixin, wmixout, wq = pl.pallas_call(
        _mix_prompt_kernel,
        grid=(bp, seq // tm),
        in_specs=[pl.BlockSpec((None, tm, d), lambda b, j: (b, j, 0)),
                  _resident(g1), any_space, any_space, _resident(gmix), any_space, _resident(poolw),
                  _resident(pscale), _resident(convw), any_space, any_space],
        out_specs=[pl.BlockSpec((None, tm, d), lambda b, j: (b, j, 0)),
                   pl.BlockSpec((POOL_HIST, bp, D_POOL), lambda b, j: (0, 0, 0)),
                   pl.BlockSpec((None, CONV_HIST, D_CONV), lambda b, j: (b, 0, 0))]
                  + [any_space] * len(mix_cast_shapes),
        out_shape=[jax.ShapeDtypeStruct((bp, seq, d), F32),
                   jax.ShapeDtypeStruct((POOL_HIST, bp, D_POOL), F32),
                   jax.ShapeDtypeStruct((bp, CONV_HIST, D_CONV), F32)]
                  + [jax.ShapeDtypeStruct(s, BF16) for s in mix_cast_shapes],
        scratch_shapes=[pltpu.VMEM((tm, D_FF), BF16),
                        pltpu.VMEM((tm + POOL_PAD, D_POOL), F32),
                        pltpu.VMEM((tm + CONV_PAD, D_CONV), F32),
                        pltpu.VMEM((tm, d), BF16)]
                       + [pltpu.VMEM(s, BF16) for s in mix_cast_shapes]
                       + [pltpu.VMEM(stage_shape, F32),
                          pltpu.SemaphoreType.DMA((CAST_SLOTS,)),
                          pltpu.SemaphoreType.DMA((len(mix_cast_shapes),))],
        compiler_params=_params(2),
        name="mix_prompt",
    )(x_prompt, g1, w_ffn1_in, w_ffn1_out, gmix, w_mix_in, poolw, pscale, convw, w_mix_out, w_xq)

    tms = TM_SAMPLE
    ns = tms // tdec
    xs = x_sample.reshape(bs * tdec, d)
    hist_pool = jnp.transpose(state_pool[l], (1, 0, 2))
    mix_sample_weights = (g1, w1in, w1out, gmix, wmixin, poolw, pscale, convw, wmixout, gq, wq)
    x2s, qs, npool_s, nconv_s = pl.pallas_call(
        _mix_sample_kernel,
        grid=(bs * tdec // tms,),
        in_specs=[pl.BlockSpec((tms, d), lambda i: (i, 0)),
                  pl.BlockSpec((POOL_HIST, ns, D_POOL), lambda i: (0, i, 0)),
                  pl.BlockSpec((None, ns, CONV_HIST, D_CONV), lambda i: (0, i, 0, 0))]
                 + [_resident(w) for w in mix_sample_weights],
        out_specs=[pl.BlockSpec((tms, d), lambda i: (i, 0)),
                   pl.BlockSpec((tms, d), lambda i: (i, 0)),
                   pl.BlockSpec((POOL_HIST, ns, D_POOL), lambda i: (0, i, 0)),
                   pl.BlockSpec((ns, CONV_HIST, D_CONV), lambda i: (i, 0, 0))],
        out_shape=[jax.ShapeDtypeStruct((bs * tdec, d), F32),
                   jax.ShapeDtypeStruct((bs * tdec, d), F32),
                   jax.ShapeDtypeStruct((POOL_HIST, bs, D_POOL), F32),
                   jax.ShapeDtypeStruct((bs, CONV_HIST, D_CONV), F32)],
        scratch_shapes=[pltpu.VMEM((tms, D_FF), BF16),
                        pltpu.VMEM((len(POOL_WINDOWS), tms, POOL_GROUP), F32),
                        pltpu.VMEM((len(POOL_WINDOWS), tms, POOL_GROUP), F32),
                        pltpu.VMEM((ns, CONV_PAD + tdec, D_CONV), F32),
                        pltpu.VMEM((tms, d), BF16)],
        compiler_params=_params(1),
        name="mix_sample",
    )(xs, hist_pool, state_conv, *mix_sample_weights)

    nj = seq // tm
    n_steps = bp * nj
    seqs_per_step = bs // n_steps
    assert seqs_per_step * n_steps == bs
    assert (2 * seqs_per_step, N_MEM, d) == stage_shape
    qrows = seqs_per_step * tdec
    tail_cast_shapes = [(d, d), (d, 2 * D_FF), (D_FF, d)]
    y_prompt, os_, wo, w2in, w2out = pl.pallas_call(
        _tail_prompt_kernel,
        grid=(bp, nj),
        in_specs=[pl.BlockSpec((None, tm, d), lambda b, j: (b, j, 0)),
                  pl.BlockSpec((None, d, N_MEM), lambda b, j: (b, 0, 0)),
                  pl.BlockSpec((None, N_MEM, d), lambda b, j: (b, 0, 0)),
                  pl.BlockSpec((qrows, d), lambda b, j: (b * nj + j, 0)),
                  any_space, any_space,
                  _resident(gq), _resident(wq), any_space, _resident(g2), any_space, any_space, _resident(gf)],
        out_specs=[pl.BlockSpec((None, tm, d), lambda b, j: (b, j, 0)),
                   pl.BlockSpec((qrows, d), lambda b, j: (b * nj + j, 0))]
                  + [any_space] * len(tail_cast_shapes),
        out_shape=[jax.ShapeDtypeStruct((bp, seq, d), F32),
                   jax.ShapeDtypeStruct((bs * tdec, d), F32)]
                  + [jax.ShapeDtypeStruct(s, BF16) for s in tail_cast_shapes],
        scratch_shapes=[pltpu.VMEM((tm, D_FF), BF16),
                        pltpu.VMEM((tm, d), BF16)]
                       + [pltpu.VMEM(s, BF16) for s in tail_cast_shapes]
                       + [pltpu.VMEM(stage_shape, F32),
                          pltpu.SemaphoreType.DMA((2, N_XHEADS)),
                          pltpu.SemaphoreType.DMA((CAST_SLOTS,)),
                          pltpu.SemaphoreType.DMA((len(tail_cast_shapes),))],
        compiler_params=_params(2),
        name="tail_prompt",
    )(x2p, kt, vb, qs, cache_mem_k, cache_mem_v, gq, wq, w_xo, g2, w_ffn2_in, w_ffn2_out, gf)

    tail_sample_weights = (wo, g2, w2in, w2out, gf)
    y_sample = pl.pallas_call(
        _tail_sample_kernel,
        grid=(bs * tdec // tms,),
        in_specs=[pl.BlockSpec((tms, d), lambda i: (i, 0)),
                  pl.BlockSpec((tms, d), lambda i: (i, 0))]
                 + [_resident(w) for w in tail_sample_weights],
        out_specs=pl.BlockSpec((tms, d), lambda i: (i, 0)),
        out_shape=jax.ShapeDtypeStruct((bs * tdec, d), F32),
        scratch_shapes=[pltpu.VMEM((tms, D_FF), BF16)],
        compiler_params=_params(1),
        name="tail_sample",
    )(x2s, os_, *tail_sample_weights)

    return (y_prompt,
            y_sample.reshape(bs, tdec, d),
            jnp.transpose(npool_p, (1, 0, 2))[None],
            nconv_p[None],
            mk,
            mv,
            jnp.transpose(npool_s, (1, 0, 2))[None],
            nconv_s[None])
```

```python
import functools

import jax
import jax.numpy as jnp
from jax import lax
from jax.experimental import pallas as pl
from jax.experimental.pallas import tpu as pltpu

D_MODEL = 1024
D_POOL = 512
D_CONV = 512
POOL_WINDOWS = (2, 4, 8, 16)
POOL_GROUP = 128
POOL_HIST = 15
CONV_HIST = 2
D_FF = 2816
N_MEM = 256
N_XHEADS = 4
XHEAD_DIM = 256
EPS = 1e-6
PAST_LEN = 16384

LANES = 128
SUBLANES = 8
FF_CHUNK = 512
FF_CHUNKS = tuple((s, min(FF_CHUNK, D_FF - s)) for s in range(0, D_FF, FF_CHUNK))
POOL_PAD = 16
CONV_PAD = 8
VMEM_LIMIT_BYTES = 60 * 1024 * 1024

CAST_SLOTS, CAST_ROWS, CAST_COLS = 8, 256, 1024

TM_PROMPT = 512
TM_SAMPLE = 256

F32 = jnp.float32
BF16 = jnp.bfloat16


def _rms(x, g):
    y = x * lax.rsqrt(jnp.mean(x * x, axis=-1, keepdims=True) + EPS)
    return y * g


def _dot(a, b):
    return jnp.dot(a, b, preferred_element_type=F32)


def _ffn_half(x, g_ref, win_ref, wout_ref, act_ref, between_phases=None):
    m = x.shape[0]
    h = _rms(x, g_ref[...]).astype(BF16)
    for s0, n in FF_CHUNKS:
        gate = _dot(h, win_ref[:, s0:s0 + n])
        up = _dot(h, win_ref[:, D_FF + s0:D_FF + s0 + n])
        silu = gate * (1.0 / (1.0 + jnp.exp(-gate)))
        act_ref[0:m, s0:s0 + n] = (silu * up).astype(BF16)
    if between_phases is not None:
        between_phases()
    y = _dot(act_ref[0:m, :], wout_ref[...])
    return x + 0.5 * y


def _cast_weights_to_vmem(weights, stage_ref, sem):
    n_slots, cast_rows, cast_cols = stage_ref.shape
    blocks = []
    for w_hbm, dst in weights:
        rows, cols = dst.shape
        for r0 in range(0, rows, cast_rows):
            for c0 in range(0, cols, cast_cols):
                blocks.append((w_hbm, dst, r0, min(cast_rows, rows - r0), c0, min(cast_cols, cols - c0)))

    def copy(i):
        w_hbm, _, r0, nr, c0, nc = blocks[i]
        slot = i % n_slots
        return pltpu.make_async_copy(w_hbm.at[0, pl.ds(r0, nr), pl.ds(c0, nc)],
                                     stage_ref.at[slot, pl.ds(0, nr), pl.ds(0, nc)], sem.at[slot])

    for i in range(min(n_slots, len(blocks))):
        copy(i).start()
    for i, (_, dst, r0, nr, c0, nc) in enumerate(blocks):
        copy(i).wait()
        dst[r0:r0 + nr, c0:c0 + nc] = stage_ref[i % n_slots, 0:nr, 0:nc].astype(BF16)
        if i + n_slots < len(blocks):
            copy(i + n_slots).start()


def _export_copies(pairs, sem):
    return [pltpu.make_async_copy(src, dst, sem.at[k]) for k, (src, dst) in enumerate(pairs)]


def _window_sum(x, w):
    s = x
    sh = 1
    while sh < w:
        s = s + pltpu.roll(s, sh, 0)
        sh *= 2
    return s


def _pool_group(ext, a, pos, gi, poolw_ref, pscale_ref, extract):
    w = POOL_WINDOWS[gi]
    ssum = extract(_window_sum(ext, w))
    cnt = jnp.minimum(pos + 1, w).astype(F32)
    d = (ssum / cnt - a).astype(BF16)
    y = _dot(d, poolw_ref[gi])
    return y * pscale_ref[:, gi * POOL_GROUP:(gi + 1) * POOL_GROUP]


def _conv3(ext, convw_ref, extract):
    u0 = extract(pltpu.roll(ext, 2, 0))
    u1 = extract(pltpu.roll(ext, 1, 0))
    u2 = extract(ext)
    return convw_ref[0:1, :] * u0 + convw_ref[1:2, :] * u1 + convw_ref[2:3, :] * u2


def _memkv_copies(kscr, vscr, mk_hbm, mv_hbm, sem, b, slot):
    copies = []
    for h in range(N_XHEADS):
        cols = pl.ds(h * XHEAD_DIM, XHEAD_DIM)
        copies.append(pltpu.make_async_copy(kscr.at[slot, :, cols], mk_hbm.at[0, b, :, h, :], sem.at[slot, 0, h]))
        copies.append(pltpu.make_async_copy(vscr.at[slot, :, cols], mv_hbm.at[0, b, :, h, :], sem.at[slot, 1, h]))
    return copies


def _memkv_kernel(mem_ref, g_ref, wk_ref, wv_ref, mk_hbm, mv_hbm, kt_ref, vb_ref, kscr, vscr, sem):
    b = pl.program_id(0)
    nb = pl.num_programs(0)
    slot = lax.rem(b, 2)

    @pl.when(b >= 2)
    def _():
        for c in _memkv_copies(kscr, vscr, mk_hbm, mv_hbm, sem, b - 2, slot):
            c.wait()

    mn = _rms(mem_ref[...], g_ref[...]).astype(BF16)
    k = _dot(mn, wk_ref[...])
    v = _dot(mn, wv_ref[...])
    kscr[slot] = k
    vscr[slot] = v
    kt_ref[...] = k.T.astype(BF16)
    vb_ref[...] = v.astype(BF16)
    for c in _memkv_copies(kscr, vscr, mk_hbm, mv_hbm, sem, b, slot):
        c.start()

    @pl.when(b == nb - 1)
    def _():
        for c in _memkv_copies(kscr, vscr, mk_hbm, mv_hbm, sem, b - 1, 1 - slot):
            c.wait()
        for c in _memkv_copies(kscr, vscr, mk_hbm, mv_hbm, sem, b, slot):
            c.wait()


def _mix_prompt_kernel(x_ref, g1_ref, w1in_hbm, w1out_hbm, gmix_ref, wmixin_hbm, poolw_ref,
                       pscale_ref, convw_ref, wmixout_hbm, wq_hbm,
                       x2_ref, npool_ref, nconv_ref, w1in_out, w1out_out, wmixin_out, wmixout_out, wq_out,
                       act_ref, epool_ref, econv_ref, mixed_ref,
                       w1in_ref, w1out_ref, wmixin_ref, wmixout_ref, wq_ref, stage_ref, stage_sem, export_sem):
    tm = x_ref.shape[0]
    b = pl.program_id(0)
    j = pl.program_id(1)
    first = jnp.logical_and(b == 0, j == 0)
    last = jnp.logical_and(b == pl.num_programs(0) - 1, j == pl.num_programs(1) - 1)
    exports = _export_copies([(w1in_ref, w1in_out), (w1out_ref, w1out_out), (wmixin_ref, wmixin_out),
                              (wmixout_ref, wmixout_out), (wq_ref, wq_out)], export_sem)

    @pl.when(first)
    def _():
        _cast_weights_to_vmem([(w1in_hbm, w1in_ref), (w1out_hbm, w1out_ref), (wmixin_hbm, wmixin_ref),
                               (wmixout_hbm, wmixout_ref), (wq_hbm, wq_ref)], stage_ref, stage_sem)
        for c in exports:
            c.start()

    @pl.when(j == 0)
    def _():
        epool_ref[0:POOL_PAD, :] = jnp.zeros((POOL_PAD, D_POOL), F32)
        econv_ref[0:CONV_PAD, :] = jnp.zeros((CONV_PAD, D_CONV), F32)

    x1 = _ffn_half(x_ref[...], g1_ref, w1in_ref, w1out_ref, act_ref)
    z = _dot(_rms(x1, gmix_ref[...]).astype(BF16), wmixin_ref[...])
    a = z[:, 0:D_POOL]
    cb = z[:, D_POOL:D_POOL + D_CONV]
    u = z[:, D_POOL + D_CONV:D_POOL + 2 * D_CONV] * z[:, D_POOL + 2 * D_CONV:]
    epool_ref[POOL_PAD:, :] = a
    econv_ref[CONV_PAD:, :] = u

    pos = j * tm + lax.broadcasted_iota(jnp.int32, (tm, 1), 0)
    for gi in range(len(POOL_WINDOWS)):
        sl = slice(gi * POOL_GROUP, (gi + 1) * POOL_GROUP)
        p = _pool_group(epool_ref[:, sl], a[:, sl], pos, gi, poolw_ref, pscale_ref,
                        lambda s: s[POOL_PAD:, :])
        mixed_ref[:, sl] = p.astype(BF16)
    yc = _conv3(econv_ref[...], convw_ref, lambda s: s[CONV_PAD:, :])
    mixed_ref[:, D_POOL:] = (cb * yc).astype(BF16)
    x2_ref[...] = x1 + _dot(mixed_ref[...], wmixout_ref[...])

    epool_ref[0:POOL_PAD, :] = epool_ref[tm:tm + POOL_PAD, :]
    econv_ref[0:CONV_PAD, :] = econv_ref[tm:tm + CONV_PAD, :]

    @pl.when(j == pl.num_programs(1) - 1)
    def _():
        for r in range(POOL_HIST):
            row = POOL_PAD - POOL_HIST + r
            npool_ref[r, pl.ds(b, 1), :] = epool_ref[row:row + 1, :]
        nconv_ref[...] = econv_ref[CONV_PAD - CONV_HIST:CONV_PAD, :]

    @pl.when(last)
    def _():
        for c in exports:
            c.wait()


def _mix_sample_kernel(x_ref, hpool_ref, hconv_ref, g1_ref, w1in_ref, w1out_ref, gmix_ref,
                       wmixin_ref, poolw_ref, pscale_ref, convw_ref, wmixout_ref, gq_ref, wq_ref,
                       x2_ref, q_ref, npool_ref, nconv_ref,
                       act_ref, a_ref, d_ref, econv_ref, mixed_ref):
    tm = x_ref.shape[0]
    ns = hpool_ref.shape[1]
    t = tm // ns
    crow = CONV_PAD + t

    x1 = _ffn_half(x_ref[...], g1_ref, w1in_ref, w1out_ref, act_ref)
    z = _dot(_rms(x1, gmix_ref[...]).astype(BF16), wmixin_ref[...])
    cb = z[:, D_POOL:D_POOL + D_CONV]
    u = z[:, D_POOL + D_CONV:D_POOL + 2 * D_CONV] * z[:, D_POOL + 2 * D_CONV:]

    n_ext = POOL_HIST + t
    for gi, w in enumerate(POOL_WINDOWS):
        sl = slice(gi * POOL_GROUP, (gi + 1) * POOL_GROUP)
        a_ref[gi] = z[:, sl]
        new = [a_ref[gi, pl.ds(tt, ns, stride=t), :] for tt in range(t)]
        sums = {i: hpool_ref[i, :, sl] for i in range(POOL_HIST)}
        sums.update({POOL_HIST + tt: new[tt] for tt in range(t)})
        for r in range(POOL_HIST):
            npool_ref[r, :, sl] = sums[n_ext - POOL_HIST + r]
        sh = 1
        while sh < w:
            sums = {i: sums[i] + sums[i - sh] for i in sums if (i - sh) in sums}
            sh *= 2
        for tt in range(t):
            cnt = float(min(PAST_LEN + tt + 1, w))
            d_ref[gi, pl.ds(tt, ns, stride=t), :] = sums[POOL_HIST + tt] / cnt - new[tt]
        y = _dot(d_ref[gi].astype(BF16), poolw_ref[gi])
        mixed_ref[:, sl] = (y * pscale_ref[:, sl]).astype(BF16)

    econv_ref[:, 0:CONV_PAD - CONV_HIST, :] = jnp.zeros((ns, CONV_PAD - CONV_HIST, D_CONV), F32)
    econv_ref[:, CONV_PAD - CONV_HIST:CONV_PAD, :] = hconv_ref[...]
    econv_ref[:, CONV_PAD:, :] = u.reshape(ns, t, D_CONV)
    yc = _conv3(econv_ref[...].reshape(ns * crow, D_CONV), convw_ref,
                lambda s: s.reshape(ns, crow, D_CONV)[:, CONV_PAD:, :].reshape(tm, D_CONV))
    mixed_ref[:, D_POOL:] = (cb * yc).astype(BF16)
    x2 = x1 + _dot(mixed_ref[...], wmixout_ref[...])
    x2_ref[...] = x2
    q_ref[...] = _queries(x2, gq_ref, wq_ref)
    nconv_ref[...] = econv_ref[:, crow - CONV_HIST:crow, :]


def _softmax(s):
    e = jnp.exp(s - jnp.max(s, axis=-1, keepdims=True))
    return e / jnp.sum(e, axis=-1, keepdims=True)


def _queries(x, gq_ref, wq_ref):
    return _dot(_rms(x, gq_ref[...]).astype(BF16), wq_ref[...]) * (XHEAD_DIM ** -0.5)


def _kv_copies(k_hbm, v_hbm, kv_ref, sem, step):
    ns = kv_ref.shape[0] // 2
    seqs = pl.ds(step * ns, ns)
    copies = []
    for h in range(N_XHEADS):
        cols = pl.ds(h * XHEAD_DIM, XHEAD_DIM)
        copies.append(pltpu.make_async_copy(k_hbm.at[0, seqs, :, h, :], kv_ref.at[pl.ds(0, ns), :, cols],
                                            sem.at[0, h]))
        copies.append(pltpu.make_async_copy(v_hbm.at[0, seqs, :, h, :], kv_ref.at[pl.ds(ns, ns), :, cols],
                                            sem.at[1, h]))
    return copies


def _cached_scores(q_ref, kv_ref):
    ns = kv_ref.shape[0] // 2
    t = q_ref.shape[0] // ns
    d = q_ref.shape[1]
    rows_per_seq = N_XHEADS * t
    row_head = lax.broadcasted_iota(jnp.int32, (rows_per_seq, d), 0) // t
    col_head = lax.broadcasted_iota(jnp.int32, (rows_per_seq, d), 1) // XHEAD_DIM
    own_head = row_head == col_head
    scores = []
    for s in range(ns):
        qs = q_ref[s * t:(s + 1) * t, :]
        qe = jnp.where(own_head, jnp.concatenate([qs] * N_XHEADS, axis=0), 0.0).astype(BF16)
        scores.append(lax.dot_general(qe, kv_ref[s].astype(BF16), (((1,), (1,)), ((), ())),
                                      preferred_element_type=F32))
    return jnp.concatenate(scores, axis=0)


def _cached_outputs(scores, kv_ref, o_ref):
    ns = kv_ref.shape[0] // 2
    t = o_ref.shape[0] // ns
    rows_per_seq = N_XHEADS * t
    p = _softmax(scores).astype(BF16)
    for s in range(ns):
        oa = _dot(p[s * rows_per_seq:(s + 1) * rows_per_seq, :], kv_ref[ns + s].astype(BF16))
        for h in range(N_XHEADS):
            sl = slice(h * XHEAD_DIM, (h + 1) * XHEAD_DIM)
            o_ref[s * t:(s + 1) * t, sl] = oa[h * t:(h + 1) * t, sl]


def _tail_prompt_kernel(x_ref, kt_ref, v_ref, qs_ref, ck_hbm, cv_hbm, gq_ref, wq_ref, wo_hbm, g2_ref,
                        w2in_hbm, w2out_hbm, gf_ref,
                        y_ref, os_ref, wo_out, w2in_out, w2out_out,
                        act_ref, o_ref, wo_ref, w2in_ref, w2out_ref, kv_ref, kv_sem, stage_sem, export_sem):
    step = pl.program_id(0) * pl.num_programs(1) + pl.program_id(1)
    n_steps = pl.num_programs(0) * pl.num_programs(1)
    exports = _export_copies([(wo_ref, wo_out), (w2in_ref, w2in_out), (w2out_ref, w2out_out)], export_sem)

    slot = lax.rem(step, 2)

    @pl.when(step == 0)
    def _():
        _cast_weights_to_vmem([(wo_hbm, wo_ref), (w2in_hbm, w2in_ref), (w2out_hbm, w2out_ref)],
                              kv_ref.at[0], stage_sem)
        for c in exports:
            c.start()
        for c in _kv_copies(ck_hbm, cv_hbm, kv_ref.at[0], kv_sem.at[0], 0):
            c.start()

    for c in _kv_copies(ck_hbm, cv_hbm, kv_ref.at[slot], kv_sem.at[slot], step):
        c.wait()

    @pl.when(step + 1 < n_steps)
    def _():
        for c in _kv_copies(ck_hbm, cv_hbm, kv_ref.at[1 - slot], kv_sem.at[1 - slot], step + 1):
            c.start()

    kv = kv_ref.at[slot]
    sample_scores = _cached_scores(qs_ref, kv)
    x2 = x_ref[...]
    q = _queries(x2, gq_ref, wq_ref).astype(BF16)
    heads = [slice(h * XHEAD_DIM, (h + 1) * XHEAD_DIM) for h in range(N_XHEADS)]
    scores = [_dot(q[:, heads[0]], kt_ref[heads[0], :])]
    for h in range(N_XHEADS):
        if h + 1 < N_XHEADS:
            scores.append(_dot(q[:, heads[h + 1]], kt_ref[heads[h + 1], :]))
        p = _softmax(scores[h]).astype(BF16)
        o_ref[:, heads[h]] = _dot(p, v_ref[:, heads[h]]).astype(BF16)
    x3 = x2 + _dot(o_ref[...], wo_ref[...])
    x4 = _ffn_half(x3, g2_ref, w2in_ref, w2out_ref, act_ref)
    _cached_outputs(sample_scores, kv, os_ref)
    y_ref[...] = _rms(x4, gf_ref[...])

    @pl.when(step == n_steps - 1)
    def _():
        for c in exports:
            c.wait()


def _tail_sample_kernel(x_ref, o_ref, wo_ref, g2_ref, w2in_ref, w2out_ref, gf_ref, y_ref, act_ref):
    x3 = x_ref[...] + _dot(o_ref[...].astype(BF16), wo_ref[...])
    x4 = _ffn_half(x3, g2_ref, w2in_ref, w2out_ref, act_ref)
    y_ref[...] = _rms(x4, gf_ref[...])


def _resident(arr):
    nd = arr.ndim
    return pl.BlockSpec(arr.shape, lambda *_: (0,) * nd, pipeline_mode=pl.Buffered(1))


def _params(n_grid):
    return pltpu.CompilerParams(dimension_semantics=("arbitrary",) * n_grid,
                                vmem_limit_bytes=VMEM_LIMIT_BYTES)


def kernel(x_prompt, x_sample, mem_prompt, state_pool, state_conv, cache_mem_k, cache_mem_v, g_ffn1, w_ffn1_in, w_ffn1_out, g_mix, w_mix_in, pool_w, pool_scale, conv_w, w_mix_out, g_xq, g_mem, w_xq, w_xk, w_xv, w_xo, g_ffn2, w_ffn2_in, w_ffn2_out, g_final):
    depth = g_ffn1.shape[0]
    assert depth == 1
    bp, seq, d = x_prompt.shape
    bs, tdec, _ = x_sample.shape
    l = 0

    poolw = pool_w[l].astype(BF16)
    wk = w_xk[l].astype(BF16)
    wv = w_xv[l].astype(BF16)
    any_space = pl.BlockSpec(memory_space=pl.ANY)
    stage_shape = (CAST_SLOTS, CAST_ROWS, CAST_COLS)
    g1, gmix, gq, gmem, g2 = g_ffn1[l:l + 1], g_mix[l:l + 1], g_xq[l:l + 1], g_mem[l:l + 1], g_ffn2[l:l + 1]
    gf = g_final.reshape(1, d)
    pscale = pool_scale[l:l + 1]
    convw = conv_w[l]

    mk, mv, kt, vb = pl.pallas_call(
        _memkv_kernel,
        grid=(bp,),
        in_specs=[pl.BlockSpec((None, N_MEM, d), lambda b: (b, 0, 0)),
                  _resident(gmem), _resident(wk), _resident(wv)],
        out_specs=[pl.BlockSpec(memory_space=pl.ANY),
                   pl.BlockSpec(memory_space=pl.ANY),
                   pl.BlockSpec((None, d, N_MEM), lambda b: (b, 0, 0)),
                   pl.BlockSpec((None, N_MEM, d), lambda b: (b, 0, 0))],
        out_shape=[jax.ShapeDtypeStruct((1, bp, N_MEM, N_XHEADS, XHEAD_DIM), F32),
                   jax.ShapeDtypeStruct((1, bp, N_MEM, N_XHEADS, XHEAD_DIM), F32),
                   jax.ShapeDtypeStruct((bp, d, N_MEM), BF16),
                   jax.ShapeDtypeStruct((bp, N_MEM, d), BF16)],
        scratch_shapes=[pltpu.VMEM((2, N_MEM, d), F32),
                        pltpu.VMEM((2, N_MEM, d), F32),
                        pltpu.SemaphoreType.DMA((2, 2, N_XHEADS))],
        compiler_params=_params(1),
        name="mem_kv",
    )(mem_prompt, gmem, wk, wv)

    tm = TM_PROMPT
    mix_cast_shapes = [(d, 2 * D_FF), (D_FF, d), (d, D_POOL + 3 * D_CONV), (D_POOL + D_CONV, d), (d, d)]
    x2p, npool_p, nconv_p, w1in, w1out, wmixin, wmixout, wq = pl.pallas_call(
        _mix_prompt_kernel,
        grid=(bp, seq // tm),
        in_specs=[pl.BlockSpec((None, tm, d), lambda b, j: (b, j, 0)),
                  _resident(g1), any_space, any_space, _resident(gmix), any_space, _resident(poolw),
                  _resident(pscale), _resident(convw), any_space, any_space],
        out_specs=[pl.BlockSpec((None, tm, d), lambda b, j: (b, j, 0)),
                   pl.BlockSpec((POOL_HIST, bp, D_POOL), lambda b, j: (0, 0, 0)),
                   pl.BlockSpec((None, CONV_HIST, D_CONV), lambda b, j: (b, 0, 0))]
                  + [any_space] * len(mix_cast_shapes),
        out_shape=[jax.ShapeDtypeStruct((bp, seq, d), F32),
                   jax.ShapeDtypeStruct((POOL_HIST, bp, D_POOL), F32),
                   jax.ShapeDtypeStruct((bp, CONV_HIST, D_CONV), F32)]
                  + [jax.ShapeDtypeStruct(s, BF16) for s in mix_cast_shapes],
        scratch_shapes=[pltpu.VMEM((tm, D_FF), BF16),
                        pltpu.VMEM((tm + POOL_PAD, D_POOL), F32),
                        pltpu.VMEM((tm + CONV_PAD, D_CONV), F32),
                        pltpu.VMEM((tm, d), BF16)]
                       + [pltpu.VMEM(s, BF16) for s in mix_cast_shapes]
                       + [pltpu.VMEM(stage_shape, F32),
                          pltpu.SemaphoreType.DMA((CAST_SLOTS,)),
                          pltpu.SemaphoreType.DMA((len(mix_cast_shapes),))],
        compiler_params=_params(2),
        name="mix_prompt",
    )(x_prompt, g1, w_ffn1_in, w_ffn1_out, gmix, w_mix_in, poolw, pscale, convw, w_mix_out, w_xq)

    tms = TM_SAMPLE
    ns = tms // tdec
    xs = x_sample.reshape(bs * tdec, d)
    hist_pool = jnp.transpose(state_pool[l], (1, 0, 2))
    mix_sample_weights = (g1, w1in, w1out, gmix, wmixin, poolw, pscale, convw, wmixout, gq, wq)
    x2s, qs, npool_s, nconv_s = pl.pallas_call(
        _mix_sample_kernel,
        grid=(bs * tdec // tms,),
        in_specs=[pl.BlockSpec((tms, d), lambda i: (i, 0)),
                  pl.BlockSpec((POOL_HIST, ns, D_POOL), lambda i: (0, i, 0)),
                  pl.BlockSpec((None, ns, CONV_HIST, D_CONV), lambda i: (0, i, 0, 0))]
                 + [_resident(w) for w in mix_sample_weights],
        out_specs=[pl.BlockSpec((tms, d), lambda i: (i, 0)),
                   pl.BlockSpec((tms, d), lambda i: (i, 0)),
                   pl.BlockSpec((POOL_HIST, ns, D_POOL), lambda i: (0, i, 0)),
                   pl.BlockSpec((ns, CONV_HIST, D_CONV), lambda i: (i, 0, 0))],
        out_shape=[jax.ShapeDtypeStruct((bs * tdec, d), F32),
                   jax.ShapeDtypeStruct((bs * tdec, d), F32),
                   jax.ShapeDtypeStruct((POOL_HIST, bs, D_POOL), F32),
                   jax.ShapeDtypeStruct((bs, CONV_HIST, D_CONV), F32)],
        scratch_shapes=[pltpu.VMEM((tms, D_FF), BF16),
                        pltpu.VMEM((len(POOL_WINDOWS), tms, POOL_GROUP), F32),
                        pltpu.VMEM((len(POOL_WINDOWS), tms, POOL_GROUP), F32),
                        pltpu.VMEM((ns, CONV_PAD + tdec, D_CONV), F32),
                        pltpu.VMEM((tms, d), BF16)],
        compiler_params=_params(1),
        name="mix_sample",
    )(xs, hist_pool, state_conv, *mix_sample_weights)

    nj = seq // tm
    n_steps = bp * nj
    seqs_per_step = bs // n_steps
    assert seqs_per_step * n_steps == bs
    assert (2 * seqs_per_step, N_MEM, d) == stage_shape
    qrows = seqs_per_step * tdec
    tail_cast_shapes = [(d, d), (d, 2 * D_FF), (D_FF, d)]
    y_prompt, os_, wo, w2in, w2out = pl.pallas_call(
        _tail_prompt_kernel,
        grid=(bp, nj),
        in_specs=[pl.BlockSpec((None, tm, d), lambda b, j: (b, j, 0)),
                  pl.BlockSpec((None, d, N_MEM), lambda b, j: (b, 0, 0)),
                  pl.BlockSpec((None, N_MEM, d), lambda b, j: (b, 0, 0)),
                  pl.BlockSpec((qrows, d), lambda b, j: (b * nj + j, 0)),
                  any_space, any_space,
                  _resident(gq), _resident(wq), any_space, _resident(g2), any_space, any_space, _resident(gf)],
        out_specs=[pl.BlockSpec((None, tm, d), lambda b, j: (b, j, 0)),
                   pl.BlockSpec((qrows, d), lambda b, j: (b * nj + j, 0))]
                  + [any_space] * len(tail_cast_shapes),
        out_shape=[jax.ShapeDtypeStruct((bp, seq, d), F32),
                   jax.ShapeDtypeStruct((bs * tdec, d), F32)]
                  + [jax.ShapeDtypeStruct(s, BF16) for s in tail_cast_shapes],
        scratch_shapes=[pltpu.VMEM((tm, D_FF), BF16),
                        pltpu.VMEM((tm, d), BF16)]
                       + [pltpu.VMEM(s, BF16) for s in tail_cast_shapes]
                       + [pltpu.VMEM((2,) + stage_shape, F32),
                          pltpu.SemaphoreType.DMA((2, 2, N_XHEADS)),
                          pltpu.SemaphoreType.DMA((CAST_SLOTS,)),
                          pltpu.SemaphoreType.DMA((len(tail_cast_shapes),))],
        compiler_params=_params(2),
        name="tail_prompt",
    )(x2p, kt, vb, qs, cache_mem_k, cache_mem_v, gq, wq, w_xo, g2, w_ffn2_in, w_ffn2_out, gf)

    tail_sample_weights = (wo, g2, w2in, w2out, gf)
    y_sample = pl.pallas_call(
        _tail_sample_kernel,
        grid=(bs * tdec // tms,),
        in_specs=[pl.BlockSpec((tms, d), lambda i: (i, 0)),
                  pl.BlockSpec((tms, d), lambda i: (i, 0))]
                 + [_resident(w) for w in tail_sample_weights],
        out_specs=pl.BlockSpec((tms, d), lambda i: (i, 0)),
        out_shape=jax.ShapeDtypeStruct((bs * tdec, d), F32),
        scratch_shapes=[pltpu.VMEM((tms, D_FF), BF16)],
        compiler_params=_params(1),
        name="tail_sample",
    )(x2s, os_, *tail_sample_weights)

    return (y_prompt,
            y_sample.reshape(bs, tdec, d),
            jnp.transpose(npool_p, (1, 0, 2))[None],
            nconv_p[None],
            mk,
            mv,
            jnp.transpose(npool_s, (1, 0, 2))[None],
            nconv_s[None])
```

```python
import functools

import jax
import jax.numpy as jnp
from jax import lax
from jax.experimental import pallas as pl
from jax.experimental.pallas import tpu as pltpu

D_MODEL = 1024
D_POOL = 512
D_CONV = 512
POOL_WINDOWS = (2, 4, 8, 16)
POOL_GROUP = 128
POOL_HIST = 15
CONV_HIST = 2
D_FF = 2816
N_MEM = 256
N_XHEADS = 4
XHEAD_DIM = 256
EPS = 1e-6
PAST_LEN = 16384

LANES = 128
SUBLANES = 8
FF_CHUNK = 512
FF_CHUNKS = tuple((s, min(FF_CHUNK, D_FF - s)) for s in range(0, D_FF, FF_CHUNK))
POOL_PAD = 16
CONV_PAD = 8
VMEM_LIMIT_BYTES = 60 * 1024 * 1024

CAST_SLOTS, CAST_ROWS, CAST_COLS = 8, 256, 1024
EDGE_ROW_PARTS = 2

TM_PROMPT = 512
TM_SAMPLE = 256

F32 = jnp.float32
BF16 = jnp.bfloat16


def _rms(x, g):
    y = x * lax.rsqrt(jnp.mean(x * x, axis=-1, keepdims=True) + EPS)
    return y * g


def _dot(a, b):
    return jnp.dot(a, b, preferred_element_type=F32)


def _dot_rows(a, b, parts):
    rows = a.shape[0] // parts
    return jnp.concatenate([_dot(a[i * rows:(i + 1) * rows], b) for i in range(parts)], axis=0)


def _ffn_half(x, g_ref, win_ref, wout_ref, act_ref):
    m = x.shape[0]
    h = _rms(x, g_ref[...]).astype(BF16)
    for i, (s0, n) in enumerate(FF_CHUNKS):
        parts = EDGE_ROW_PARTS if i == 0 else 1
        gate = _dot_rows(h, win_ref[:, s0:s0 + n], parts)
        up = _dot_rows(h, win_ref[:, D_FF + s0:D_FF + s0 + n], parts)
        silu = gate * (1.0 / (1.0 + jnp.exp(-gate)))
        act_ref[0:m, s0:s0 + n] = (silu * up).astype(BF16)
    y = _dot_rows(act_ref[0:m, :], wout_ref[...], EDGE_ROW_PARTS)
    return x + 0.5 * y


def _cast_weights_to_vmem(weights, stage_ref, sem):
    n_slots, cast_rows, cast_cols = stage_ref.shape
    blocks = []
    for w_hbm, dst in weights:
        rows, cols = dst.shape
        for r0 in range(0, rows, cast_rows):
            for c0 in range(0, cols, cast_cols):
                blocks.append((w_hbm, dst, r0, min(cast_rows, rows - r0), c0, min(cast_cols, cols - c0)))

    def copy(i):
        w_hbm, _, r0, nr, c0, nc = blocks[i]
        slot = i % n_slots
        return pltpu.make_async_copy(w_hbm.at[0, pl.ds(r0, nr), pl.ds(c0, nc)],
                                     stage_ref.at[slot, pl.ds(0, nr), pl.ds(0, nc)], sem.at[slot])

    for i in range(min(n_slots, len(blocks))):
        copy(i).start()
    for i, (_, dst, r0, nr, c0, nc) in enumerate(blocks):
        copy(i).wait()
        dst[r0:r0 + nr, c0:c0 + nc] = stage_ref[i % n_slots, 0:nr, 0:nc].astype(BF16)
        if i + n_slots < len(blocks):
            copy(i + n_slots).start()


def _export_copies(pairs, sem):
    return [pltpu.make_async_copy(src, dst, sem.at[k]) for k, (src, dst) in enumerate(pairs)]


def _window_sum(x, w):
    s = x
    sh = 1
    while sh < w:
        s = s + pltpu.roll(s, sh, 0)
        sh *= 2
    return s


def _pool_group(ext, a, pos, gi, poolw_ref, pscale_ref, extract):
    w = POOL_WINDOWS[gi]
    ssum = extract(_window_sum(ext, w))
    cnt = jnp.minimum(pos + 1, w).astype(F32)
    d = (ssum / cnt - a).astype(BF16)
    y = _dot(d, poolw_ref[gi])
    return y * pscale_ref[:, gi * POOL_GROUP:(gi + 1) * POOL_GROUP]


def _conv3(ext, convw_ref, extract):
    u0 = extract(pltpu.roll(ext, 2, 0))
    u1 = extract(pltpu.roll(ext, 1, 0))
    u2 = extract(ext)
    return convw_ref[0:1, :] * u0 + convw_ref[1:2, :] * u1 + convw_ref[2:3, :] * u2


def _memkv_copies(kscr, vscr, mk_hbm, mv_hbm, sem, b, slot):
    copies = []
    for h in range(N_XHEADS):
        cols = pl.ds(h * XHEAD_DIM, XHEAD_DIM)
        copies.append(pltpu.make_async_copy(kscr.at[slot, :, cols], mk_hbm.at[0, b, :, h, :], sem.at[slot, 0, h]))
        copies.append(pltpu.make_async_copy(vscr.at[slot, :, cols], mv_hbm.at[0, b, :, h, :], sem.at[slot, 1, h]))
    return copies


def _memkv_kernel(mem_ref, g_ref, wk_ref, wv_ref, mk_hbm, mv_hbm, kt_ref, vb_ref, kscr, vscr, sem):
    b = pl.program_id(0)
    nb = pl.num_programs(0)
    slot = lax.rem(b, 2)

    @pl.when(b >= 2)
    def _():
        for c in _memkv_copies(kscr, vscr, mk_hbm, mv_hbm, sem, b - 2, slot):
            c.wait()

    mn = _rms(mem_ref[...], g_ref[...]).astype(BF16)
    k = _dot(mn, wk_ref[...])
    v = _dot(mn, wv_ref[...])
    kscr[slot] = k
    vscr[slot] = v
    kt_ref[...] = k.T.astype(BF16)
    vb_ref[...] = v.astype(BF16)
    for c in _memkv_copies(kscr, vscr, mk_hbm, mv_hbm, sem, b, slot):
        c.start()

    @pl.when(b == nb - 1)
    def _():
        for c in _memkv_copies(kscr, vscr, mk_hbm, mv_hbm, sem, b - 1, 1 - slot):
            c.wait()
        for c in _memkv_copies(kscr, vscr, mk_hbm, mv_hbm, sem, b, slot):
            c.wait()


def _mix_prompt_kernel(x_ref, g1_ref, w1in_hbm, w1out_hbm, gmix_ref, wmixin_hbm, poolw_ref,
                       pscale_ref, convw_ref, wmixout_hbm, wq_hbm,
                       x2_ref, npool_ref, nconv_ref, w1in_out, w1out_out, wmixin_out, wmixout_out, wq_out,
                       act_ref, epool_ref, econv_ref, mixed_ref,
                       w1in_ref, w1out_ref, wmixin_ref, wmixout_ref, wq_ref, stage_ref, stage_sem, export_sem):
    tm = x_ref.shape[0]
    b = pl.program_id(0)
    j = pl.program_id(1)
    first = jnp.logical_and(b == 0, j == 0)
    last = jnp.logical_and(b == pl.num_programs(0) - 1, j == pl.num_programs(1) - 1)
    exports = _export_copies([(w1in_ref, w1in_out), (w1out_ref, w1out_out), (wmixin_ref, wmixin_out),
                              (wmixout_ref, wmixout_out), (wq_ref, wq_out)], export_sem)

    @pl.when(first)
    def _():
        _cast_weights_to_vmem([(w1in_hbm, w1in_ref), (w1out_hbm, w1out_ref), (wmixin_hbm, wmixin_ref),
                               (wmixout_hbm, wmixout_ref), (wq_hbm, wq_ref)], stage_ref, stage_sem)
        for c in exports:
            c.start()

    @pl.when(j == 0)
    def _():
        epool_ref[0:POOL_PAD, :] = jnp.zeros((POOL_PAD, D_POOL), F32)
        econv_ref[0:CONV_PAD, :] = jnp.zeros((CONV_PAD, D_CONV), F32)

    x1 = _ffn_half(x_ref[...], g1_ref, w1in_ref, w1out_ref, act_ref)
    z = _dot_rows(_rms(x1, gmix_ref[...]).astype(BF16), wmixin_ref[...], EDGE_ROW_PARTS)
    a = z[:, 0:D_POOL]
    cb = z[:, D_POOL:D_POOL + D_CONV]
    u = z[:, D_POOL + D_CONV:D_POOL + 2 * D_CONV] * z[:, D_POOL + 2 * D_CONV:]
    epool_ref[POOL_PAD:, :] = a
    econv_ref[CONV_PAD:, :] = u

    pos = j * tm + lax.broadcasted_iota(jnp.int32, (tm, 1), 0)
    for gi in range(len(POOL_WINDOWS)):
        sl = slice(gi * POOL_GROUP, (gi + 1) * POOL_GROUP)
        p = _pool_group(epool_ref[:, sl], a[:, sl], pos, gi, poolw_ref, pscale_ref,
                        lambda s: s[POOL_PAD:, :])
        mixed_ref[:, sl] = p.astype(BF16)
    yc = _conv3(econv_ref[...], convw_ref, lambda s: s[CONV_PAD:, :])
    mixed_ref[:, D_POOL:] = (cb * yc).astype(BF16)
    x2_ref[...] = x1 + _dot_rows(mixed_ref[...], wmixout_ref[...], EDGE_ROW_PARTS)

    epool_ref[0:POOL_PAD, :] = epool_ref[tm:tm + POOL_PAD, :]
    econv_ref[0:CONV_PAD, :] = econv_ref[tm:tm + CONV_PAD, :]

    @pl.when(j == pl.num_programs(1) - 1)
    def _():
        for r in range(POOL_HIST):
            row = POOL_PAD - POOL_HIST + r
            npool_ref[r, pl.ds(b, 1), :] = epool_ref[row:row + 1, :]
        nconv_ref[...] = econv_ref[CONV_PAD - CONV_HIST:CONV_PAD, :]

    @pl.when(last)
    def _():
        for c in exports:
            c.wait()


def _mix_sample_kernel(x_ref, hpool_ref, hconv_ref, g1_ref, w1in_ref, w1out_ref, gmix_ref,
                       wmixin_ref, poolw_ref, pscale_ref, convw_ref, wmixout_ref, gq_ref, wq_ref,
                       x2_ref, q_ref, npool_ref, nconv_ref,
                       act_ref, a_ref, d_ref, econv_ref, mixed_ref):
    tm = x_ref.shape[0]
    ns = hpool_ref.shape[1]
    t = tm // ns
    crow = CONV_PAD + t

    x1 = _ffn_half(x_ref[...], g1_ref, w1in_ref, w1out_ref, act_ref)
    z = _dot_rows(_rms(x1, gmix_ref[...]).astype(BF16), wmixin_ref[...], EDGE_ROW_PARTS)
    cb = z[:, D_POOL:D_POOL + D_CONV]
    u = z[:, D_POOL + D_CONV:D_POOL + 2 * D_CONV] * z[:, D_POOL + 2 * D_CONV:]

    n_ext = POOL_HIST + t
    for gi, w in enumerate(POOL_WINDOWS):
        sl = slice(gi * POOL_GROUP, (gi + 1) * POOL_GROUP)
        a_ref[gi] = z[:, sl]
        new = [a_ref[gi, pl.ds(tt, ns, stride=t), :] for tt in range(t)]
        sums = {i: hpool_ref[i, :, sl] for i in range(POOL_HIST)}
        sums.update({POOL_HIST + tt: new[tt] for tt in range(t)})
        for r in range(POOL_HIST):
            npool_ref[r, :, sl] = sums[n_ext - POOL_HIST + r]
        sh = 1
        while sh < w:
            sums = {i: sums[i] + sums[i - sh] for i in sums if (i - sh) in sums}
            sh *= 2
        for tt in range(t):
            cnt = float(min(PAST_LEN + tt + 1, w))
            d_ref[gi, pl.ds(tt, ns, stride=t), :] = sums[POOL_HIST + tt] / cnt - new[tt]
        y = _dot(d_ref[gi].astype(BF16), poolw_ref[gi])
        mixed_ref[:, sl] = (y * pscale_ref[:, sl]).astype(BF16)

    econv_ref[:, 0:CONV_PAD - CONV_HIST, :] = jnp.zeros((ns, CONV_PAD - CONV_HIST, D_CONV), F32)
    econv_ref[:, CONV_PAD - CONV_HIST:CONV_PAD, :] = hconv_ref[...]
    econv_ref[:, CONV_PAD:, :] = u.reshape(ns, t, D_CONV)
    yc = _conv3(econv_ref[...].reshape(ns * crow, D_CONV), convw_ref,
                lambda s: s.reshape(ns, crow, D_CONV)[:, CONV_PAD:, :].reshape(tm, D_CONV))
    mixed_ref[:, D_POOL:] = (cb * yc).astype(BF16)
    x2 = x1 + _dot(mixed_ref[...], wmixout_ref[...])
    x2_ref[...] = x2
    q_ref[...] = _queries(x2, gq_ref, wq_ref)
    nconv_ref[...] = econv_ref[:, crow - CONV_HIST:crow, :]


def _softmax(s):
    e = jnp.exp(s - jnp.max(s, axis=-1, keepdims=True))
    return e / jnp.sum(e, axis=-1, keepdims=True)


def _queries(x, gq_ref, wq_ref):
    return _dot_rows(_rms(x, gq_ref[...]).astype(BF16), wq_ref[...], EDGE_ROW_PARTS) * (XHEAD_DIM ** -0.5)


def _kv_copies(k_hbm, v_hbm, kv_ref, sem, step, slot):
    ns = kv_ref.shape[1] // 2
    seqs = pl.ds(step * ns, ns)
    copies = []
    for h in range(N_XHEADS):
        cols = pl.ds(h * XHEAD_DIM, XHEAD_DIM)
        copies.append(pltpu.make_async_copy(k_hbm.at[0, seqs, :, h, :], kv_ref.at[slot, pl.ds(0, ns), :, cols],
                                            sem.at[slot, 0, h]))
        copies.append(pltpu.make_async_copy(v_hbm.at[0, seqs, :, h, :], kv_ref.at[slot, pl.ds(ns, ns), :, cols],
                                            sem.at[slot, 1, h]))
    return copies


def _cached_scores(q_ref, kv_ref):
    ns = kv_ref.shape[0] // 2
    t = q_ref.shape[0] // ns
    d = q_ref.shape[1]
    rows_per_seq = N_XHEADS * t
    row_head = lax.broadcasted_iota(jnp.int32, (rows_per_seq, d), 0) // t
    col_head = lax.broadcasted_iota(jnp.int32, (rows_per_seq, d), 1) // XHEAD_DIM
    own_head = row_head == col_head
    scores = []
    for s in range(ns):
        qs = q_ref[s * t:(s + 1) * t, :]
        qe = jnp.where(own_head, jnp.concatenate([qs] * N_XHEADS, axis=0), 0.0).astype(BF16)
        scores.append(lax.dot_general(qe, kv_ref[s].astype(BF16), (((1,), (1,)), ((), ())),
                                      preferred_element_type=F32))
    return jnp.concatenate(scores, axis=0)


def _cached_outputs(scores, kv_ref, o_ref):
    ns = kv_ref.shape[0] // 2
    t = o_ref.shape[0] // ns
    rows_per_seq = N_XHEADS * t
    p = _softmax(scores).astype(BF16)
    for s in range(ns):
        oa = _dot(p[s * rows_per_seq:(s + 1) * rows_per_seq, :], kv_ref[ns + s].astype(BF16))
        for h in range(N_XHEADS):
            sl = slice(h * XHEAD_DIM, (h + 1) * XHEAD_DIM)
            o_ref[s * t:(s + 1) * t, sl] = oa[h * t:(h + 1) * t, sl]


def _tail_prompt_kernel(x_ref, kt_ref, v_ref, qs_ref, ck_hbm, cv_hbm, gq_ref, wq_ref, wo_hbm, g2_ref,
                        w2in_hbm, w2out_hbm, gf_ref,
                        y_ref, os_ref, wo_out, w2in_out, w2out_out,
                        act_ref, o_ref, wo_ref, w2in_ref, w2out_ref, kv_ref, kv_sem, stage_sem, export_sem):
    step = pl.program_id(0) * pl.num_programs(1) + pl.program_id(1)
    n_steps = pl.num_programs(0) * pl.num_programs(1)
    exports = _export_copies([(wo_ref, wo_out), (w2in_ref, w2in_out), (w2out_ref, w2out_out)], export_sem)

    slot = lax.rem(step, 2)

    @pl.when(step == 0)
    def _():
        _cast_weights_to_vmem([(wo_hbm, wo_ref), (w2in_hbm, w2in_ref), (w2out_hbm, w2out_ref)],
                              kv_ref.at[0], stage_sem)
        for c in exports:
            c.start()
        for c in _kv_copies(ck_hbm, cv_hbm, kv_ref, kv_sem, 0, 0):
            c.start()

    for c in _kv_copies(ck_hbm, cv_hbm, kv_ref, kv_sem, step, slot):
        c.wait()

    @pl.when(step + 1 < n_steps)
    def _():
        for c in _kv_copies(ck_hbm, cv_hbm, kv_ref, kv_sem, step + 1, 1 - slot):
            c.start()

    kv = kv_ref.at[slot]
    sample_scores = _cached_scores(qs_ref, kv)
    x2 = x_ref[...]
    q = _queries(x2, gq_ref, wq_ref).astype(BF16)
    heads = [slice(h * XHEAD_DIM, (h + 1) * XHEAD_DIM) for h in range(N_XHEADS)]
    scores = [_dot(q[:, heads[0]], kt_ref[heads[0], :])]
    for h in range(N_XHEADS):
        if h + 1 < N_XHEADS:
            scores.append(_dot(q[:, heads[h + 1]], kt_ref[heads[h + 1], :]))
        p = _softmax(scores[h]).astype(BF16)
        o_ref[:, heads[h]] = _dot(p, v_ref[:, heads[h]]).astype(BF16)
    x3 = x2 + _dot_rows(o_ref[...], wo_ref[...], EDGE_ROW_PARTS)
    x4 = _ffn_half(x3, g2_ref, w2in_ref, w2out_ref, act_ref)
    _cached_outputs(sample_scores, kv, os_ref)
    y_ref[...] = _rms(x4, gf_ref[...])

    @pl.when(step == n_steps - 1)
    def _():
        for c in exports:
            c.wait()


def _tail_sample_kernel(x_ref, o_ref, wo_ref, g2_ref, w2in_ref, w2out_ref, gf_ref, y_ref, act_ref):
    x3 = x_ref[...] + _dot(o_ref[...].astype(BF16), wo_ref[...])
    x4 = _ffn_half(x3, g2_ref, w2in_ref, w2out_ref, act_ref)
    y_ref[...] = _rms(x4, gf_ref[...])


def _resident(arr):
    nd = arr.ndim
    return pl.BlockSpec(arr.shape, lambda *_: (0,) * nd, pipeline_mode=pl.Buffered(1))


def _params(n_grid):
    return pltpu.CompilerParams(dimension_semantics=("arbitrary",) * n_grid,
                                vmem_limit_bytes=VMEM_LIMIT_BYTES)


def kernel(x_prompt, x_sample, mem_prompt, state_pool, state_conv, cache_mem_k, cache_mem_v, g_ffn1, w_ffn1_in, w_ffn1_out, g_mix, w_mix_in, pool_w, pool_scale, conv_w, w_mix_out, g_xq, g_mem, w_xq, w_xk, w_xv, w_xo, g_ffn2, w_ffn2_in, w_ffn2_out, g_final):
    depth = g_ffn1.shape[0]
    assert depth == 1
    bp, seq, d = x_prompt.shape
    bs, tdec, _ = x_sample.shape
    l = 0

    poolw = pool_w[l].astype(BF16)
    wk = w_xk[l].astype(BF16)
    wv = w_xv[l].astype(BF16)
    any_space = pl.BlockSpec(memory_space=pl.ANY)
    stage_shape = (CAST_SLOTS, CAST_ROWS, CAST_COLS)
    g1, gmix, gq, gmem, g2 = g_ffn1[l:l + 1], g_mix[l:l + 1], g_xq[l:l + 1], g_mem[l:l + 1], g_ffn2[l:l + 1]
    gf = g_final.reshape(1, d)
    pscale = pool_scale[l:l + 1]
    convw = conv_w[l]

    mk, mv, kt, vb = pl.pallas_call(
        _memkv_kernel,
        grid=(bp,),
        in_specs=[pl.BlockSpec((None, N_MEM, d), lambda b: (b, 0, 0)),
                  _resident(gmem), _resident(wk), _resident(wv)],
        out_specs=[pl.BlockSpec(memory_space=pl.ANY),
                   pl.BlockSpec(memory_space=pl.ANY),
                   pl.BlockSpec((None, d, N_MEM), lambda b: (b, 0, 0)),
                   pl.BlockSpec((None, N_MEM, d), lambda b: (b, 0, 0))],
        out_shape=[jax.ShapeDtypeStruct((1, bp, N_MEM, N_XHEADS, XHEAD_DIM), F32),
                   jax.ShapeDtypeStruct((1, bp, N_MEM, N_XHEADS, XHEAD_DIM), F32),
                   jax.ShapeDtypeStruct((bp, d, N_MEM), BF16),
                   jax.ShapeDtypeStruct((bp, N_MEM, d), BF16)],
        scratch_shapes=[pltpu.VMEM((2, N_MEM, d), F32),
                        pltpu.VMEM((2, N_MEM, d), F32),
                        pltpu.SemaphoreType.DMA((2, 2, N_XHEADS))],
        compiler_params=_params(1),
        name="mem_kv",
    )(mem_prompt, gmem, wk, wv)

    tm = TM_PROMPT
    mix_cast_shapes = [(d, 2 * D_FF), (D_FF, d), (d, D_POOL + 3 * D_CONV), (D_POOL + D_CONV, d), (d, d)]
    x2p, npool_p, nconv_p, w1in, w1out, wmixin, wmixout, wq = pl.pallas_call(
        _mix_prompt_kernel,
        grid=(bp, seq // tm),
        in_specs=[pl.BlockSpec((None, tm, d), lambda b, j: (b, j, 0)),
                  _resident(g1), any_space, any_space, _resident(gmix), any_space, _resident(poolw),
                  _resident(pscale), _resident(convw), any_space, any_space],
        out_specs=[pl.BlockSpec((None, tm, d), lambda b, j: (b, j, 0)),
                   pl.BlockSpec((POOL_HIST, bp, D_POOL), lambda b, j: (0, 0, 0)),
                   pl.BlockSpec((None, CONV_HIST, D_CONV), lambda b, j: (b, 0, 0))]
                  + [any_space] * len(mix_cast_shapes),
        out_shape=[jax.ShapeDtypeStruct((bp, seq, d), F32),
                   jax.ShapeDtypeStruct((POOL_HIST, bp, D_POOL), F32),
                   jax.ShapeDtypeStruct((bp, CONV_HIST, D_CONV), F32)]
                  + [jax.ShapeDtypeStruct(s, BF16) for s in mix_cast_shapes],
        scratch_shapes=[pltpu.VMEM((tm, D_FF), BF16),
                        pltpu.VMEM((tm + POOL_PAD, D_POOL), F32),
                        pltpu.VMEM((tm + CONV_PAD, D_CONV), F32),
                        pltpu.VMEM((tm, d), BF16)]
                       + [pltpu.VMEM(s, BF16) for s in mix_cast_shapes]
                       + [pltpu.VMEM(stage_shape, F32),
                          pltpu.SemaphoreType.DMA((CAST_SLOTS,)),
                          pltpu.SemaphoreType.DMA((len(mix_cast_shapes),))],
        compiler_params=_params(2),
        name="mix_prompt",
    )(x_prompt, g1, w_ffn1_in, w_ffn1_out, gmix, w_mix_in, poolw, pscale, convw, w_mix_out, w_xq)

    tms = TM_SAMPLE
    ns = tms // tdec
    xs = x_sample.reshape(bs * tdec, d)
    hist_pool = jnp.transpose(state_pool[l], (1, 0, 2))
    mix_sample_weights = (g1, w1in, w1out, gmix, wmixin, poolw, pscale, convw, wmixout, gq, wq)
    x2s, qs, npool_s, nconv_s = pl.pallas_call(
        _mix_sample_kernel,
        grid=(bs * tdec // tms,),
        in_specs=[pl.BlockSpec((tms, d), lambda i: (i, 0)),
                  pl.BlockSpec((POOL_HIST, ns, D_POOL), lambda i: (0, i, 0)),
                  pl.BlockSpec((None, ns, CONV_HIST, D_CONV), lambda i: (0, i, 0, 0))]
                 + [_resident(w) for w in mix_sample_weights],
        out_specs=[pl.BlockSpec((tms, d), lambda i: (i, 0)),
                   pl.BlockSpec((tms, d), lambda i: (i, 0)),
                   pl.BlockSpec((POOL_HIST, ns, D_POOL), lambda i: (0, i, 0)),
                   pl.BlockSpec((ns, CONV_HIST, D_CONV), lambda i: (i, 0, 0))],
        out_shape=[jax.ShapeDtypeStruct((bs * tdec, d), F32),
                   jax.ShapeDtypeStruct((bs * tdec, d), F32),
                   jax.ShapeDtypeStruct((POOL_HIST, bs, D_POOL), F32),
                   jax.ShapeDtypeStruct((bs, CONV_HIST, D_CONV), F32)],
        scratch_shapes=[pltpu.VMEM((tms, D_FF), BF16),
                        pltpu.VMEM((len(POOL_WINDOWS), tms, POOL_GROUP), F32),
                        pltpu.VMEM((len(POOL_WINDOWS), tms, POOL_GROUP), F32),
                        pltpu.VMEM((ns, CONV_PAD + tdec, D_CONV), F32),
                        pltpu.VMEM((tms, d), BF16)],
        compiler_params=_params(1),
        name="mix_sample",
    )(xs, hist_pool, state_conv, *mix_sample_weights)

    nj = seq // tm
    n_steps = bp * nj
    seqs_per_step = bs // n_steps
    assert seqs_per_step * n_steps == bs
    assert (2 * seqs_per_step, N_MEM, d) == stage_shape
    qrows = seqs_per_step * tdec
    tail_cast_shapes = [(d, d), (d, 2 * D_FF), (D_FF, d)]
    y_prompt, os_, wo, w2in, w2out = pl.pallas_call(
        _tail_prompt_kernel,
        grid=(bp, nj),
        in_specs=[pl.BlockSpec((None, tm, d), lambda b, j: (b, j, 0)),
                  pl.BlockSpec((None, d, N_MEM), lambda b, j: (b, 0, 0)),
                  pl.BlockSpec((None, N_MEM, d), lambda b, j: (b, 0, 0)),
                  pl.BlockSpec((qrows, d), lambda b, j: (b * nj + j, 0)),
                  any_space, any_space,
                  _resident(gq), _resident(wq), any_space, _resident(g2), any_space, any_space, _resident(gf)],
        out_specs=[pl.BlockSpec((None, tm, d), lambda b, j: (b, j, 0)),
                   pl.BlockSpec((qrows, d), lambda b, j: (b * nj + j, 0))]
                  + [any_space] * len(tail_cast_shapes),
        out_shape=[jax.ShapeDtypeStruct((bp, seq, d), F32),
                   jax.ShapeDtypeStruct((bs * tdec, d), F32)]
                  + [jax.ShapeDtypeStruct(s, BF16) for s in tail_cast_shapes],
        scratch_shapes=[pltpu.VMEM((tm, D_FF), BF16),
                        pltpu.VMEM((tm, d), BF16)]
                       + [pltpu.VMEM(s, BF16) for s in tail_cast_shapes]
                       + [pltpu.VMEM((2,) + stage_shape, F32),
                          pltpu.SemaphoreType.DMA((2, 2, N_XHEADS)),
                          pltpu.SemaphoreType.DMA((CAST_SLOTS,)),
                          pltpu.SemaphoreType.DMA((len(tail_cast_shapes),))],
        compiler_params=_params(2),
        name="tail_prompt",
    )(x2p, kt, vb, qs, cache_mem_k, cache_mem_v, gq, wq, w_xo, g2, w_ffn2_in, w_ffn2_out, gf)

    tail_sample_weights = (wo, g2, w2in, w2out, gf)
    y_sample = pl.pallas_call(
        _tail_sample_kernel,
        grid=(bs * tdec // tms,),
        in_specs=[pl.BlockSpec((tms, d), lambda i: (i, 0)),
                  pl.BlockSpec((tms, d), lambda i: (i, 0))]
                 + [_resident(w) for w in tail_sample_weights],
        out_specs=pl.BlockSpec((tms, d), lambda i: (i, 0)),
        out_shape=jax.ShapeDtypeStruct((bs * tdec, d), F32),
        scratch_shapes=[pltpu.VMEM((tms, D_FF), BF16)],
        compiler_params=_params(1),
        name="tail_sample",
    )(x2s, os_, *tail_sample_weights)

    return (y_prompt,
            y_sample.reshape(bs, tdec, d),
            jnp.transpose(npool_p, (1, 0, 2))[None],
            nconv_p[None],
            mk,
            mv,
            jnp.transpose(npool_s, (1, 0, 2))[None],
            nconv_s[None])
```

```python
import jax
import jax.numpy as jnp
from jax import lax
from jax.experimental import pallas as pl
from jax.experimental.pallas import tpu as pltpu

D_MODEL = 1024
D_POOL = 512
D_CONV = 512
POOL_WINDOWS = (2, 4, 8, 16)
POOL_GROUP = 128
POOL_HIST = 15
CONV_HIST = 2
D_FF = 2816
N_MEM = 256
N_XHEADS = 4
XHEAD_DIM = 256
EPS = 1e-6
PAST_LEN = 16384

FF_CHUNK = 512
FF_CHUNKS = tuple((s, min(FF_CHUNK, D_FF - s)) for s in range(0, D_FF, FF_CHUNK))
POOL_PAD = 16
CONV_PAD = 8
VMEM_LIMIT_BYTES = 60 * 1024 * 1024

CAST_SLOTS, CAST_ROWS, CAST_COLS = 8, 256, 1024
EDGE_ROW_PARTS = 2

TM_PROMPT = 512

F32 = jnp.float32
BF16 = jnp.bfloat16


def _rms(x, g):
    y = x * lax.rsqrt(jnp.mean(x * x, axis=-1, keepdims=True) + EPS)
    return y * g


def _dot(a, b):
    return jnp.dot(a, b, preferred_element_type=F32)


def _dot_rows(a, b, parts):
    rows = a.shape[0] // parts
    return jnp.concatenate([_dot(a[i * rows:(i + 1) * rows], b) for i in range(parts)], axis=0)


def _ffn_half(x, g_ref, win_ref, wout_ref, act_ref):
    m = x.shape[0]
    h = _rms(x, g_ref[...]).astype(BF16)
    for i, (s0, n) in enumerate(FF_CHUNKS):
        parts = EDGE_ROW_PARTS if i == 0 else 1
        gate = _dot_rows(h, win_ref[:, s0:s0 + n], parts)
        up = _dot_rows(h, win_ref[:, D_FF + s0:D_FF + s0 + n], parts)
        silu = gate * (1.0 / (1.0 + jnp.exp(-gate)))
        act_ref[0:m, s0:s0 + n] = (silu * up).astype(BF16)
    y = _dot_rows(act_ref[0:m, :], wout_ref[...], EDGE_ROW_PARTS)
    return x + 0.5 * y


def _cast_weights_to_vmem(weights, stage_ref, sem):
    n_slots, cast_rows, cast_cols = stage_ref.shape
    blocks = []
    for w_hbm, dst in weights:
        rows, cols = dst.shape
        for r0 in range(0, rows, cast_rows):
            for c0 in range(0, cols, cast_cols):
                blocks.append((w_hbm, dst, r0, min(cast_rows, rows - r0), c0, min(cast_cols, cols - c0)))

    def copy(i):
        w_hbm, _, r0, nr, c0, nc = blocks[i]
        slot = i % n_slots
        return pltpu.make_async_copy(w_hbm.at[0, pl.ds(r0, nr), pl.ds(c0, nc)],
                                     stage_ref.at[slot, pl.ds(0, nr), pl.ds(0, nc)], sem.at[slot])

    for i in range(min(n_slots, len(blocks))):
        copy(i).start()
    for i, (_, dst, r0, nr, c0, nc) in enumerate(blocks):
        copy(i).wait()
        dst[r0:r0 + nr, c0:c0 + nc] = stage_ref[i % n_slots, 0:nr, 0:nc].astype(BF16)
        if i + n_slots < len(blocks):
            copy(i + n_slots).start()


def _window_sum(x, w):
    s = x
    sh = 1
    while sh < w:
        s = s + pltpu.roll(s, sh, 0)
        sh *= 2
    return s


def _pool_delta_rows(ext, a, pos, w, extract):
    ssum = extract(_window_sum(ext, w))
    cnt = jnp.minimum(pos + 1, w).astype(F32)
    return ssum / cnt - a


def _pool_delta_steps(hist, new, w, pos0):
    n_hist = len(hist)
    sums = dict(enumerate(hist + new))
    sh = 1
    while sh < w:
        sums = {i: sums[i] + sums[i - sh] for i in sums if (i - sh) in sums}
        sh *= 2
    return [sums[n_hist + tt] / float(min(pos0 + tt + 1, w)) - new[tt] for tt in range(len(new))]


def _conv3(ext, convw_ref, extract):
    u0 = extract(pltpu.roll(ext, 2, 0))
    u1 = extract(pltpu.roll(ext, 1, 0))
    u2 = extract(ext)
    return convw_ref[0:1, :] * u0 + convw_ref[1:2, :] * u1 + convw_ref[2:3, :] * u2


def _memkv_copies(kscr, vscr, mk_hbm, mv_hbm, sem, b, slot):
    copies = []
    for h in range(N_XHEADS):
        cols = pl.ds(h * XHEAD_DIM, XHEAD_DIM)
        copies.append(pltpu.make_async_copy(kscr.at[slot, :, cols], mk_hbm.at[0, b, :, h, :], sem.at[slot, 0, h]))
        copies.append(pltpu.make_async_copy(vscr.at[slot, :, cols], mv_hbm.at[0, b, :, h, :], sem.at[slot, 1, h]))
    return copies


def _memkv_kernel(mem_ref, g_ref, wk_ref, wv_ref, mk_hbm, mv_hbm, kt_ref, vb_ref, kscr, vscr, sem):
    b = pl.program_id(0)
    nb = pl.num_programs(0)
    slot = lax.rem(b, 2)

    @pl.when(b >= 2)
    def _():
        for c in _memkv_copies(kscr, vscr, mk_hbm, mv_hbm, sem, b - 2, slot):
            c.wait()

    mn = _rms(mem_ref[...], g_ref[...]).astype(BF16)
    k = _dot(mn, wk_ref[...])
    v = _dot(mn, wv_ref[...])
    kscr[slot] = k
    vscr[slot] = v
    kt_ref[...] = k.T.astype(BF16)
    vb_ref[...] = v.astype(BF16)
    for c in _memkv_copies(kscr, vscr, mk_hbm, mv_hbm, sem, b, slot):
        c.start()

    @pl.when(b == nb - 1)
    def _():
        for c in _memkv_copies(kscr, vscr, mk_hbm, mv_hbm, sem, b - 1, 1 - slot):
            c.wait()
        for c in _memkv_copies(kscr, vscr, mk_hbm, mv_hbm, sem, b, slot):
            c.wait()


def _mix_kernel(x_ref, xs_ref, hpool_ref, hconv_ref, g1_ref, w1in_hbm, w1out_hbm, gmix_ref, wmixin_hbm,
                poolw_ref, pscale_ref, convw_ref, wmixout_hbm,
                x2_ref, x2s_ref, npool_ref, nconv_ref, npool_s_ref, nconv_s_ref,
                act_ref, epool_ref, econv_ref, a_ref, d_ref, econv_s_ref, mixed_ref,
                w1in_ref, w1out_ref, wmixin_ref, wmixout_ref, stage_ref, stage_sem):
    tm = x_ref.shape[0]
    ns = hpool_ref.shape[1]
    t = xs_ref.shape[0] // ns
    crow = CONV_PAD + t
    b = pl.program_id(0)
    j = pl.program_id(1)

    @pl.when(jnp.logical_and(b == 0, j == 0))
    def _():
        _cast_weights_to_vmem([(w1in_hbm, w1in_ref), (w1out_hbm, w1out_ref), (wmixin_hbm, wmixin_ref),
                               (wmixout_hbm, wmixout_ref)], stage_ref, stage_sem)

    @pl.when(j == 0)
    def _():
        epool_ref[0:POOL_PAD, :] = jnp.zeros((POOL_PAD, D_POOL), F32)
        econv_ref[0:CONV_PAD, :] = jnp.zeros((CONV_PAD, D_CONV), F32)

    x1 = _ffn_half(jnp.concatenate([x_ref[...], xs_ref[...]], axis=0), g1_ref, w1in_ref, w1out_ref, act_ref)
    z = _dot_rows(_rms(x1, gmix_ref[...]).astype(BF16), wmixin_ref[...], EDGE_ROW_PARTS)
    a = z[:, 0:D_POOL]
    cb = z[:, D_POOL:D_POOL + D_CONV]
    u = z[:, D_POOL + D_CONV:D_POOL + 2 * D_CONV] * z[:, D_POOL + 2 * D_CONV:]

    epool_ref[POOL_PAD:, :] = a[0:tm]
    pos = j * tm + lax.broadcasted_iota(jnp.int32, (tm, 1), 0)
    for gi, w in enumerate(POOL_WINDOWS):
        sl = slice(gi * POOL_GROUP, (gi + 1) * POOL_GROUP)
        d_prompt = _pool_delta_rows(epool_ref[:, sl], a[0:tm, sl], pos, w, lambda s: s[POOL_PAD:, :])
        a_ref[gi] = a[tm:, sl]
        hist = [hpool_ref[i, :, sl] for i in range(POOL_HIST)]
        new = [a_ref[gi, pl.ds(tt, ns, stride=t), :] for tt in range(t)]
        for r, slab in enumerate((hist + new)[-POOL_HIST:]):
            npool_s_ref[r, :, sl] = slab
        for tt, slab in enumerate(_pool_delta_steps(hist, new, w, PAST_LEN)):
            d_ref[gi, pl.ds(tt, ns, stride=t), :] = slab
        d_all = jnp.concatenate([d_prompt, d_ref[gi]], axis=0).astype(BF16)
        mixed_ref[:, sl] = (_dot(d_all, poolw_ref[gi]) * pscale_ref[:, sl]).astype(BF16)

    econv_ref[CONV_PAD:, :] = u[0:tm]
    yc_prompt = _conv3(econv_ref[...], convw_ref, lambda s: s[CONV_PAD:, :])
    econv_s_ref[:, 0:CONV_PAD - CONV_HIST, :] = jnp.zeros((ns, CONV_PAD - CONV_HIST, D_CONV), F32)
    econv_s_ref[:, CONV_PAD - CONV_HIST:CONV_PAD, :] = hconv_ref[...]
    econv_s_ref[:, CONV_PAD:, :] = u[tm:].reshape(ns, t, D_CONV)
    yc_sample = _conv3(econv_s_ref[...].reshape(ns * crow, D_CONV), convw_ref,
                       lambda s: s.reshape(ns, crow, D_CONV)[:, CONV_PAD:, :].reshape(ns * t, D_CONV))
    mixed_ref[:, D_POOL:] = (cb * jnp.concatenate([yc_prompt, yc_sample], axis=0)).astype(BF16)

    x2 = x1 + _dot_rows(mixed_ref[...], wmixout_ref[...], EDGE_ROW_PARTS)
    x2_ref[...] = x2[0:tm]
    x2s_ref[...] = x2[tm:]
    nconv_s_ref[...] = econv_s_ref[:, crow - CONV_HIST:crow, :]

    epool_ref[0:POOL_PAD, :] = epool_ref[tm:tm + POOL_PAD, :]
    econv_ref[0:CONV_PAD, :] = econv_ref[tm:tm + CONV_PAD, :]

    @pl.when(j == pl.num_programs(1) - 1)
    def _():
        for r in range(POOL_HIST):
            row = POOL_PAD - POOL_HIST + r
            npool_ref[r, pl.ds(b, 1), :] = epool_ref[row:row + 1, :]
        nconv_ref[...] = econv_ref[CONV_PAD - CONV_HIST:CONV_PAD, :]


def _softmax(s):
    e = jnp.exp(s - jnp.max(s, axis=-1, keepdims=True))
    return e / jnp.sum(e, axis=-1, keepdims=True)


def _queries(x, gq_ref, wq_ref):
    return _dot_rows(_rms(x, gq_ref[...]).astype(BF16), wq_ref[...], EDGE_ROW_PARTS) * (XHEAD_DIM ** -0.5)


def _kv_copies(k_hbm, v_hbm, kv_ref, sem, step, slot):
    ns = kv_ref.shape[1] // 2
    seqs = pl.ds(step * ns, ns)
    copies = []
    for h in range(N_XHEADS):
        cols = pl.ds(h * XHEAD_DIM, XHEAD_DIM)
        copies.append(pltpu.make_async_copy(k_hbm.at[0, seqs, :, h, :], kv_ref.at[slot, pl.ds(0, ns), :, cols],
                                            sem.at[slot, 0, h]))
        copies.append(pltpu.make_async_copy(v_hbm.at[0, seqs, :, h, :], kv_ref.at[slot, pl.ds(ns, ns), :, cols],
                                            sem.at[slot, 1, h]))
    return copies


def _cached_scores(q, kv_ref):
    ns = kv_ref.shape[0] // 2
    t = q.shape[0] // ns
    d = q.shape[1]
    rows_per_seq = N_XHEADS * t
    row_head = lax.broadcasted_iota(jnp.int32, (rows_per_seq, d), 0) // t
    col_head = lax.broadcasted_iota(jnp.int32, (rows_per_seq, d), 1) // XHEAD_DIM
    own_head = row_head == col_head
    scores = []
    for s in range(ns):
        qe = jnp.where(own_head, jnp.concatenate([q[s * t:(s + 1) * t, :]] * N_XHEADS, axis=0), 0.0).astype(BF16)
        scores.append(lax.dot_general(qe, kv_ref[s].astype(BF16), (((1,), (1,)), ((), ())),
                                      preferred_element_type=F32))
    return jnp.concatenate(scores, axis=0)


def _cached_outputs(scores, kv_ref, o_ref):
    ns = kv_ref.shape[0] // 2
    t = o_ref.shape[0] // ns
    rows_per_seq = N_XHEADS * t
    p = _softmax(scores).astype(BF16)
    for s in range(ns):
        oa = _dot(p[s * rows_per_seq:(s + 1) * rows_per_seq, :], kv_ref[ns + s].astype(BF16))
        for h in range(N_XHEADS):
            sl = slice(h * XHEAD_DIM, (h + 1) * XHEAD_DIM)
            o_ref[s * t:(s + 1) * t, sl] = oa[h * t:(h + 1) * t, sl]


def _tail_kernel(x_ref, kt_ref, v_ref, x2s_ref, ck_hbm, cv_hbm, gq_ref, wq_hbm, wo_hbm, g2_ref,
                 w2in_hbm, w2out_hbm, gf_ref,
                 y_ref, ys_ref,
                 act_ref, o_ref, os_ref, wq_ref, wo_ref, w2in_ref, w2out_ref, kv_ref, kv_sem, stage_sem):
    step = pl.program_id(0) * pl.num_programs(1) + pl.program_id(1)
    n_steps = pl.num_programs(0) * pl.num_programs(1)
    tm = x_ref.shape[0]
    slot = lax.rem(step, 2)

    @pl.when(step == 0)
    def _():
        _cast_weights_to_vmem([(wq_hbm, wq_ref), (wo_hbm, wo_ref), (w2in_hbm, w2in_ref),
                               (w2out_hbm, w2out_ref)], kv_ref.at[0], stage_sem)
        for c in _kv_copies(ck_hbm, cv_hbm, kv_ref, kv_sem, 0, 0):
            c.start()

    for c in _kv_copies(ck_hbm, cv_hbm, kv_ref, kv_sem, step, slot):
        c.wait()

    @pl.when(step + 1 < n_steps)
    def _():
        for c in _kv_copies(ck_hbm, cv_hbm, kv_ref, kv_sem, step + 1, 1 - slot):
            c.start()

    kv = kv_ref.at[slot]
    x2 = jnp.concatenate([x_ref[...], x2s_ref[...]], axis=0)
    q_all = _queries(x2, gq_ref, wq_ref)
    q = q_all[0:tm].astype(BF16)
    sample_scores = _cached_scores(q_all[tm:], kv)
    heads = [slice(h * XHEAD_DIM, (h + 1) * XHEAD_DIM) for h in range(N_XHEADS)]
    scores = [_dot(q[:, heads[0]], kt_ref[heads[0], :])]
    for h in range(N_XHEADS):
        if h + 1 < N_XHEADS:
            scores.append(_dot(q[:, heads[h + 1]], kt_ref[heads[h + 1], :]))
        p = _softmax(scores[h]).astype(BF16)
        o_ref[0:tm, heads[h]] = _dot(p, v_ref[:, heads[h]]).astype(BF16)
    _cached_outputs(sample_scores, kv, os_ref)
    o_ref[tm:, :] = os_ref[...].astype(BF16)
    x3 = x2 + _dot_rows(o_ref[...], wo_ref[...], EDGE_ROW_PARTS)
    x4 = _ffn_half(x3, g2_ref, w2in_ref, w2out_ref, act_ref)
    y = _rms(x4, gf_ref[...])
    y_ref[...] = y[0:tm]
    ys_ref[...] = y[tm:]


def _resident(arr):
    nd = arr.ndim
    return pl.BlockSpec(arr.shape, lambda *_: (0,) * nd, pipeline_mode=pl.Buffered(1))


def _params(n_grid):
    return pltpu.CompilerParams(dimension_semantics=("arbitrary",) * n_grid,
                                vmem_limit_bytes=VMEM_LIMIT_BYTES)


def kernel(x_prompt, x_sample, mem_prompt, state_pool, state_conv, cache_mem_k, cache_mem_v, g_ffn1, w_ffn1_in, w_ffn1_out, g_mix, w_mix_in, pool_w, pool_scale, conv_w, w_mix_out, g_xq, g_mem, w_xq, w_xk, w_xv, w_xo, g_ffn2, w_ffn2_in, w_ffn2_out, g_final):
    depth = g_ffn1.shape[0]
    assert depth == 1
    bp, seq, d = x_prompt.shape
    bs, tdec, _ = x_sample.shape
    l = 0

    poolw = pool_w[l].astype(BF16)
    wk = w_xk[l].astype(BF16)
    wv = w_xv[l].astype(BF16)
    any_space = pl.BlockSpec(memory_space=pl.ANY)
    stage_shape = (CAST_SLOTS, CAST_ROWS, CAST_COLS)
    g1, gmix, gq, gmem, g2 = g_ffn1[l:l + 1], g_mix[l:l + 1], g_xq[l:l + 1], g_mem[l:l + 1], g_ffn2[l:l + 1]
    gf = g_final.reshape(1, d)
    pscale = pool_scale[l:l + 1]
    convw = conv_w[l]

    mk, mv, kt, vb = pl.pallas_call(
        _memkv_kernel,
        grid=(bp,),
        in_specs=[pl.BlockSpec((None, N_MEM, d), lambda b: (b, 0, 0)),
                  _resident(gmem), _resident(wk), _resident(wv)],
        out_specs=[any_space,
                   any_space,
                   pl.BlockSpec((None, d, N_MEM), lambda b: (b, 0, 0)),
                   pl.BlockSpec((None, N_MEM, d), lambda b: (b, 0, 0))],
        out_shape=[jax.ShapeDtypeStruct((1, bp, N_MEM, N_XHEADS, XHEAD_DIM), F32),
                   jax.ShapeDtypeStruct((1, bp, N_MEM, N_XHEADS, XHEAD_DIM), F32),
                   jax.ShapeDtypeStruct((bp, d, N_MEM), BF16),
                   jax.ShapeDtypeStruct((bp, N_MEM, d), BF16)],
        scratch_shapes=[pltpu.VMEM((2, N_MEM, d), F32),
                        pltpu.VMEM((2, N_MEM, d), F32),
                        pltpu.SemaphoreType.DMA((2, 2, N_XHEADS))],
        compiler_params=_params(1),
        name="mem_kv",
    )(mem_prompt, gmem, wk, wv)

    tm = TM_PROMPT
    nj = seq // tm
    n_steps = bp * nj
    seqs_per_step = bs // n_steps
    assert seqs_per_step * n_steps == bs
    srows = seqs_per_step * tdec
    xs = x_sample.reshape(bs * tdec, d)
    hist_pool = jnp.transpose(state_pool[l], (1, 0, 2)).reshape(POOL_HIST, n_steps, seqs_per_step, D_POOL)
    hist_conv = state_conv[l].reshape(n_steps, seqs_per_step, CONV_HIST, D_CONV)
    tile_rows = pl.BlockSpec((None, tm, d), lambda b, j: (b, j, 0))
    step_rows = pl.BlockSpec((srows, d), lambda b, j: (b * nj + j, 0))
    step_pool = pl.BlockSpec((POOL_HIST, None, seqs_per_step, D_POOL), lambda b, j: (0, b * nj + j, 0, 0))
    step_conv = pl.BlockSpec((None, seqs_per_step, CONV_HIST, D_CONV), lambda b, j: (b * nj + j, 0, 0, 0))
    mix_cast_shapes = [(d, 2 * D_FF), (D_FF, d), (d, D_POOL + 3 * D_CONV), (D_POOL + D_CONV, d)]
    x2p, x2s, npool_p, nconv_p, npool_s, nconv_s = pl.pallas_call(
        _mix_kernel,
        grid=(bp, nj),
        in_specs=[tile_rows, step_rows, step_pool, step_conv,
                  _resident(g1), any_space, any_space, _resident(gmix), any_space, _resident(poolw),
                  _resident(pscale), _resident(convw), any_space],
        out_specs=[tile_rows, step_rows,
                   pl.BlockSpec((POOL_HIST, bp, D_POOL), lambda b, j: (0, 0, 0)),
                   pl.BlockSpec((None, CONV_HIST, D_CONV), lambda b, j: (b, 0, 0)),
                   step_pool, step_conv],
        out_shape=[jax.ShapeDtypeStruct((bp, seq, d), F32),
                   jax.ShapeDtypeStruct((bs * tdec, d), F32),
                   jax.ShapeDtypeStruct((POOL_HIST, bp, D_POOL), F32),
                   jax.ShapeDtypeStruct((bp, CONV_HIST, D_CONV), F32),
                   jax.ShapeDtypeStruct((POOL_HIST, n_steps, seqs_per_step, D_POOL), F32),
                   jax.ShapeDtypeStruct((n_steps, seqs_per_step, CONV_HIST, D_CONV), F32)],
        scratch_shapes=[pltpu.VMEM((tm + srows, D_FF), BF16),
                        pltpu.VMEM((tm + POOL_PAD, D_POOL), F32),
                        pltpu.VMEM((tm + CONV_PAD, D_CONV), F32),
                        pltpu.VMEM((len(POOL_WINDOWS), srows, POOL_GROUP), F32),
                        pltpu.VMEM((len(POOL_WINDOWS), srows, POOL_GROUP), F32),
                        pltpu.VMEM((seqs_per_step, CONV_PAD + tdec, D_CONV), F32),
                        pltpu.VMEM((tm + srows, d), BF16)]
                       + [pltpu.VMEM(s, BF16) for s in mix_cast_shapes]
                       + [pltpu.VMEM(stage_shape, F32),
                          pltpu.SemaphoreType.DMA((CAST_SLOTS,))],
        compiler_params=_params(2),
        name="mix",
    )(x_prompt, xs, hist_pool, hist_conv, g1, w_ffn1_in, w_ffn1_out, gmix, w_mix_in, poolw, pscale, convw,
      w_mix_out)

    assert (2 * seqs_per_step, N_MEM, d) == stage_shape
    tail_cast_shapes = [(d, d), (d, d), (d, 2 * D_FF), (D_FF, d)]
    y_prompt, y_sample = pl.pallas_call(
        _tail_kernel,
        grid=(bp, nj),
        in_specs=[tile_rows,
                  pl.BlockSpec((None, d, N_MEM), lambda b, j: (b, 0, 0)),
                  pl.BlockSpec((None, N_MEM, d), lambda b, j: (b, 0, 0)),
                  step_rows,
                  any_space, any_space,
                  _resident(gq), any_space, any_space, _resident(g2), any_space, any_space, _resident(gf)],
        out_specs=[tile_rows, step_rows],
        out_shape=[jax.ShapeDtypeStruct((bp, seq, d), F32),
                   jax.ShapeDtypeStruct((bs * tdec, d), F32)],
        scratch_shapes=[pltpu.VMEM((tm + srows, D_FF), BF16),
                        pltpu.VMEM((tm + srows, d), BF16),
                        pltpu.VMEM((srows, d), F32)]
                       + [pltpu.VMEM(s, BF16) for s in tail_cast_shapes]
                       + [pltpu.VMEM((2,) + stage_shape, F32),
                          pltpu.SemaphoreType.DMA((2, 2, N_XHEADS)),
                          pltpu.SemaphoreType.DMA((CAST_SLOTS,))],
        compiler_params=_params(2),
        name="tail",
    )(x2p, kt, vb, x2s, cache_mem_k, cache_mem_v, gq, w_xq, w_xo, g2, w_ffn2_in, w_ffn2_out, gf)

    return (y_prompt,
            y_sample.reshape(bs, tdec, d),
            jnp.transpose(npool_p, (1, 0, 2))[None],
            nconv_p[None],
            mk,
            mv,
            jnp.transpose(npool_s.reshape(POOL_HIST, bs, D_POOL), (1, 0, 2))[None],
            nconv_s.reshape(1, bs, CONV_HIST, D_CONV))
```

```python
import jax
import jax.numpy as jnp
from jax import lax
from jax.experimental import pallas as pl
from jax.experimental.pallas import tpu as pltpu

D_MODEL = 1024
D_POOL = 512
D_CONV = 512
POOL_WINDOWS = (2, 4, 8, 16)
POOL_GROUP = 128
POOL_HIST = 15
CONV_HIST = 2
D_FF = 2816
N_MEM = 256
N_XHEADS = 4
XHEAD_DIM = 256
EPS = 1e-6
PAST_LEN = 16384

FF_CHUNK = 512
FF_CHUNKS = tuple((s, min(FF_CHUNK, D_FF - s)) for s in range(0, D_FF, FF_CHUNK))
POOL_PAD = 16
CONV_PAD = 8
VMEM_LIMIT_BYTES = 60 * 1024 * 1024

CAST_SLOTS, CAST_ROWS, CAST_COLS = 8, 256, 1024
EDGE_ROW_PARTS = 2
TM_PROMPT = 512

F32 = jnp.float32
BF16 = jnp.bfloat16


def _rms(x, g):
    y = x * lax.rsqrt(jnp.mean(x * x, axis=-1, keepdims=True) + EPS)
    return y * g


def _dot(a, b):
    return jnp.dot(a, b, preferred_element_type=F32)


def _dot_rows(a, b, parts):
    rows = a.shape[0] // parts
    return jnp.concatenate([_dot(a[i * rows:(i + 1) * rows], b) for i in range(parts)], axis=0)


def _ffn_half(x, g_ref, win_ref, wout_ref, act_ref):
    m = x.shape[0]
    h = _rms(x, g_ref[...]).astype(BF16)
    for i, (s0, n) in enumerate(FF_CHUNKS):
        parts = EDGE_ROW_PARTS if i == 0 else 1
        gate = _dot_rows(h, win_ref[:, s0:s0 + n], parts)
        up = _dot_rows(h, win_ref[:, D_FF + s0:D_FF + s0 + n], parts)
        silu = gate * (1.0 / (1.0 + jnp.exp(-gate)))
        act_ref[0:m, s0:s0 + n] = (silu * up).astype(BF16)
    y = _dot_rows(act_ref[0:m, :], wout_ref[...], EDGE_ROW_PARTS)
    return x + 0.5 * y


def _cast_weights_to_vmem(weights, stage_ref, sem):
    n_slots, cast_rows, cast_cols = stage_ref.shape
    blocks = []
    for w_hbm, dst in weights:
        rows, cols = dst.shape
        for r0 in range(0, rows, cast_rows):
            for c0 in range(0, cols, cast_cols):
                blocks.append((w_hbm, dst, r0, min(cast_rows, rows - r0), c0, min(cast_cols, cols - c0)))

    def copy(i):
        w_hbm, _, r0, nr, c0, nc = blocks[i]
        slot = i % n_slots
        return pltpu.make_async_copy(w_hbm.at[0, pl.ds(r0, nr), pl.ds(c0, nc)],
                                     stage_ref.at[slot, pl.ds(0, nr), pl.ds(0, nc)], sem.at[slot])

    for i in range(min(n_slots, len(blocks))):
        copy(i).start()
    for i, (_, dst, r0, nr, c0, nc) in enumerate(blocks):
        copy(i).wait()
        dst[r0:r0 + nr, c0:c0 + nc] = stage_ref[i % n_slots, 0:nr, 0:nc].astype(BF16)
        if i + n_slots < len(blocks):
            copy(i + n_slots).start()


def _window_sum(x, w):
    s = x
    sh = 1
    while sh < w:
        s = s + pltpu.roll(s, sh, 0)
        sh *= 2
    return s


def _pool_delta_rows(ext, a, pos, w, extract):
    ssum = extract(_window_sum(ext, w))
    cnt = jnp.minimum(pos + 1, w).astype(F32)
    return ssum / cnt - a


def _pool_delta_steps(hist, new, w, pos0):
    n_hist = len(hist)
    sums = dict(enumerate(hist + new))
    sh = 1
    while sh < w:
        sums = {i: sums[i] + sums[i - sh] for i in sums if (i - sh) in sums}
        sh *= 2
    return [sums[n_hist + tt] / float(min(pos0 + tt + 1, w)) - new[tt] for tt in range(len(new))]


def _conv3(ext, convw_ref, extract):
    u0 = extract(pltpu.roll(ext, 2, 0))
    u1 = extract(pltpu.roll(ext, 1, 0))
    u2 = extract(ext)
    return convw_ref[0:1, :] * u0 + convw_ref[1:2, :] * u1 + convw_ref[2:3, :] * u2


def _memkv_copies(kv_scr, mk_hbm, mv_hbm, sem, b):
    copies = []
    for h in range(N_XHEADS):
        cols = pl.ds(h * XHEAD_DIM, XHEAD_DIM)
        copies.append(pltpu.make_async_copy(kv_scr.at[0, :, cols], mk_hbm.at[0, b, :, h, :], sem.at[0, h]))
        copies.append(pltpu.make_async_copy(kv_scr.at[1, :, cols], mv_hbm.at[0, b, :, h, :], sem.at[1, h]))
    return copies


def _mix_kernel(x_ref, xs_ref, hpool_ref, hconv_ref, mem_ref, g1_ref, w1in_hbm, w1out_hbm, gmix_ref, wmixin_hbm,
                poolw_ref, pscale_ref, convw_ref, wmixout_hbm, gmem_ref, wk_hbm, wv_hbm,
                x2_ref, x2s_ref, npool_ref, nconv_ref, npool_s_ref, nconv_s_ref, mk_hbm, mv_hbm, kt_ref, vb_ref,
                act_ref, epool_ref, econv_ref, a_ref, d_ref, econv_s_ref, mixed_ref,
                w1in_ref, w1out_ref, wmixin_ref, wmixout_ref, wk_ref, wv_ref, stage_ref, stage_sem, memkv_sem):
    tm = x_ref.shape[0]
    ns = hpool_ref.shape[1]
    t = xs_ref.shape[0] // ns
    crow = CONV_PAD + t
    b = pl.program_id(0)
    j = pl.program_id(1)

    @pl.when(jnp.logical_and(b == 0, j == 0))
    def _():
        _cast_weights_to_vmem([(wk_hbm, wk_ref), (wv_hbm, wv_ref), (w1in_hbm, w1in_ref), (w1out_hbm, w1out_ref),
                               (wmixin_hbm, wmixin_ref), (wmixout_hbm, wmixout_ref)], stage_ref, stage_sem)

    @pl.when(jnp.logical_and(j == 0, b > 0))
    def _():
        for c in _memkv_copies(stage_ref, mk_hbm, mv_hbm, memkv_sem, b - 1):
            c.wait()

    @pl.when(j == 0)
    def _():
        epool_ref[0:POOL_PAD, :] = jnp.zeros((POOL_PAD, D_POOL), F32)
        econv_ref[0:CONV_PAD, :] = jnp.zeros((CONV_PAD, D_CONV), F32)
        mn = _rms(mem_ref[...], gmem_ref[...]).astype(BF16)
        k = _dot(mn, wk_ref[...])
        v = _dot(mn, wv_ref[...])
        stage_ref[0] = k
        stage_ref[1] = v
        kt_ref[...] = k.T.astype(BF16)
        vb_ref[...] = v.astype(BF16)
        for c in _memkv_copies(stage_ref, mk_hbm, mv_hbm, memkv_sem, b):
            c.start()

    x1 = _ffn_half(jnp.concatenate([x_ref[...], xs_ref[...]], axis=0), g1_ref, w1in_ref, w1out_ref, act_ref)
    z = _dot_rows(_rms(x1, gmix_ref[...]).astype(BF16), wmixin_ref[...], EDGE_ROW_PARTS)
    a = z[:, 0:D_POOL]
    cb = z[:, D_POOL:D_POOL + D_CONV]
    u = z[:, D_POOL + D_CONV:D_POOL + 2 * D_CONV] * z[:, D_POOL + 2 * D_CONV:]

    epool_ref[POOL_PAD:, :] = a[0:tm]
    pos = j * tm + lax.broadcasted_iota(jnp.int32, (tm, 1), 0)
    for gi, w in enumerate(POOL_WINDOWS):
        sl = slice(gi * POOL_GROUP, (gi + 1) * POOL_GROUP)
        d_prompt = _pool_delta_rows(epool_ref[:, sl], a[0:tm, sl], pos, w, lambda s: s[POOL_PAD:, :])
        a_ref[gi] = a[tm:, sl]
        hist = [hpool_ref[i, :, sl] for i in range(POOL_HIST)]
        new = [a_ref[gi, pl.ds(tt, ns, stride=t), :] for tt in range(t)]
        for r, slab in enumerate((hist + new)[-POOL_HIST:]):
            npool_s_ref[r, :, sl] = slab
        for tt, slab in enumerate(_pool_delta_steps(hist, new, w, PAST_LEN)):
            d_ref[gi, pl.ds(tt, ns, stride=t), :] = slab
        d_all = jnp.concatenate([d_prompt, d_ref[gi]], axis=0).astype(BF16)
        mixed_ref[:, sl] = (_dot(d_all, poolw_ref[gi].astype(BF16)) * pscale_ref[:, sl]).astype(BF16)

    econv_ref[CONV_PAD:, :] = u[0:tm]
    yc_prompt = _conv3(econv_ref[...], convw_ref, lambda s: s[CONV_PAD:, :])
    econv_s_ref[:, 0:CONV_PAD - CONV_HIST, :] = jnp.zeros((ns, CONV_PAD - CONV_HIST, D_CONV), F32)
    econv_s_ref[:, CONV_PAD - CONV_HIST:CONV_PAD, :] = hconv_ref[...]
    econv_s_ref[:, CONV_PAD:, :] = u[tm:].reshape(ns, t, D_CONV)
    yc_sample = _conv3(econv_s_ref[...].reshape(ns * crow, D_CONV), convw_ref,
                       lambda s: s.reshape(ns, crow, D_CONV)[:, CONV_PAD:, :].reshape(ns * t, D_CONV))
    mixed_ref[:, D_POOL:] = (cb * jnp.concatenate([yc_prompt, yc_sample], axis=0)).astype(BF16)

    x2 = x1 + _dot_rows(mixed_ref[...], wmixout_ref[...], EDGE_ROW_PARTS)
    x2_ref[...] = x2[0:tm]
    x2s_ref[...] = x2[tm:]
    nconv_s_ref[...] = econv_s_ref[:, crow - CONV_HIST:crow, :]

    epool_ref[0:POOL_PAD, :] = epool_ref[tm:tm + POOL_PAD, :]
    econv_ref[0:CONV_PAD, :] = econv_ref[tm:tm + CONV_PAD, :]

    @pl.when(j == pl.num_programs(1) - 1)
    def _():
        for r in range(POOL_HIST):
            row = POOL_PAD - POOL_HIST + r
            npool_ref[r, pl.ds(b, 1), :] = epool_ref[row:row + 1, :]
        nconv_ref[...] = econv_ref[CONV_PAD - CONV_HIST:CONV_PAD, :]

    @pl.when(jnp.logical_and(b == pl.num_programs(0) - 1, j == pl.num_programs(1) - 1))
    def _():
        for c in _memkv_copies(stage_ref, mk_hbm, mv_hbm, memkv_sem, b):
            c.wait()


def _softmax(s):
    e = jnp.exp(s - jnp.max(s, axis=-1, keepdims=True))
    return e / jnp.sum(e, axis=-1, keepdims=True)


def _queries(x, gq_ref, wq_ref):
    return _dot_rows(_rms(x, gq_ref[...]).astype(BF16), wq_ref[...], EDGE_ROW_PARTS) * (XHEAD_DIM ** -0.5)


def _kv_copies(k_hbm, v_hbm, kv_ref, sem, step, slot):
    ns = kv_ref.shape[1] // 2
    seqs = pl.ds(step * ns, ns)
    copies = []
    for h in range(N_XHEADS):
        cols = pl.ds(h * XHEAD_DIM, XHEAD_DIM)
        copies.append(pltpu.make_async_copy(k_hbm.at[0, seqs, :, h, :], kv_ref.at[slot, pl.ds(0, ns), :, cols],
                                            sem.at[slot, 0, h]))
        copies.append(pltpu.make_async_copy(v_hbm.at[0, seqs, :, h, :], kv_ref.at[slot, pl.ds(ns, ns), :, cols],
                                            sem.at[slot, 1, h]))
    return copies


def _cached_scores(q, kv_ref):
    ns = kv_ref.shape[0] // 2
    t = q.shape[0] // ns
    d = q.shape[1]
    rows_per_seq = N_XHEADS * t
    row_head = lax.broadcasted_iota(jnp.int32, (rows_per_seq, d), 0) // t
    col_head = lax.broadcasted_iota(jnp.int32, (rows_per_seq, d), 1) // XHEAD_DIM
    own_head = row_head == col_head
    scores = []
    for s in range(ns):
        qe = jnp.where(own_head, jnp.concatenate([q[s * t:(s + 1) * t, :]] * N_XHEADS, axis=0), 0.0).astype(BF16)
        scores.append(lax.dot_general(qe, kv_ref[s].astype(BF16), (((1,), (1,)), ((), ())),
                                      preferred_element_type=F32))
    return jnp.concatenate(scores, axis=0)


def _cached_outputs(scores, kv_ref, o_ref):
    ns = kv_ref.shape[0] // 2
    t = o_ref.shape[0] // ns
    rows_per_seq = N_XHEADS * t
    p = _softmax(scores).astype(BF16)
    for s in range(ns):
        oa = _dot(p[s * rows_per_seq:(s + 1) * rows_per_seq, :], kv_ref[ns + s].astype(BF16))
        for h in range(N_XHEADS):
            sl = slice(h * XHEAD_DIM, (h + 1) * XHEAD_DIM)
            o_ref[s * t:(s + 1) * t, sl] = oa[h * t:(h + 1) * t, sl]


def _tail_kernel(x_ref, kt_ref, v_ref, x2s_ref, ck_hbm, cv_hbm, gq_ref, wq_hbm, wo_hbm, g2_ref,
                 w2in_hbm, w2out_hbm, gf_ref,
                 y_ref, ys_ref,
                 act_ref, o_ref, os_ref, wq_ref, wo_ref, w2in_ref, w2out_ref, kv_ref, kv_sem, stage_sem):
    step = pl.program_id(0) * pl.num_programs(1) + pl.program_id(1)
    n_steps = pl.num_programs(0) * pl.num_programs(1)
    tm = x_ref.shape[0]
    slot = lax.rem(step, 2)

    @pl.when(step == 0)
    def _():
        _cast_weights_to_vmem([(wq_hbm, wq_ref), (wo_hbm, wo_ref), (w2in_hbm, w2in_ref),
                               (w2out_hbm, w2out_ref)], kv_ref.at[0], stage_sem)
        for c in _kv_copies(ck_hbm, cv_hbm, kv_ref, kv_sem, 0, 0):
            c.start()

    for c in _kv_copies(ck_hbm, cv_hbm, kv_ref, kv_sem, step, slot):
        c.wait()

    @pl.when(step + 1 < n_steps)
    def _():
        for c in _kv_copies(ck_hbm, cv_hbm, kv_ref, kv_sem, step + 1, 1 - slot):
            c.start()

    kv = kv_ref.at[slot]
    x2 = jnp.concatenate([x_ref[...], x2s_ref[...]], axis=0)
    q_all = _queries(x2, gq_ref, wq_ref)
    q = q_all[0:tm].astype(BF16)
    sample_scores = _cached_scores(q_all[tm:], kv)
    heads = [slice(h * XHEAD_DIM, (h + 1) * XHEAD_DIM) for h in range(N_XHEADS)]
    scores = [_dot(q[:, heads[0]], kt_ref[heads[0], :])]
    for h in range(N_XHEADS):
        if h + 1 < N_XHEADS:
            scores.append(_dot(q[:, heads[h + 1]], kt_ref[heads[h + 1], :]))
        p = _softmax(scores[h]).astype(BF16)
        o_ref[0:tm, heads[h]] = _dot(p, v_ref[:, heads[h]]).astype(BF16)
    _cached_outputs(sample_scores, kv, os_ref)
    o_ref[tm:, :] = os_ref[...].astype(BF16)
    x3 = x2 + _dot_rows(o_ref[...], wo_ref[...], EDGE_ROW_PARTS)
    x4 = _ffn_half(x3, g2_ref, w2in_ref, w2out_ref, act_ref)
    y = _rms(x4, gf_ref[...])
    y_ref[...] = y[0:tm]
    ys_ref[...] = y[tm:]


def _resident(arr):
    nd = arr.ndim
    return pl.BlockSpec(arr.shape, lambda *_: (0,) * nd, pipeline_mode=pl.Buffered(1))


def _params(n_grid):
    return pltpu.CompilerParams(dimension_semantics=("arbitrary",) * n_grid,
                                vmem_limit_bytes=VMEM_LIMIT_BYTES)


def kernel(x_prompt, x_sample, mem_prompt, state_pool, state_conv, cache_mem_k, cache_mem_v, g_ffn1, w_ffn1_in, w_ffn1_out, g_mix, w_mix_in, pool_w, pool_scale, conv_w, w_mix_out, g_xq, g_mem, w_xq, w_xk, w_xv, w_xo, g_ffn2, w_ffn2_in, w_ffn2_out, g_final):
    depth = g_ffn1.shape[0]
    assert depth == 1
    bp, seq, d = x_prompt.shape
    bs, tdec, _ = x_sample.shape
    l = 0

    poolw = pool_w[l]
    any_space = pl.BlockSpec(memory_space=pl.ANY)
    stage_shape = (CAST_SLOTS, CAST_ROWS, CAST_COLS)
    g1, gmix, gq, gmem, g2 = g_ffn1[l:l + 1], g_mix[l:l + 1], g_xq[l:l + 1], g_mem[l:l + 1], g_ffn2[l:l + 1]
    gf = g_final.reshape(1, d)
    pscale = pool_scale[l:l + 1]
    convw = conv_w[l]

    tm = TM_PROMPT
    nj = seq // tm
    n_steps = bp * nj
    seqs_per_step = bs // n_steps
    assert seqs_per_step * n_steps == bs
    srows = seqs_per_step * tdec
    xs = x_sample.reshape(bs * tdec, d)
    hist_pool = jnp.transpose(state_pool[l], (1, 0, 2)).reshape(POOL_HIST, n_steps, seqs_per_step, D_POOL)
    hist_conv = state_conv[l].reshape(n_steps, seqs_per_step, CONV_HIST, D_CONV)
    tile_rows = pl.BlockSpec((None, tm, d), lambda b, j: (b, j, 0))
    step_rows = pl.BlockSpec((srows, d), lambda b, j: (b * nj + j, 0))
    step_pool = pl.BlockSpec((POOL_HIST, None, seqs_per_step, D_POOL), lambda b, j: (0, b * nj + j, 0, 0))
    step_conv = pl.BlockSpec((None, seqs_per_step, CONV_HIST, D_CONV), lambda b, j: (b * nj + j, 0, 0, 0))
    mix_cast_shapes = [(d, 2 * D_FF), (D_FF, d), (d, D_POOL + 3 * D_CONV), (D_POOL + D_CONV, d), (d, d), (d, d)]
    batch_mem = pl.BlockSpec((None, N_MEM, d), lambda b, j: (b, 0, 0), pipeline_mode=pl.Buffered(1))
    x2p, x2s, npool_p, nconv_p, npool_s, nconv_s, mk, mv, kt, vb = pl.pallas_call(
        _mix_kernel,
        grid=(bp, nj),
        in_specs=[tile_rows, step_rows, step_pool, step_conv, batch_mem,
                  _resident(g1), any_space, any_space, _resident(gmix), any_space, _resident(poolw),
                  _resident(pscale), _resident(convw), any_space, _resident(gmem), any_space, any_space],
        out_specs=[tile_rows, step_rows,
                   pl.BlockSpec((POOL_HIST, bp, D_POOL), lambda b, j: (0, 0, 0)),
                   pl.BlockSpec((None, CONV_HIST, D_CONV), lambda b, j: (b, 0, 0)),
                   step_pool, step_conv, any_space, any_space,
                   pl.BlockSpec((None, d, N_MEM), lambda b, j: (b, 0, 0)),
                   pl.BlockSpec((None, N_MEM, d), lambda b, j: (b, 0, 0))],
        out_shape=[jax.ShapeDtypeStruct((bp, seq, d), F32),
                   jax.ShapeDtypeStruct((bs * tdec, d), F32),
                   jax.ShapeDtypeStruct((POOL_HIST, bp, D_POOL), F32),
                   jax.ShapeDtypeStruct((bp, CONV_HIST, D_CONV), F32),
                   jax.ShapeDtypeStruct((POOL_HIST, n_steps, seqs_per_step, D_POOL), F32),
                   jax.ShapeDtypeStruct((n_steps, seqs_per_step, CONV_HIST, D_CONV), F32),
                   jax.ShapeDtypeStruct((1, bp, N_MEM, N_XHEADS, XHEAD_DIM), F32),
                   jax.ShapeDtypeStruct((1, bp, N_MEM, N_XHEADS, XHEAD_DIM), F32),
                   jax.ShapeDtypeStruct((bp, d, N_MEM), BF16),
                   jax.ShapeDtypeStruct((bp, N_MEM, d), BF16)],
        scratch_shapes=[pltpu.VMEM((tm + srows, D_FF), BF16),
                        pltpu.VMEM((tm + POOL_PAD, D_POOL), F32),
                        pltpu.VMEM((tm + CONV_PAD, D_CONV), F32),
                        pltpu.VMEM((len(POOL_WINDOWS), srows, POOL_GROUP), F32),
                        pltpu.VMEM((len(POOL_WINDOWS), srows, POOL_GROUP), F32),
                        pltpu.VMEM((seqs_per_step, CONV_PAD + tdec, D_CONV), F32),
                        pltpu.VMEM((tm + srows, d), BF16)]
                       + [pltpu.VMEM(s, BF16) for s in mix_cast_shapes]
                       + [pltpu.VMEM(stage_shape, F32),
                          pltpu.SemaphoreType.DMA((CAST_SLOTS,)),
                          pltpu.SemaphoreType.DMA((2, N_XHEADS))],
        compiler_params=_params(2),
        name="mix",
    )(x_prompt, xs, hist_pool, hist_conv, mem_prompt, g1, w_ffn1_in, w_ffn1_out, gmix, w_mix_in, poolw, pscale,
      convw, w_mix_out, gmem, w_xk, w_xv)

    assert (2 * seqs_per_step, N_MEM, d) == stage_shape
    tail_cast_shapes = [(d, d), (d, d), (d, 2 * D_FF), (D_FF, d)]
    y_prompt, y_sample = pl.pallas_call(
        _tail_kernel,
        grid=(bp, nj),
        in_specs=[tile_rows,
                  pl.BlockSpec((None, d, N_MEM), lambda b, j: (b, 0, 0)),
                  pl.BlockSpec((None, N_MEM, d), lambda b, j: (b, 0, 0)),
                  step_rows,
                  any_space, any_space,
                  _resident(gq), any_space, any_space, _resident(g2), any_space, any_space, _resident(gf)],
        out_specs=[tile_rows, step_rows],
        out_shape=[jax.ShapeDtypeStruct((bp, seq, d), F32),
                   jax.ShapeDtypeStruct((bs * tdec, d), F32)],
        scratch_shapes=[pltpu.VMEM((tm + srows, D_FF), BF16),
                        pltpu.VMEM((tm + srows, d), BF16),
                        pltpu.VMEM((srows, d), F32)]
                       + [pltpu.VMEM(s, BF16) for s in tail_cast_shapes]
                       + [pltpu.VMEM((2,) + stage_shape, F32),
                          pltpu.SemaphoreType.DMA((2, 2, N_XHEADS)),
                          pltpu.SemaphoreType.DMA((CAST_SLOTS,))],
        compiler_params=_params(2),
        name="tail",
    )(x2p, kt, vb, x2s, cache_mem_k, cache_mem_v, gq, w_xq, w_xo, g2, w_ffn2_in, w_ffn2_out, gf)

    return (y_prompt,
            y_sample.reshape(bs, tdec, d),
            jnp.transpose(npool_p, (1, 0, 2))[None],
            nconv_p[None],
            mk,
            mv,
            jnp.transpose(npool_s.reshape(POOL_HIST, bs, D_POOL), (1, 0, 2))[None],
            nconv_s.reshape(1, bs, CONV_HIST, D_CONV))
```

```python
import jax
import jax.numpy as jnp
from jax import lax
from jax.experimental import pallas as pl
from jax.experimental.pallas import tpu as pltpu

D_MODEL = 1024
D_POOL = 512
D_CONV = 512
POOL_WINDOWS = (2, 4, 8, 16)
POOL_GROUP = 128
POOL_HIST = 15
CONV_HIST = 2
D_FF = 2816
N_MEM = 256
N_XHEADS = 4
XHEAD_DIM = 256
EPS = 1e-6
PAST_LEN = 16384

FF_CHUNK = 512
FF_CHUNKS = tuple((s, min(FF_CHUNK, D_FF - s)) for s in range(0, D_FF, FF_CHUNK))
POOL_PAD = 16
CONV_PAD = 8
VMEM_LIMIT_BYTES = 60 * 1024 * 1024

CAST_SLOTS, CAST_ROWS, CAST_COLS = 8, 256, 1024
EDGE_ROW_PARTS = 2
TM_PROMPT = 512

F32 = jnp.float32
BF16 = jnp.bfloat16


def _rms(x, g):
    y = x * lax.rsqrt(jnp.mean(x * x, axis=-1, keepdims=True) + EPS)
    return y * g


def _dot(a, b):
    return jnp.dot(a, b, preferred_element_type=F32)


def _dot_rows(a, b, parts):
    rows = a.shape[0] // parts
    return jnp.concatenate([_dot(a[i * rows:(i + 1) * rows], b) for i in range(parts)], axis=0)


def _ffn_half(x, g_ref, win_ref, wout_ref, act_ref):
    m = x.shape[0]
    h = _rms(x, g_ref[...]).astype(BF16)
    for i, (s0, n) in enumerate(FF_CHUNKS):
        parts = EDGE_ROW_PARTS if i == 0 else 1
        gate = _dot_rows(h, win_ref[:, s0:s0 + n], parts)
        up = _dot_rows(h, win_ref[:, D_FF + s0:D_FF + s0 + n], parts)
        silu = gate * (1.0 / (1.0 + jnp.exp(-gate)))
        act_ref[0:m, s0:s0 + n] = (silu * up).astype(BF16)
    y = _dot_rows(act_ref[0:m, :], wout_ref[...], EDGE_ROW_PARTS)
    return x + 0.5 * y


def _cast_weights_to_vmem(weights, stage_ref, sem):
    n_slots, cast_rows, cast_cols = stage_ref.shape
    blocks = []
    for w_hbm, dst in weights:
        rows, cols = dst.shape
        for r0 in range(0, rows, cast_rows):
            for c0 in range(0, cols, cast_cols):
                blocks.append((w_hbm, dst, r0, min(cast_rows, rows - r0), c0, min(cast_cols, cols - c0)))

    def copy(i):
        w_hbm, _, r0, nr, c0, nc = blocks[i]
        slot = i % n_slots
        return pltpu.make_async_copy(w_hbm.at[0, pl.ds(r0, nr), pl.ds(c0, nc)],
                                     stage_ref.at[slot, pl.ds(0, nr), pl.ds(0, nc)], sem.at[slot])

    for i in range(min(n_slots, len(blocks))):
        copy(i).start()
    for i, (_, dst, r0, nr, c0, nc) in enumerate(blocks):
        copy(i).wait()
        dst[r0:r0 + nr, c0:c0 + nc] = stage_ref[i % n_slots, 0:nr, 0:nc].astype(BF16)
        if i + n_slots < len(blocks):
            copy(i + n_slots).start()


def _window_sum(x, w):
    s = x
    sh = 1
    while sh < w:
        s = s + pltpu.roll(s, sh, 0)
        sh *= 2
    return s


def _pool_delta_rows(ext, a, pos, w, extract):
    ssum = extract(_window_sum(ext, w))
    cnt = jnp.minimum(pos + 1, w).astype(F32)
    return ssum / cnt - a


def _pool_delta_steps(hist, new, w, pos0):
    n_hist = len(hist)
    sums = dict(enumerate(hist + new))
    sh = 1
    while sh < w:
        sums = {i: sums[i] + sums[i - sh] for i in sums if (i - sh) in sums}
        sh *= 2
    return [sums[n_hist + tt] / float(min(pos0 + tt + 1, w)) - new[tt] for tt in range(len(new))]


def _conv3(ext, convw_ref, extract):
    u0 = extract(pltpu.roll(ext, 2, 0))
    u1 = extract(pltpu.roll(ext, 1, 0))
    u2 = extract(ext)
    return convw_ref[0:1, :] * u0 + convw_ref[1:2, :] * u1 + convw_ref[2:3, :] * u2


def _memkv_copies(kv_scr, mk_hbm, mv_hbm, sem, b):
    copies = []
    for h in range(N_XHEADS):
        cols = pl.ds(h * XHEAD_DIM, XHEAD_DIM)
        copies.append(pltpu.make_async_copy(kv_scr.at[0, :, cols], mk_hbm.at[0, b, :, h, :], sem.at[0, h]))
        copies.append(pltpu.make_async_copy(kv_scr.at[1, :, cols], mv_hbm.at[0, b, :, h, :], sem.at[1, h]))
    return copies


def _mix_kernel(x_ref, xs_ref, hpool_ref, hconv_ref, mem_ref, g1_ref, w1in_hbm, w1out_hbm, gmix_ref, wmixin_hbm,
                poolw_ref, pscale_ref, convw_ref, wmixout_hbm, gmem_ref, wk_hbm, wv_hbm,
                x2_ref, x2s_ref, npool_ref, nconv_ref, npool_s_ref, nconv_s_ref, mk_hbm, mv_hbm, kt_ref, vb_ref,
                act_ref, epool_ref, econv_ref, a_ref, d_ref, econv_s_ref, mixed_ref,
                w1in_ref, w1out_ref, wmixin_ref, wmixout_ref, wk_ref, wv_ref, stage_ref, stage_sem, memkv_sem):
    tm = x_ref.shape[0]
    ns = hpool_ref.shape[1]
    t = xs_ref.shape[0] // ns
    crow = CONV_PAD + t
    b = pl.program_id(0)
    j = pl.program_id(1)

    @pl.when(jnp.logical_and(b == 0, j == 0))
    def _():
        _cast_weights_to_vmem([(wk_hbm, wk_ref), (wv_hbm, wv_ref), (w1in_hbm, w1in_ref), (w1out_hbm, w1out_ref),
                               (wmixin_hbm, wmixin_ref), (wmixout_hbm, wmixout_ref)], stage_ref, stage_sem)

    @pl.when(jnp.logical_and(j == 0, b > 0))
    def _():
        for c in _memkv_copies(stage_ref, mk_hbm, mv_hbm, memkv_sem, b - 1):
            c.wait()

    @pl.when(j == 0)
    def _():
        epool_ref[0:POOL_PAD, :] = jnp.zeros((POOL_PAD, D_POOL), F32)
        econv_ref[0:CONV_PAD, :] = jnp.zeros((CONV_PAD, D_CONV), F32)
        mn = _rms(mem_ref[...], gmem_ref[...]).astype(BF16)
        k = _dot(mn, wk_ref[...])
        v = _dot(mn, wv_ref[...])
        stage_ref[0] = k
        stage_ref[1] = v
        kt_ref[...] = k.T.astype(BF16)
        vb_ref[...] = v.astype(BF16)
        for c in _memkv_copies(stage_ref, mk_hbm, mv_hbm, memkv_sem, b):
            c.start()

    x1 = _ffn_half(jnp.concatenate([x_ref[...], xs_ref[...]], axis=0), g1_ref, w1in_ref, w1out_ref, act_ref)
    z = _dot_rows(_rms(x1, gmix_ref[...]).astype(BF16), wmixin_ref[...], EDGE_ROW_PARTS)
    a = z[:, 0:D_POOL]
    cb = z[:, D_POOL:D_POOL + D_CONV]
    u = z[:, D_POOL + D_CONV:D_POOL + 2 * D_CONV] * z[:, D_POOL + 2 * D_CONV:]

    epool_ref[POOL_PAD:, :] = a[0:tm]
    pos = j * tm + lax.broadcasted_iota(jnp.int32, (tm, 1), 0)
    for gi, w in enumerate(POOL_WINDOWS):
        sl = slice(gi * POOL_GROUP, (gi + 1) * POOL_GROUP)
        d_prompt = _pool_delta_rows(epool_ref[:, sl], a[0:tm, sl], pos, w, lambda s: s[POOL_PAD:, :])
        a_ref[gi] = a[tm:, sl]
        hist = [hpool_ref[i, :, sl] for i in range(POOL_HIST)]
        new = [a_ref[gi, pl.ds(tt, ns, stride=t), :] for tt in range(t)]
        for r, slab in enumerate((hist + new)[-POOL_HIST:]):
            npool_s_ref[r, :, sl] = slab
        for tt, slab in enumerate(_pool_delta_steps(hist, new, w, PAST_LEN)):
            d_ref[gi, pl.ds(tt, ns, stride=t), :] = slab
        d_all = jnp.concatenate([d_prompt, d_ref[gi]], axis=0).astype(BF16)
        mixed_ref[:, sl] = (_dot(d_all, poolw_ref[gi].astype(BF16)) * pscale_ref[:, sl]).astype(BF16)

    econv_ref[CONV_PAD:, :] = u[0:tm]
    yc_prompt = _conv3(econv_ref[...], convw_ref, lambda s: s[CONV_PAD:, :])
    econv_s_ref[:, 0:CONV_PAD - CONV_HIST, :] = jnp.zeros((ns, CONV_PAD - CONV_HIST, D_CONV), F32)
    econv_s_ref[:, CONV_PAD - CONV_HIST:CONV_PAD, :] = hconv_ref[...]
    econv_s_ref[:, CONV_PAD:, :] = u[tm:].reshape(ns, t, D_CONV)
    yc_sample = _conv3(econv_s_ref[...].reshape(ns * crow, D_CONV), convw_ref,
                       lambda s: s.reshape(ns, crow, D_CONV)[:, CONV_PAD:, :].reshape(ns * t, D_CONV))
    mixed_ref[:, D_POOL:] = (cb * jnp.concatenate([yc_prompt, yc_sample], axis=0)).astype(BF16)

    x2 = x1 + _dot_rows(mixed_ref[...], wmixout_ref[...], EDGE_ROW_PARTS)
    x2_ref[...] = x2[0:tm]
    x2s_ref[...] = x2[tm:]
    nconv_s_ref[...] = econv_s_ref[:, crow - CONV_HIST:crow, :]

    epool_ref[0:POOL_PAD, :] = epool_ref[tm:tm + POOL_PAD, :]
    econv_ref[0:CONV_PAD, :] = econv_ref[tm:tm + CONV_PAD, :]

    @pl.when(j == pl.num_programs(1) - 1)
    def _():
        for r in range(POOL_HIST):
            row = POOL_PAD - POOL_HIST + r
            npool_ref[r, pl.ds(b, 1), :] = epool_ref[row:row + 1, :]
        nconv_ref[...] = econv_ref[CONV_PAD - CONV_HIST:CONV_PAD, :]

    @pl.when(jnp.logical_and(b == pl.num_programs(0) - 1, j == pl.num_programs(1) - 1))
    def _():
        for c in _memkv_copies(stage_ref, mk_hbm, mv_hbm, memkv_sem, b):
            c.wait()


def _softmax(s):
    e = jnp.exp(s - jnp.max(s, axis=-1, keepdims=True))
    return e / jnp.sum(e, axis=-1, keepdims=True)


def _queries(x, gq_ref, wq_ref):
    return _dot_rows(_rms(x, gq_ref[...]).astype(BF16), wq_ref[...], EDGE_ROW_PARTS) * (XHEAD_DIM ** -0.5)


def _kv_copies(k_hbm, v_hbm, kv_ref, sem, step, slot):
    ns = kv_ref.shape[1] // 2
    seqs = pl.ds(step * ns, ns)
    copies = []
    for h in range(N_XHEADS):
        cols = pl.ds(h * XHEAD_DIM, XHEAD_DIM)
        copies.append(pltpu.make_async_copy(k_hbm.at[0, seqs, :, h, :], kv_ref.at[slot, pl.ds(0, ns), :, cols],
                                            sem.at[slot, 0, h]))
        copies.append(pltpu.make_async_copy(v_hbm.at[0, seqs, :, h, :], kv_ref.at[slot, pl.ds(ns, ns), :, cols],
                                            sem.at[slot, 1, h]))
    return copies


def _cached_scores(q, kv_ref):
    ns = kv_ref.shape[0] // 2
    t = q.shape[0] // ns
    d = q.shape[1]
    rows_per_seq = N_XHEADS * t
    row_head = lax.broadcasted_iota(jnp.int32, (rows_per_seq, d), 0) // t
    col_head = lax.broadcasted_iota(jnp.int32, (rows_per_seq, d), 1) // XHEAD_DIM
    own_head = row_head == col_head
    scores = []
    for s in range(ns):
        qe = jnp.where(own_head, jnp.concatenate([q[s * t:(s + 1) * t, :]] * N_XHEADS, axis=0), 0.0).astype(BF16)
        scores.append(lax.dot_general(qe, kv_ref[s].astype(BF16), (((1,), (1,)), ((), ())),
                                      preferred_element_type=F32))
    return jnp.concatenate(scores, axis=0)


def _cached_outputs(scores, kv_ref, o_ref):
    ns = kv_ref.shape[0] // 2
    t = o_ref.shape[0] // ns
    rows_per_seq = N_XHEADS * t
    p = _softmax(scores).astype(BF16)
    for s in range(ns):
        oa = _dot(p[s * rows_per_seq:(s + 1) * rows_per_seq, :], kv_ref[ns + s].astype(BF16))
        for h in range(N_XHEADS):
            sl = slice(h * XHEAD_DIM, (h + 1) * XHEAD_DIM)
            o_ref[s * t:(s + 1) * t, sl] = oa[h * t:(h + 1) * t, sl]


def _tail_kernel(x_ref, kt_ref, v_ref, x2s_ref, ck_hbm, cv_hbm, gq_ref, wq_hbm, wo_hbm, g2_ref,
                 w2in_hbm, w2out_hbm, gf_ref,
                 y_ref, ys_ref,
                 act_ref, o_ref, os_ref, wq_ref, wo_ref, w2in_ref, w2out_ref, kv_ref, kv_sem, stage_sem):
    step = pl.program_id(0) * pl.num_programs(1) + pl.program_id(1)
    n_steps = pl.num_programs(0) * pl.num_programs(1)
    tm = x_ref.shape[0]
    slot = lax.rem(step, 2)

    @pl.when(step == 0)
    def _():
        _cast_weights_to_vmem([(wq_hbm, wq_ref), (wo_hbm, wo_ref), (w2in_hbm, w2in_ref),
                               (w2out_hbm, w2out_ref)], kv_ref.at[0], stage_sem)
        for c in _kv_copies(ck_hbm, cv_hbm, kv_ref, kv_sem, 0, 0):
            c.start()

    for c in _kv_copies(ck_hbm, cv_hbm, kv_ref, kv_sem, step, slot):
        c.wait()

    @pl.when(step + 1 < n_steps)
    def _():
        for c in _kv_copies(ck_hbm, cv_hbm, kv_ref, kv_sem, step + 1, 1 - slot):
            c.start()

    kv = kv_ref.at[slot]
    x2 = jnp.concatenate([x_ref[...], x2s_ref[...]], axis=0)
    q_all = _queries(x2, gq_ref, wq_ref)
    q = q_all[0:tm].astype(BF16)
    heads = [slice(h * XHEAD_DIM, (h + 1) * XHEAD_DIM) for h in range(N_XHEADS)]
    scores = [_dot(q[:, heads[h]], kt_ref[heads[h], :]) for h in range(N_XHEADS)]
    sample_scores = _cached_scores(q_all[tm:], kv)
    for h in range(N_XHEADS):
        p = _softmax(scores[h]).astype(BF16)
        o_ref[0:tm, heads[h]] = _dot(p, v_ref[:, heads[h]]).astype(BF16)
        if h == 1:
            _cached_outputs(sample_scores, kv, os_ref)
    o_ref[tm:, :] = os_ref[...].astype(BF16)
    x3 = x2 + _dot_rows(o_ref[...], wo_ref[...], EDGE_ROW_PARTS)
    x4 = _ffn_half(x3, g2_ref, w2in_ref, w2out_ref, act_ref)
    y = _rms(x4, gf_ref[...])
    y_ref[...] = y[0:tm]
    ys_ref[...] = y[tm:]


def _resident(arr):
    nd = arr.ndim
    return pl.BlockSpec(arr.shape, lambda *_: (0,) * nd, pipeline_mode=pl.Buffered(1))


def _params(n_grid):
    return pltpu.CompilerParams(dimension_semantics=("arbitrary",) * n_grid,
                                vmem_limit_bytes=VMEM_LIMIT_BYTES)


def kernel(x_prompt, x_sample, mem_prompt, state_pool, state_conv, cache_mem_k, cache_mem_v, g_ffn1, w_ffn1_in, w_ffn1_out, g_mix, w_mix_in, pool_w, pool_scale, conv_w, w_mix_out, g_xq, g_mem, w_xq, w_xk, w_xv, w_xo, g_ffn2, w_ffn2_in, w_ffn2_out, g_final):
    depth = g_ffn1.shape[0]
    assert depth == 1
    bp, seq, d = x_prompt.shape
    bs, tdec, _ = x_sample.shape
    l = 0

    poolw = pool_w[l]
    any_space = pl.BlockSpec(memory_space=pl.ANY)
    stage_shape = (CAST_SLOTS, CAST_ROWS, CAST_COLS)
    g1, gmix, gq, gmem, g2 = g_ffn1[l:l + 1], g_mix[l:l + 1], g_xq[l:l + 1], g_mem[l:l + 1], g_ffn2[l:l + 1]
    gf = g_final.reshape(1, d)
    pscale = pool_scale[l:l + 1]
    convw = conv_w[l]

    tm = TM_PROMPT
    nj = seq // tm
    n_steps = bp * nj
    seqs_per_step = bs // n_steps
    assert seqs_per_step * n_steps == bs
    srows = seqs_per_step * tdec
    xs = x_sample.reshape(bs * tdec, d)
    hist_pool = jnp.transpose(state_pool[l], (1, 0, 2)).reshape(POOL_HIST, n_steps, seqs_per_step, D_POOL)
    hist_conv = state_conv[l].reshape(n_steps, seqs_per_step, CONV_HIST, D_CONV)
    tile_rows = pl.BlockSpec((None, tm, d), lambda b, j: (b, j, 0))
    step_rows = pl.BlockSpec((srows, d), lambda b, j: (b * nj + j, 0))
    step_pool = pl.BlockSpec((POOL_HIST, None, seqs_per_step, D_POOL), lambda b, j: (0, b * nj + j, 0, 0))
    step_conv = pl.BlockSpec((None, seqs_per_step, CONV_HIST, D_CONV), lambda b, j: (b * nj + j, 0, 0, 0))
    mix_cast_shapes = [(d, 2 * D_FF), (D_FF, d), (d, D_POOL + 3 * D_CONV), (D_POOL + D_CONV, d), (d, d), (d, d)]
    batch_mem = pl.BlockSpec((None, N_MEM, d), lambda b, j: (b, 0, 0))
    x2p, x2s, npool_p, nconv_p, npool_s, nconv_s, mk, mv, kt, vb = pl.pallas_call(
        _mix_kernel,
        grid=(bp, nj),
        in_specs=[tile_rows, step_rows, step_pool, step_conv, batch_mem,
                  _resident(g1), any_space, any_space, _resident(gmix), any_space, _resident(poolw),
                  _resident(pscale), _resident(convw), any_space, _resident(gmem), any_space, any_space],
        out_specs=[tile_rows, step_rows,
                   pl.BlockSpec((POOL_HIST, bp, D_POOL), lambda b, j: (0, 0, 0)),
                   pl.BlockSpec((None, CONV_HIST, D_CONV), lambda b, j: (b, 0, 0)),
                   step_pool, step_conv, any_space, any_space,
                   pl.BlockSpec((None, d, N_MEM), lambda b, j: (b, 0, 0)),
                   pl.BlockSpec((None, N_MEM, d), lambda b, j: (b, 0, 0))],
        out_shape=[jax.ShapeDtypeStruct((bp, seq, d), F32),
                   jax.ShapeDtypeStruct((bs * tdec, d), F32),
                   jax.ShapeDtypeStruct((POOL_HIST, bp, D_POOL), F32),
                   jax.ShapeDtypeStruct((bp, CONV_HIST, D_CONV), F32),
                   jax.ShapeDtypeStruct((POOL_HIST, n_steps, seqs_per_step, D_POOL), F32),
                   jax.ShapeDtypeStruct((n_steps, seqs_per_step, CONV_HIST, D_CONV), F32),
                   jax.ShapeDtypeStruct((1, bp, N_MEM, N_XHEADS, XHEAD_DIM), F32),
                   jax.ShapeDtypeStruct((1, bp, N_MEM, N_XHEADS, XHEAD_DIM), F32),
                   jax.ShapeDtypeStruct((bp, d, N_MEM), BF16),
                   jax.ShapeDtypeStruct((bp, N_MEM, d), BF16)],
        scratch_shapes=[pltpu.VMEM((tm + srows, D_FF), BF16),
                        pltpu.VMEM((tm + POOL_PAD, D_POOL), F32),
                        pltpu.VMEM((tm + CONV_PAD, D_CONV), F32),
                        pltpu.VMEM((len(POOL_WINDOWS), srows, POOL_GROUP), F32),
                        pltpu.VMEM((len(POOL_WINDOWS), srows, POOL_GROUP), F32),
                        pltpu.VMEM((seqs_per_step, CONV_PAD + tdec, D_CONV), F32),
                        pltpu.VMEM((tm + srows, d), BF16)]
                       + [pltpu.VMEM(s, BF16) for s in mix_cast_shapes]
                       + [pltpu.VMEM(stage_shape, F32),
                          pltpu.SemaphoreType.DMA((CAST_SLOTS,)),
                          pltpu.SemaphoreType.DMA((2, N_XHEADS))],
        compiler_params=_params(2),
        name="mix",
    )(x_prompt, xs, hist_pool, hist_conv, mem_prompt, g1, w_ffn1_in, w_ffn1_out, gmix, w_mix_in, poolw, pscale,
      convw, w_mix_out, gmem, w_xk, w_xv)

    assert (2 * seqs_per_step, N_MEM, d) == stage_shape
    tail_cast_shapes = [(d, d), (d, d), (d, 2 * D_FF), (D_FF, d)]
    y_prompt, y_sample = pl.pallas_call(
        _tail_kernel,
        grid=(bp, nj),
        in_specs=[tile_rows,
                  pl.BlockSpec((None, d, N_MEM), lambda b, j: (b, 0, 0)),
                  pl.BlockSpec((None, N_MEM, d), lambda b, j: (b, 0, 0)),
                  step_rows,
                  any_space, any_space,
                  _resident(gq), any_space, any_space, _resident(g2), any_space, any_space, _resident(gf)],
        out_specs=[tile_rows, step_rows],
        out_shape=[jax.ShapeDtypeStruct((bp, seq, d), F32),
                   jax.ShapeDtypeStruct((bs * tdec, d), F32)],
        scratch_shapes=[pltpu.VMEM((tm + srows, D_FF), BF16),
                        pltpu.VMEM((tm + srows, d), BF16),
                        pltpu.VMEM((srows, d), F32)]
                       + [pltpu.VMEM(s, BF16) for s in tail_cast_shapes]
                       + [pltpu.VMEM((2,) + stage_shape, F32),
                          pltpu.SemaphoreType.DMA((2, 2, N_XHEADS)),
                          pltpu.SemaphoreType.DMA((CAST_SLOTS,))],
        compiler_params=_params(2),
        name="tail",
    )(x2p, kt, vb, x2s, cache_mem_k, cache_mem_v, gq, w_xq, w_xo, g2, w_ffn2_in, w_ffn2_out, gf)

    return (y_prompt,
            y_sample.reshape(bs, tdec, d),
            jnp.transpose(npool_p, (1, 0, 2))[None],
            nconv_p[None],
            mk,
            mv,
            jnp.transpose(npool_s.reshape(POOL_HIST, bs, D_POOL), (1, 0, 2))[None],
            nconv_s.reshape(1, bs, CONV_HIST, D_CONV))
```

```python
import jax
import jax.numpy as jnp
from jax import lax
from jax.experimental import pallas as pl
from jax.experimental.pallas import tpu as pltpu

D_MODEL = 1024
D_POOL = 512
D_CONV = 512
POOL_WINDOWS = (2, 4, 8, 16)
POOL_GROUP = 128
POOL_HIST = 15
CONV_HIST = 2
D_FF = 2816
N_MEM = 256
N_XHEADS = 4
XHEAD_DIM = 256
EPS = 1e-6
PAST_LEN = 16384

FF_CHUNK = 512
FF_CHUNKS = tuple((s, min(FF_CHUNK, D_FF - s)) for s in range(0, D_FF, FF_CHUNK))
POOL_PAD = 16
CONV_PAD = 8
VMEM_LIMIT_BYTES = 60 * 1024 * 1024

CAST_SLOTS, CAST_ROWS, CAST_COLS = 8, 256, 1024
EDGE_ROW_PARTS = 2
TM_PROMPT = 512

F32 = jnp.float32
BF16 = jnp.bfloat16


def _rms(x, g):
    y = x * lax.rsqrt(jnp.mean(x * x, axis=-1, keepdims=True) + EPS)
    return y * g


def _dot(a, b):
    return jnp.dot(a, b, preferred_element_type=F32)


def _dot_rows(a, b, parts):
    rows = a.shape[0] // parts
    return jnp.concatenate([_dot(a[i * rows:(i + 1) * rows], b) for i in range(parts)], axis=0)


def _ffn_half(x, g_ref, win_ref, wout_ref, act_ref):
    m = x.shape[0]
    h = _rms(x, g_ref[...]).astype(BF16)
    for i, (s0, n) in enumerate(FF_CHUNKS):
        parts = EDGE_ROW_PARTS if i == 0 else 1
        gate = _dot_rows(h, win_ref[:, s0:s0 + n], parts)
        up = _dot_rows(h, win_ref[:, D_FF + s0:D_FF + s0 + n], parts)
        silu = gate * (1.0 / (1.0 + jnp.exp(-gate)))
        act_ref[0:m, s0:s0 + n] = (silu * up).astype(BF16)
    y = _dot_rows(act_ref[0:m, :], wout_ref[...], EDGE_ROW_PARTS)
    return x + 0.5 * y


def _cast_weights_to_vmem(weights, stage_ref, sem):
    n_slots, cast_rows, cast_cols = stage_ref.shape
    blocks = []
    for w_hbm, dst in weights:
        rows, cols = dst.shape
        for r0 in range(0, rows, cast_rows):
            for c0 in range(0, cols, cast_cols):
                blocks.append((w_hbm, dst, r0, min(cast_rows, rows - r0), c0, min(cast_cols, cols - c0)))

    def copy(i):
        w_hbm, _, r0, nr, c0, nc = blocks[i]
        slot = i % n_slots
        return pltpu.make_async_copy(w_hbm.at[0, pl.ds(r0, nr), pl.ds(c0, nc)],
                                     stage_ref.at[slot, pl.ds(0, nr), pl.ds(0, nc)], sem.at[slot])

    for i in range(min(n_slots, len(blocks))):
        copy(i).start()
    for i, (_, dst, r0, nr, c0, nc) in enumerate(blocks):
        copy(i).wait()
        dst[r0:r0 + nr, c0:c0 + nc] = stage_ref[i % n_slots, 0:nr, 0:nc].astype(BF16)
        if i + n_slots < len(blocks):
            copy(i + n_slots).start()


def _window_sum(x, w):
    s = x
    sh = 1
    while sh < w:
        s = s + pltpu.roll(s, sh, 0)
        sh *= 2
    return s


def _pool_delta_rows(ext, a, pos, w, extract):
    ssum = extract(_window_sum(ext, w))
    cnt = jnp.minimum(pos + 1, w).astype(F32)
    return ssum / cnt - a


def _pool_delta_steps(hist, new, w, pos0):
    n_hist = len(hist)
    sums = dict(enumerate(hist + new))
    sh = 1
    while sh < w:
        sums = {i: sums[i] + sums[i - sh] for i in sums if (i - sh) in sums}
        sh *= 2
    return [sums[n_hist + tt] / float(min(pos0 + tt + 1, w)) - new[tt] for tt in range(len(new))]


def _conv3(ext, convw_ref, extract):
    u0 = extract(pltpu.roll(ext, 2, 0))
    u1 = extract(pltpu.roll(ext, 1, 0))
    u2 = extract(ext)
    return convw_ref[0:1, :] * u0 + convw_ref[1:2, :] * u1 + convw_ref[2:3, :] * u2


def _memkv_copies(kv_scr, mk_hbm, mv_hbm, sem, b):
    copies = []
    for h in range(N_XHEADS):
        cols = pl.ds(h * XHEAD_DIM, XHEAD_DIM)
        copies.append(pltpu.make_async_copy(kv_scr.at[0, :, cols], mk_hbm.at[0, b, :, h, :], sem.at[0, h]))
        copies.append(pltpu.make_async_copy(kv_scr.at[1, :, cols], mv_hbm.at[0, b, :, h, :], sem.at[1, h]))
    return copies


def _mix_kernel(x_ref, xs_ref, hpool_hbm, hconv_ref, mem_ref, g1_ref, w1in_hbm, w1out_hbm, gmix_ref, wmixin_hbm,
                poolw_ref, pscale_ref, convw_ref, wmixout_hbm, gmem_ref, wk_hbm, wv_hbm,
                x2_ref, x2s_ref, npool_ref, nconv_ref, npool_s_hbm, nconv_s_ref, mk_hbm, mv_hbm, kt_ref, vb_ref,
                act_ref, epool_ref, econv_ref, a_ref, d_ref, econv_s_ref, mixed_ref,
                w1in_ref, w1out_ref, wmixin_ref, wmixout_ref, wk_ref, wv_ref, stage_ref, stage_sem, memkv_sem,
                hpool2_ref, npool2_s_ref, pool_sem):
    tm = x_ref.shape[0]
    ns = hpool2_ref.shape[2]
    t = xs_ref.shape[0] // ns
    crow = CONV_PAD + t
    b = pl.program_id(0)
    j = pl.program_id(1)
    step = b * pl.num_programs(1) + j
    n_steps = pl.num_programs(0) * pl.num_programs(1)
    slot = lax.rem(step, 2)

    def hist_in(st, sl):
        return pltpu.make_async_copy(hpool_hbm.at[:, pl.ds(st * ns, ns), :], hpool2_ref.at[sl], pool_sem.at[0, sl])

    def state_out(st, sl):
        return pltpu.make_async_copy(npool2_s_ref.at[sl], npool_s_hbm.at[:, pl.ds(st * ns, ns), :],
                                     pool_sem.at[1, sl])

    @pl.when(step == 0)
    def _():
        hist_in(0, 0).start()

    @pl.when(step >= 2)
    def _():
        state_out(step - 2, slot).wait()

    hist_in(step, slot).wait()

    @pl.when(step + 1 < n_steps)
    def _():
        hist_in(step + 1, 1 - slot).start()

    hpool_ref = hpool2_ref.at[slot]
    npool_s_ref = npool2_s_ref.at[slot]

    @pl.when(jnp.logical_and(b == 0, j == 0))
    def _():
        _cast_weights_to_vmem([(wk_hbm, wk_ref), (wv_hbm, wv_ref), (w1in_hbm, w1in_ref), (w1out_hbm, w1out_ref),
                               (wmixin_hbm, wmixin_ref), (wmixout_hbm, wmixout_ref)], stage_ref, stage_sem)

    @pl.when(jnp.logical_and(j == 0, b > 0))
    def _():
        for c in _memkv_copies(stage_ref, mk_hbm, mv_hbm, memkv_sem, b - 1):
            c.wait()

    @pl.when(j == 0)
    def _():
        epool_ref[0:POOL_PAD, :] = jnp.zeros((POOL_PAD, D_POOL), F32)
        econv_ref[0:CONV_PAD, :] = jnp.zeros((CONV_PAD, D_CONV), F32)
        mn = _rms(mem_ref[...], gmem_ref[...]).astype(BF16)
        k = _dot(mn, wk_ref[...])
        v = _dot(mn, wv_ref[...])
        stage_ref[0] = k
        stage_ref[1] = v
        kt_ref[...] = k.T.astype(BF16)
        vb_ref[...] = v.astype(BF16)
        for c in _memkv_copies(stage_ref, mk_hbm, mv_hbm, memkv_sem, b):
            c.start()

    x1 = _ffn_half(jnp.concatenate([x_ref[...], xs_ref[...]], axis=0), g1_ref, w1in_ref, w1out_ref, act_ref)
    z = _dot_rows(_rms(x1, gmix_ref[...]).astype(BF16), wmixin_ref[...], EDGE_ROW_PARTS)
    a = z[:, 0:D_POOL]
    cb = z[:, D_POOL:D_POOL + D_CONV]
    u = z[:, D_POOL + D_CONV:D_POOL + 2 * D_CONV] * z[:, D_POOL + 2 * D_CONV:]

    epool_ref[POOL_PAD:, :] = a[0:tm]
    pos = j * tm + lax.broadcasted_iota(jnp.int32, (tm, 1), 0)
    for gi, w in enumerate(POOL_WINDOWS):
        sl = slice(gi * POOL_GROUP, (gi + 1) * POOL_GROUP)
        d_prompt = _pool_delta_rows(epool_ref[:, sl], a[0:tm, sl], pos, w, lambda s: s[POOL_PAD:, :])
        a_ref[gi] = a[tm:, sl]
        hist = [hpool_ref[i, :, sl] for i in range(POOL_HIST)]
        new = [a_ref[gi, pl.ds(tt, ns, stride=t), :] for tt in range(t)]
        for r, slab in enumerate((hist + new)[-POOL_HIST:]):
            npool_s_ref[r, :, sl] = slab
        for tt, slab in enumerate(_pool_delta_steps(hist, new, w, PAST_LEN)):
            d_ref[gi, pl.ds(tt, ns, stride=t), :] = slab
        d_all = jnp.concatenate([d_prompt, d_ref[gi]], axis=0).astype(BF16)
        mixed_ref[:, sl] = (_dot(d_all, poolw_ref[gi].astype(BF16)) * pscale_ref[:, sl]).astype(BF16)

    econv_ref[CONV_PAD:, :] = u[0:tm]
    yc_prompt = _conv3(econv_ref[...], convw_ref, lambda s: s[CONV_PAD:, :])
    econv_s_ref[:, 0:CONV_PAD - CONV_HIST, :] = jnp.zeros((ns, CONV_PAD - CONV_HIST, D_CONV), F32)
    econv_s_ref[:, CONV_PAD - CONV_HIST:CONV_PAD, :] = hconv_ref[...]
    econv_s_ref[:, CONV_PAD:, :] = u[tm:].reshape(ns, t, D_CONV)
    yc_sample = _conv3(econv_s_ref[...].reshape(ns * crow, D_CONV), convw_ref,
                       lambda s: s.reshape(ns, crow, D_CONV)[:, CONV_PAD:, :].reshape(ns * t, D_CONV))
    mixed_ref[:, D_POOL:] = (cb * jnp.concatenate([yc_prompt, yc_sample], axis=0)).astype(BF16)

    x2 = x1 + _dot_rows(mixed_ref[...], wmixout_ref[...], EDGE_ROW_PARTS)
    x2_ref[...] = x2[0:tm]
    x2s_ref[...] = x2[tm:]
    nconv_s_ref[...] = econv_s_ref[:, crow - CONV_HIST:crow, :]
    state_out(step, slot).start()

    epool_ref[0:POOL_PAD, :] = epool_ref[tm:tm + POOL_PAD, :]
    econv_ref[0:CONV_PAD, :] = econv_ref[tm:tm + CONV_PAD, :]

    @pl.when(j == pl.num_programs(1) - 1)
    def _():
        for r in range(POOL_HIST):
            row = POOL_PAD - POOL_HIST + r
            npool_ref[r, pl.ds(b, 1), :] = epool_ref[row:row + 1, :]
        nconv_ref[...] = econv_ref[CONV_PAD - CONV_HIST:CONV_PAD, :]

    @pl.when(jnp.logical_and(b == pl.num_programs(0) - 1, j == pl.num_programs(1) - 1))
    def _():
        for c in _memkv_copies(stage_ref, mk_hbm, mv_hbm, memkv_sem, b):
            c.wait()
        state_out(step - 1, 1 - slot).wait()
        state_out(step, slot).wait()


def _softmax(s):
    e = jnp.exp(s - jnp.max(s, axis=-1, keepdims=True))
    return e / jnp.sum(e, axis=-1, keepdims=True)


def _queries(x, gq_ref, wq_ref):
    return _dot_rows(_rms(x, gq_ref[...]).astype(BF16), wq_ref[...], EDGE_ROW_PARTS) * (XHEAD_DIM ** -0.5)


def _kv_copies(k_hbm, v_hbm, kv_ref, sem, step, slot):
    ns = kv_ref.shape[1] // 2
    seqs = pl.ds(step * ns, ns)
    copies = []
    for h in range(N_XHEADS):
        cols = pl.ds(h * XHEAD_DIM, XHEAD_DIM)
        copies.append(pltpu.make_async_copy(k_hbm.at[0, seqs, :, h, :], kv_ref.at[slot, pl.ds(0, ns), :, cols],
                                            sem.at[slot, 0, h]))
        copies.append(pltpu.make_async_copy(v_hbm.at[0, seqs, :, h, :], kv_ref.at[slot, pl.ds(ns, ns), :, cols],
                                            sem.at[slot, 1, h]))
    return copies


def _cached_scores(q, kv_ref):
    ns = kv_ref.shape[0] // 2
    t = q.shape[0] // ns
    d = q.shape[1]
    rows_per_seq = N_XHEADS * t
    row_head = lax.broadcasted_iota(jnp.int32, (rows_per_seq, d), 0) // t
    col_head = lax.broadcasted_iota(jnp.int32, (rows_per_seq, d), 1) // XHEAD_DIM
    own_head = row_head == col_head
    scores = []
    for s in range(ns):
        qe = jnp.where(own_head, jnp.concatenate([q[s * t:(s + 1) * t, :]] * N_XHEADS, axis=0), 0.0).astype(BF16)
        scores.append(lax.dot_general(qe, kv_ref[s].astype(BF16), (((1,), (1,)), ((), ())),
                                      preferred_element_type=F32))
    return jnp.concatenate(scores, axis=0)


def _cached_outputs(scores, kv_ref, o_ref):
    ns = kv_ref.shape[0] // 2
    t = o_ref.shape[0] // ns
    rows_per_seq = N_XHEADS * t
    p = _softmax(scores).astype(BF16)
    for s in range(ns):
        oa = _dot(p[s * rows_per_seq:(s + 1) * rows_per_seq, :], kv_ref[ns + s].astype(BF16))
        for h in range(N_XHEADS):
            sl = slice(h * XHEAD_DIM, (h + 1) * XHEAD_DIM)
            o_ref[s * t:(s + 1) * t, sl] = oa[h * t:(h + 1) * t, sl]


def _tail_kernel(x_ref, kt_ref, v_ref, x2s_ref, ck_hbm, cv_hbm, gq_ref, wq_hbm, wo_hbm, g2_ref,
                 w2in_hbm, w2out_hbm, gf_ref,
                 y_ref, ys_ref,
                 act_ref, o_ref, os_ref, wq_ref, wo_ref, w2in_ref, w2out_ref, kv_ref, kv_sem, stage_sem):
    step = pl.program_id(0) * pl.num_programs(1) + pl.program_id(1)
    n_steps = pl.num_programs(0) * pl.num_programs(1)
    tm = x_ref.shape[0]
    slot = lax.rem(step, 2)

    @pl.when(step == 0)
    def _():
        _cast_weights_to_vmem([(wq_hbm, wq_ref), (wo_hbm, wo_ref), (w2in_hbm, w2in_ref),
                               (w2out_hbm, w2out_ref)], kv_ref.at[0], stage_sem)
        for c in _kv_copies(ck_hbm, cv_hbm, kv_ref, kv_sem, 0, 0):
            c.start()

    for c in _kv_copies(ck_hbm, cv_hbm, kv_ref, kv_sem, step, slot):
        c.wait()

    @pl.when(step + 1 < n_steps)
    def _():
        for c in _kv_copies(ck_hbm, cv_hbm, kv_ref, kv_sem, step + 1, 1 - slot):
            c.start()

    kv = kv_ref.at[slot]
    x2 = jnp.concatenate([x_ref[...], x2s_ref[...]], axis=0)
    q_all = _queries(x2, gq_ref, wq_ref)
    q = q_all[0:tm].astype(BF16)
    heads = [slice(h * XHEAD_DIM, (h + 1) * XHEAD_DIM) for h in range(N_XHEADS)]
    scores = [_dot(q[:, heads[h]], kt_ref[heads[h], :]) for h in range(N_XHEADS)]
    sample_scores = _cached_scores(q_all[tm:], kv)
    for h in range(N_XHEADS):
        p = _softmax(scores[h]).astype(BF16)
        o_ref[0:tm, heads[h]] = _dot(p, v_ref[:, heads[h]]).astype(BF16)
        if h == 1:
            _cached_outputs(sample_scores, kv, os_ref)
    o_ref[tm:, :] = os_ref[...].astype(BF16)
    x3 = x2 + _dot_rows(o_ref[...], wo_ref[...], EDGE_ROW_PARTS)
    x4 = _ffn_half(x3, g2_ref, w2in_ref, w2out_ref, act_ref)
    y = _rms(x4, gf_ref[...])
    y_ref[...] = y[0:tm]
    ys_ref[...] = y[tm:]


def _resident(arr):
    nd = arr.ndim
    return pl.BlockSpec(arr.shape, lambda *_: (0,) * nd, pipeline_mode=pl.Buffered(1))


def _params(n_grid):
    return pltpu.CompilerParams(dimension_semantics=("arbitrary",) * n_grid,
                                vmem_limit_bytes=VMEM_LIMIT_BYTES)


def kernel(x_prompt, x_sample, mem_prompt, state_pool, state_conv, cache_mem_k, cache_mem_v, g_ffn1, w_ffn1_in, w_ffn1_out, g_mix, w_mix_in, pool_w, pool_scale, conv_w, w_mix_out, g_xq, g_mem, w_xq, w_xk, w_xv, w_xo, g_ffn2, w_ffn2_in, w_ffn2_out, g_final):
    depth = g_ffn1.shape[0]
    assert depth == 1
    bp, seq, d = x_prompt.shape
    bs, tdec, _ = x_sample.shape
    l = 0

    poolw = pool_w[l]
    any_space = pl.BlockSpec(memory_space=pl.ANY)
    stage_shape = (CAST_SLOTS, CAST_ROWS, CAST_COLS)
    g1, gmix, gq, gmem, g2 = g_ffn1[l:l + 1], g_mix[l:l + 1], g_xq[l:l + 1], g_mem[l:l + 1], g_ffn2[l:l + 1]
    gf = g_final.reshape(1, d)
    pscale = pool_scale[l:l + 1]
    convw = conv_w[l]

    tm = TM_PROMPT
    nj = seq // tm
    n_steps = bp * nj
    seqs_per_step = bs // n_steps
    assert seqs_per_step * n_steps == bs
    srows = seqs_per_step * tdec
    xs = x_sample.reshape(bs * tdec, d)
    hist_pool = jnp.transpose(state_pool[l], (1, 0, 2))
    hist_conv = state_conv[l].reshape(n_steps, seqs_per_step, CONV_HIST, D_CONV)
    tile_rows = pl.BlockSpec((None, tm, d), lambda b, j: (b, j, 0))
    step_rows = pl.BlockSpec((srows, d), lambda b, j: (b * nj + j, 0))
    step_conv = pl.BlockSpec((None, seqs_per_step, CONV_HIST, D_CONV), lambda b, j: (b * nj + j, 0, 0, 0))
    mix_cast_shapes = [(d, 2 * D_FF), (D_FF, d), (d, D_POOL + 3 * D_CONV), (D_POOL + D_CONV, d), (d, d), (d, d)]
    batch_mem = pl.BlockSpec((None, N_MEM, d), lambda b, j: (b, 0, 0))
    x2p, x2s, npool_p, nconv_p, npool_s, nconv_s, mk, mv, kt, vb = pl.pallas_call(
        _mix_kernel,
        grid=(bp, nj),
        in_specs=[tile_rows, step_rows, any_space, step_conv, batch_mem,
                  _resident(g1), any_space, any_space, _resident(gmix), any_space, _resident(poolw),
                  _resident(pscale), _resident(convw), any_space, _resident(gmem), any_space, any_space],
        out_specs=[tile_rows, step_rows,
                   pl.BlockSpec((POOL_HIST, bp, D_POOL), lambda b, j: (0, 0, 0)),
                   pl.BlockSpec((None, CONV_HIST, D_CONV), lambda b, j: (b, 0, 0)),
                   any_space, step_conv, any_space, any_space,
                   pl.BlockSpec((None, d, N_MEM), lambda b, j: (b, 0, 0)),
                   pl.BlockSpec((None, N_MEM, d), lambda b, j: (b, 0, 0))],
        out_shape=[jax.ShapeDtypeStruct((bp, seq, d), F32),
                   jax.ShapeDtypeStruct((bs * tdec, d), F32),
                   jax.ShapeDtypeStruct((POOL_HIST, bp, D_POOL), F32),
                   jax.ShapeDtypeStruct((bp, CONV_HIST, D_CONV), F32),
                   jax.ShapeDtypeStruct((POOL_HIST, bs, D_POOL), F32),
                   jax.ShapeDtypeStruct((n_steps, seqs_per_step, CONV_HIST, D_CONV), F32),
                   jax.ShapeDtypeStruct((1, bp, N_MEM, N_XHEADS, XHEAD_DIM), F32),
                   jax.ShapeDtypeStruct((1, bp, N_MEM, N_XHEADS, XHEAD_DIM), F32),
                   jax.ShapeDtypeStruct((bp, d, N_MEM), BF16),
                   jax.ShapeDtypeStruct((bp, N_MEM, d), BF16)],
        scratch_shapes=[pltpu.VMEM((tm + srows, D_FF), BF16),
                        pltpu.VMEM((tm + POOL_PAD, D_POOL), F32),
                        pltpu.VMEM((tm + CONV_PAD, D_CONV), F32),
                        pltpu.VMEM((len(POOL_WINDOWS), srows, POOL_GROUP), F32),
                        pltpu.VMEM((len(POOL_WINDOWS), srows, POOL_GROUP), F32),
                        pltpu.VMEM((seqs_per_step, CONV_PAD + tdec, D_CONV), F32),
                        pltpu.VMEM((tm + srows, d), BF16)]
                       + [pltpu.VMEM(s, BF16) for s in mix_cast_shapes]
                       + [pltpu.VMEM(stage_shape, F32),
                          pltpu.SemaphoreType.DMA((CAST_SLOTS,)),
                          pltpu.SemaphoreType.DMA((2, N_XHEADS)),
                          pltpu.VMEM((2, POOL_HIST, seqs_per_step, D_POOL), F32),
                          pltpu.VMEM((2, POOL_HIST, seqs_per_step, D_POOL), F32),
                          pltpu.SemaphoreType.DMA((2, 2))],
        compiler_params=_params(2),
        name="mix",
    )(x_prompt, xs, hist_pool, hist_conv, mem_prompt, g1, w_ffn1_in, w_ffn1_out, gmix, w_mix_in, poolw, pscale,
      convw, w_mix_out, gmem, w_xk, w_xv)

    assert (2 * seqs_per_step, N_MEM, d) == stage_shape
    tail_cast_shapes = [(d, d), (d, d), (d, 2 * D_FF), (D_FF, d)]
    y_prompt, y_sample = pl.pallas_call(
        _tail_kernel,
        grid=(bp, nj),
        in_specs=[tile_rows,
                  pl.BlockSpec((None, d, N_MEM), lambda b, j: (b, 0, 0)),
                  pl.BlockSpec((None, N_MEM, d), lambda b, j: (b, 0, 0)),
                  step_rows,
                  any_space, any_space,
                  _resident(gq), any_space, any_space, _resident(g2), any_space, any_space, _resident(gf)],
        out_specs=[tile_rows, step_rows],
        out_shape=[jax.ShapeDtypeStruct((bp, seq, d), F32),
                   jax.ShapeDtypeStruct((bs * tdec, d), F32)],
        scratch_shapes=[pltpu.VMEM((tm + srows, D_FF), BF16),
                        pltpu.VMEM((tm + srows, d), BF16),
                        pltpu.VMEM((srows, d), F32)]
                       + [pltpu.VMEM(s, BF16) for s in tail_cast_shapes]
                       + [pltpu.VMEM((2,) + stage_shape, F32),
                          pltpu.SemaphoreType.DMA((2, 2, N_XHEADS)),
                          pltpu.SemaphoreType.DMA((CAST_SLOTS,))],
        compiler_params=_params(2),
        name="tail",
    )(x2p, kt, vb, x2s, cache_mem_k, cache_mem_v, gq, w_xq, w_xo, g2, w_ffn2_in, w_ffn2_out, gf)

    return (y_prompt,
            y_sample.reshape(bs, tdec, d),
            jnp.transpose(npool_p, (1, 0, 2))[None],
            nconv_p[None],
            mk,
            mv,
            jnp.transpose(npool_s, (1, 0, 2))[None],
            nconv_s.reshape(1, bs, CONV_HIST, D_CONV))
```

```python
import jax
import jax.numpy as jnp
from jax import lax
from jax.experimental import pallas as pl
from jax.experimental.pallas import tpu as pltpu

D_MODEL = 1024
D_POOL = 512
D_CONV = 512
POOL_WINDOWS = (2, 4, 8, 16)
POOL_GROUP = 128
POOL_HIST = 15
CONV_HIST = 2
D_FF = 2816
N_MEM = 256
N_XHEADS = 4
XHEAD_DIM = 256
EPS = 1e-6
PAST_LEN = 16384

FF_CHUNK = 512
FF_CHUNKS = tuple((s, min(FF_CHUNK, D_FF - s)) for s in range(0, D_FF, FF_CHUNK))
POOL_PAD = 16
CONV_PAD = 8
VMEM_LIMIT_BYTES = 62 * 1024 * 1024

CAST_SLOTS, CAST_ROWS, CAST_COLS = 8, 256, 1024
MIX_STAGE_SLOTS = 6
N_KV_BLOCKS, N_MIXIN_BLOCKS, N_MIXOUT_BLOCKS = 8, 8, 4
N_Q_BLOCKS = 4
W_IN_COL_ORDER = (0, 2, 3, 1, 4, 5)
FFN_BLOCKS_BEFORE_CHUNK = (12, 16, 20, 25, 30, 35)
FFN_BLOCKS_TOTAL = 35
EDGE_ROW_PARTS = 2
TM_PROMPT = 512

F32 = jnp.float32
BF16 = jnp.bfloat16


def _rms(x, g):
    y = x * lax.rsqrt(jnp.mean(x * x, axis=-1, keepdims=True) + EPS)
    return y * g


def _dot(a, b):
    return jnp.dot(a, b, preferred_element_type=F32)


def _dot_rows(a, b, parts):
    rows = a.shape[0] // parts
    return jnp.concatenate([_dot(a[i * rows:(i + 1) * rows], b) for i in range(parts)], axis=0)


def _ffn_half(x, g_ref, win_ref, wout_ref, act_ref, need=None):
    m = x.shape[0]
    h = _rms(x, g_ref[...]).astype(BF16)
    for i, (s0, n) in enumerate(FF_CHUNKS):
        if need is not None:
            need(FFN_BLOCKS_BEFORE_CHUNK[i])
        parts = EDGE_ROW_PARTS if i == 0 else 1
        gate = _dot_rows(h, win_ref[:, s0:s0 + n], parts)
        up = _dot_rows(h, win_ref[:, D_FF + s0:D_FF + s0 + n], parts)
        silu = gate * (1.0 / (1.0 + jnp.exp(-gate)))
        act_ref[0:m, s0:s0 + n] = (silu * up).astype(BF16)
    if need is not None:
        need(FFN_BLOCKS_TOTAL)
    y = _dot_rows(act_ref[0:m, :], wout_ref[...], EDGE_ROW_PARTS)
    return x + 0.5 * y


def _weight_blocks(w_hbm, dst, col_order=None):
    rows, cols = dst.shape
    col_starts = list(range(0, cols, CAST_COLS))
    if col_order is not None:
        col_starts = [col_starts[i] for i in col_order]
    return [(w_hbm, dst, r0, min(CAST_ROWS, rows - r0), c0, min(CAST_COLS, cols - c0))
            for c0 in col_starts for r0 in range(0, rows, CAST_ROWS)]


class _WeightCaster:
    def __init__(self, blocks, stage_ref, sem):
        self.blocks, self.stage_ref, self.sem = blocks, stage_ref, sem
        self.n_slots = stage_ref.shape[0]
        self.done = 0

    def _copy(self, i):
        w_hbm, _, r0, nr, c0, nc = self.blocks[i]
        slot = i % self.n_slots
        return pltpu.make_async_copy(w_hbm.at[0, pl.ds(r0, nr), pl.ds(c0, nc)],
                                     self.stage_ref.at[slot, pl.ds(0, nr), pl.ds(0, nc)], self.sem.at[slot])

    def start(self):
        for i in range(min(self.n_slots, len(self.blocks))):
            self._copy(i).start()

    def need(self, k):
        while self.done < min(k, len(self.blocks)):
            i = self.done
            _, dst, r0, nr, c0, nc = self.blocks[i]
            self._copy(i).wait()
            dst[r0:r0 + nr, c0:c0 + nc] = self.stage_ref[i % self.n_slots, 0:nr, 0:nc].astype(BF16)
            if i + self.n_slots < len(self.blocks):
                self._copy(i + self.n_slots).start()
            self.done += 1


def _window_sum(x, w):
    s = x
    sh = 1
    while sh < w:
        s = s + pltpu.roll(s, sh, 0)
        sh *= 2
    return s


def _pool_delta_rows(ext, a, pos, w, extract):
    ssum = extract(_window_sum(ext, w))
    cnt = jnp.minimum(pos + 1, w).astype(F32)
    return ssum / cnt - a


def _pool_delta_steps(hist, new, w, pos0):
    n_hist = len(hist)
    sums = dict(enumerate(hist + new))
    sh = 1
    while sh < w:
        sums = {i: sums[i] + sums[i - sh] for i in sums if (i - sh) in sums}
        sh *= 2
    return [sums[n_hist + tt] / float(min(pos0 + tt + 1, w)) - new[tt] for tt in range(len(new))]


def _conv3(ext, convw_ref, extract):
    u0 = extract(pltpu.roll(ext, 2, 0))
    u1 = extract(pltpu.roll(ext, 1, 0))
    u2 = extract(ext)
    return convw_ref[0:1, :] * u0 + convw_ref[1:2, :] * u1 + convw_ref[2:3, :] * u2


def _memkv_copies(kv_scr, mk_hbm, mv_hbm, sem, b):
    copies = []
    for h in range(N_XHEADS):
        cols = pl.ds(h * XHEAD_DIM, XHEAD_DIM)
        copies.append(pltpu.make_async_copy(kv_scr.at[0, :, cols], mk_hbm.at[0, b, :, h, :], sem.at[0, h]))
        copies.append(pltpu.make_async_copy(kv_scr.at[1, :, cols], mv_hbm.at[0, b, :, h, :], sem.at[1, h]))
    return copies


def _mix_kernel(x_ref, xs_ref, hpool_hbm, hconv_ref, mem_ref, g1_ref, w1in_hbm, w1out_hbm, gmix_ref, wmixin_hbm,
                poolw_ref, pscale_ref, convw_ref, wmixout_hbm, gmem_ref, wk_hbm, wv_hbm,
                x2_ref, x2s_ref, npool_ref, nconv_ref, npool_s_hbm, nconv_s_ref, mk_hbm, mv_hbm, kt_ref, vb_ref,
                act_ref, epool_ref, econv_ref, a_ref, d_ref, econv_s_ref, mixed_ref,
                w1in_ref, w1out_ref, wmixin_ref, wmixout_ref, wk_ref, wv_ref, stage_ref, stage_sem, memkv_sem,
                hpool2_ref, npool2_s_ref, pool_sem):
    tm = x_ref.shape[0]
    ns = hpool2_ref.shape[2]
    t = xs_ref.shape[0] // ns
    crow = CONV_PAD + t
    b = pl.program_id(0)
    j = pl.program_id(1)
    step = b * pl.num_programs(1) + j
    n_steps = pl.num_programs(0) * pl.num_programs(1)
    slot = lax.rem(step, 2)

    def hist_in(st, sl):
        return pltpu.make_async_copy(hpool_hbm.at[:, pl.ds(st * ns, ns), :], hpool2_ref.at[sl], pool_sem.at[0, sl])

    def state_out(st, sl):
        return pltpu.make_async_copy(npool2_s_ref.at[sl], npool_s_hbm.at[:, pl.ds(st * ns, ns), :],
                                     pool_sem.at[1, sl])

    @pl.when(step == 0)
    def _():
        hist_in(0, 0).start()

    @pl.when(step >= 2)
    def _():
        state_out(step - 2, slot).wait()

    hist_in(step, slot).wait()

    @pl.when(step + 1 < n_steps)
    def _():
        hist_in(step + 1, 1 - slot).start()

    hpool_ref = hpool2_ref.at[slot]
    npool_s_ref = npool2_s_ref.at[slot]

    def body(need):
        @pl.when(jnp.logical_and(j == 0, b > 0))
        def _():
            for c in _memkv_copies(stage_ref, mk_hbm, mv_hbm, memkv_sem, b - 1):
                c.wait()

        need(N_KV_BLOCKS)

        @pl.when(j == 0)
        def _():
            epool_ref[0:POOL_PAD, :] = jnp.zeros((POOL_PAD, D_POOL), F32)
            econv_ref[0:CONV_PAD, :] = jnp.zeros((CONV_PAD, D_CONV), F32)
            mn = _rms(mem_ref[...], gmem_ref[...]).astype(BF16)
            k = _dot(mn, wk_ref[...])
            v = _dot(mn, wv_ref[...])
            stage_ref[0] = k
            stage_ref[1] = v
            kt_ref[...] = k.T.astype(BF16)
            vb_ref[...] = v.astype(BF16)
            for c in _memkv_copies(stage_ref, mk_hbm, mv_hbm, memkv_sem, b):
                c.start()

        x1 = _ffn_half(jnp.concatenate([x_ref[...], xs_ref[...]], axis=0), g1_ref, w1in_ref, w1out_ref, act_ref,
                       need=lambda k: need(N_KV_BLOCKS + k))
        need(N_KV_BLOCKS + FFN_BLOCKS_TOTAL + N_MIXIN_BLOCKS)
        z = _dot_rows(_rms(x1, gmix_ref[...]).astype(BF16), wmixin_ref[...], EDGE_ROW_PARTS)
        a = z[:, 0:D_POOL]
        cb = z[:, D_POOL:D_POOL + D_CONV]
        u = z[:, D_POOL + D_CONV:D_POOL + 2 * D_CONV] * z[:, D_POOL + 2 * D_CONV:]

        epool_ref[POOL_PAD:, :] = a[0:tm]
        pos = j * tm + lax.broadcasted_iota(jnp.int32, (tm, 1), 0)
        for gi, w in enumerate(POOL_WINDOWS):
            sl = slice(gi * POOL_GROUP, (gi + 1) * POOL_GROUP)
            d_prompt = _pool_delta_rows(epool_ref[:, sl], a[0:tm, sl], pos, w, lambda s: s[POOL_PAD:, :])
            a_ref[gi] = a[tm:, sl]
            hist = [hpool_ref[i, :, sl] for i in range(POOL_HIST)]
            new = [a_ref[gi, pl.ds(tt, ns, stride=t), :] for tt in range(t)]
            for r, slab in enumerate((hist + new)[-POOL_HIST:]):
                npool_s_ref[r, :, sl] = slab
            for tt, slab in enumerate(_pool_delta_steps(hist, new, w, PAST_LEN)):
                d_ref[gi, pl.ds(tt, ns, stride=t), :] = slab
            d_all = jnp.concatenate([d_prompt, d_ref[gi]], axis=0).astype(BF16)
            mixed_ref[:, sl] = (_dot(d_all, poolw_ref[gi].astype(BF16)) * pscale_ref[:, sl]).astype(BF16)

        econv_ref[CONV_PAD:, :] = u[0:tm]
        yc_prompt = _conv3(econv_ref[...], convw_ref, lambda s: s[CONV_PAD:, :])
        econv_s_ref[:, 0:CONV_PAD - CONV_HIST, :] = jnp.zeros((ns, CONV_PAD - CONV_HIST, D_CONV), F32)
        econv_s_ref[:, CONV_PAD - CONV_HIST:CONV_PAD, :] = hconv_ref[...]
        econv_s_ref[:, CONV_PAD:, :] = u[tm:].reshape(ns, t, D_CONV)
        yc_sample = _conv3(econv_s_ref[...].reshape(ns * crow, D_CONV), convw_ref,
                           lambda s: s.reshape(ns, crow, D_CONV)[:, CONV_PAD:, :].reshape(ns * t, D_CONV))
        mixed_ref[:, D_POOL:] = (cb * jnp.concatenate([yc_prompt, yc_sample], axis=0)).astype(BF16)

        need(N_KV_BLOCKS + FFN_BLOCKS_TOTAL + N_MIXIN_BLOCKS + N_MIXOUT_BLOCKS)
        x2 = x1 + _dot_rows(mixed_ref[...], wmixout_ref[...], EDGE_ROW_PARTS)
        x2_ref[...] = x2[0:tm]
        x2s_ref[...] = x2[tm:]
        nconv_s_ref[...] = econv_s_ref[:, crow - CONV_HIST:crow, :]
        state_out(step, slot).start()

    @pl.when(step == 0)
    def _():
        caster = _WeightCaster(
            _weight_blocks(wk_hbm, wk_ref) + _weight_blocks(wv_hbm, wv_ref)
            + _weight_blocks(w1in_hbm, w1in_ref, W_IN_COL_ORDER) + _weight_blocks(w1out_hbm, w1out_ref)
            + _weight_blocks(wmixin_hbm, wmixin_ref) + _weight_blocks(wmixout_hbm, wmixout_ref),
            stage_ref.at[pl.ds(2, MIX_STAGE_SLOTS - 2)], stage_sem)
        caster.start()
        body(caster.need)

    @pl.when(step > 0)
    def _():
        body(lambda k: None)

    epool_ref[0:POOL_PAD, :] = epool_ref[tm:tm + POOL_PAD, :]
    econv_ref[0:CONV_PAD, :] = econv_ref[tm:tm + CONV_PAD, :]

    @pl.when(j == pl.num_programs(1) - 1)
    def _():
        for r in range(POOL_HIST):
            row = POOL_PAD - POOL_HIST + r
            npool_ref[r, pl.ds(b, 1), :] = epool_ref[row:row + 1, :]
        nconv_ref[...] = econv_ref[CONV_PAD - CONV_HIST:CONV_PAD, :]

    @pl.when(jnp.logical_and(b == pl.num_programs(0) - 1, j == pl.num_programs(1) - 1))
    def _():
        for c in _memkv_copies(stage_ref, mk_hbm, mv_hbm, memkv_sem, b):
            c.wait()
        state_out(step - 1, 1 - slot).wait()
        state_out(step, slot).wait()


def _softmax(s):
    e = jnp.exp(s - jnp.max(s, axis=-1, keepdims=True))
    return e / jnp.sum(e, axis=-1, keepdims=True)


def _queries(x, gq_ref, wq_ref):
    return _dot_rows(_rms(x, gq_ref[...]).astype(BF16), wq_ref[...], EDGE_ROW_PARTS) * (XHEAD_DIM ** -0.5)


def _kv_copies(k_hbm, v_hbm, kv_ref, sem, step, slot):
    ns = kv_ref.shape[1] // 2
    seqs = pl.ds(step * ns, ns)
    copies = []
    for h in range(N_XHEADS):
        cols = pl.ds(h * XHEAD_DIM, XHEAD_DIM)
        copies.append(pltpu.make_async_copy(k_hbm.at[0, seqs, :, h, :], kv_ref.at[slot, pl.ds(0, ns), :, cols],
                                            sem.at[slot, 0, h]))
        copies.append(pltpu.make_async_copy(v_hbm.at[0, seqs, :, h, :], kv_ref.at[slot, pl.ds(ns, ns), :, cols],
                                            sem.at[slot, 1, h]))
    return copies


def _cached_scores(q, kv_ref):
    ns = kv_ref.shape[0] // 2
    t = q.shape[0] // ns
    d = q.shape[1]
    rows_per_seq = N_XHEADS * t
    row_head = lax.broadcasted_iota(jnp.int32, (rows_per_seq, d), 0) // t
    col_head = lax.broadcasted_iota(jnp.int32, (rows_per_seq, d), 1) // XHEAD_DIM
    own_head = row_head == col_head
    scores = []
    for s in range(ns):
        qe = jnp.where(own_head, jnp.concatenate([q[s * t:(s + 1) * t, :]] * N_XHEADS, axis=0), 0.0).astype(BF16)
        scores.append(lax.dot_general(qe, kv_ref[s].astype(BF16), (((1,), (1,)), ((), ())),
                                      preferred_element_type=F32))
    return jnp.concatenate(scores, axis=0)


def _cached_outputs(scores, kv_ref, o_ref):
    ns = kv_ref.shape[0] // 2
    t = o_ref.shape[0] // ns
    rows_per_seq = N_XHEADS * t
    p = _softmax(scores).astype(BF16)
    for s in range(ns):
        oa = _dot(p[s * rows_per_seq:(s + 1) * rows_per_seq, :], kv_ref[ns + s].astype(BF16))
        for h in range(N_XHEADS):
            sl = slice(h * XHEAD_DIM, (h + 1) * XHEAD_DIM)
            o_ref[s * t:(s + 1) * t, sl] = oa[h * t:(h + 1) * t, sl]


def _tail_kernel(x_ref, kt_ref, v_ref, x2s_ref, ck_hbm, cv_hbm, gq_ref, wq_hbm, wo_hbm, g2_ref,
                 w2in_hbm, w2out_hbm, gf_ref,
                 y_ref, ys_ref,
                 act_ref, o_ref, os_ref, wq_ref, wo_ref, w2in_ref, w2out_ref, kv_ref, kv_sem, stage_sem):
    step = pl.program_id(0) * pl.num_programs(1) + pl.program_id(1)
    n_steps = pl.num_programs(0) * pl.num_programs(1)
    tm = x_ref.shape[0]
    slot = lax.rem(step, 2)

    @pl.when(step == 0)
    def _():
        for c in _kv_copies(ck_hbm, cv_hbm, kv_ref, kv_sem, 0, 0):
            c.start()

    for c in _kv_copies(ck_hbm, cv_hbm, kv_ref, kv_sem, step, slot):
        c.wait()

    @pl.when(jnp.logical_and(step > 0, step + 1 < n_steps))
    def _():
        for c in _kv_copies(ck_hbm, cv_hbm, kv_ref, kv_sem, step + 1, 1 - slot):
            c.start()

    def body(need):
        kv = kv_ref.at[slot]
        x2 = jnp.concatenate([x_ref[...], x2s_ref[...]], axis=0)
        need(N_Q_BLOCKS)
        q_all = _queries(x2, gq_ref, wq_ref)
        q = q_all[0:tm].astype(BF16)
        heads = [slice(h * XHEAD_DIM, (h + 1) * XHEAD_DIM) for h in range(N_XHEADS)]
        scores = [_dot(q[:, heads[h]], kt_ref[heads[h], :]) for h in range(N_XHEADS)]
        sample_scores = _cached_scores(q_all[tm:], kv)
        for h in range(N_XHEADS):
            p = _softmax(scores[h]).astype(BF16)
            o_ref[0:tm, heads[h]] = _dot(p, v_ref[:, heads[h]]).astype(BF16)
            if h == 1:
                _cached_outputs(sample_scores, kv, os_ref)
        o_ref[tm:, :] = os_ref[...].astype(BF16)
        need(2 * N_Q_BLOCKS)
        x3 = x2 + _dot_rows(o_ref[...], wo_ref[...], EDGE_ROW_PARTS)
        x4 = _ffn_half(x3, g2_ref, w2in_ref, w2out_ref, act_ref, need=lambda k: need(2 * N_Q_BLOCKS + k))
        y = _rms(x4, gf_ref[...])
        y_ref[...] = y[0:tm]
        ys_ref[...] = y[tm:]

    @pl.when(step == 0)
    def _():
        caster = _WeightCaster(
            _weight_blocks(wq_hbm, wq_ref) + _weight_blocks(wo_hbm, wo_ref)
            + _weight_blocks(w2in_hbm, w2in_ref, W_IN_COL_ORDER) + _weight_blocks(w2out_hbm, w2out_ref),
            kv_ref.at[1], stage_sem)
        caster.start()
        body(caster.need)
        for c in _kv_copies(ck_hbm, cv_hbm, kv_ref, kv_sem, 1, 1):
            c.start()

    @pl.when(step > 0)
    def _():
        body(lambda k: None)


def _resident(arr):
    nd = arr.ndim
    return pl.BlockSpec(arr.shape, lambda *_: (0,) * nd, pipeline_mode=pl.Buffered(1))


def _params(n_grid):
    return pltpu.CompilerParams(dimension_semantics=("arbitrary",) * n_grid,
                                vmem_limit_bytes=VMEM_LIMIT_BYTES)


def kernel(x_prompt, x_sample, mem_prompt, state_pool, state_conv, cache_mem_k, cache_mem_v, g_ffn1, w_ffn1_in, w_ffn1_out, g_mix, w_mix_in, pool_w, pool_scale, conv_w, w_mix_out, g_xq, g_mem, w_xq, w_xk, w_xv, w_xo, g_ffn2, w_ffn2_in, w_ffn2_out, g_final):
    depth = g_ffn1.shape[0]
    assert depth == 1
    bp, seq, d = x_prompt.shape
    bs, tdec, _ = x_sample.shape
    l = 0

    poolw = pool_w[l]
    any_space = pl.BlockSpec(memory_space=pl.ANY)
    stage_shape = (CAST_SLOTS, CAST_ROWS, CAST_COLS)
    g1, gmix, gq, gmem, g2 = g_ffn1[l:l + 1], g_mix[l:l + 1], g_xq[l:l + 1], g_mem[l:l + 1], g_ffn2[l:l + 1]
    gf = g_final.reshape(1, d)
    pscale = pool_scale[l:l + 1]
    convw = conv_w[l]

    tm = TM_PROMPT
    nj = seq // tm
    n_steps = bp * nj
    seqs_per_step = bs // n_steps
    assert seqs_per_step * n_steps == bs
    srows = seqs_per_step * tdec
    xs = x_sample.reshape(bs * tdec, d)
    hist_pool = jnp.transpose(state_pool[l], (1, 0, 2))
    hist_conv = state_conv[l].reshape(n_steps, seqs_per_step, CONV_HIST, D_CONV)
    tile_rows = pl.BlockSpec((None, tm, d), lambda b, j: (b, j, 0))
    step_rows = pl.BlockSpec((srows, d), lambda b, j: (b * nj + j, 0))
    step_conv = pl.BlockSpec((None, seqs_per_step, CONV_HIST, D_CONV), lambda b, j: (b * nj + j, 0, 0, 0))
    mix_cast_shapes = [(d, 2 * D_FF), (D_FF, d), (d, D_POOL + 3 * D_CONV), (D_POOL + D_CONV, d), (d, d), (d, d)]
    batch_mem = pl.BlockSpec((None, N_MEM, d), lambda b, j: (b, 0, 0))
    x2p, x2s, npool_p, nconv_p, npool_s, nconv_s, mk, mv, kt, vb = pl.pallas_call(
        _mix_kernel,
        grid=(bp, nj),
        in_specs=[tile_rows, step_rows, any_space, step_conv, batch_mem,
                  _resident(g1), any_space, any_space, _resident(gmix), any_space, _resident(poolw),
                  _resident(pscale), _resident(convw), any_space, _resident(gmem), any_space, any_space],
        out_specs=[tile_rows, step_rows,
                   pl.BlockSpec((POOL_HIST, bp, D_POOL), lambda b, j: (0, 0, 0)),
                   pl.BlockSpec((None, CONV_HIST, D_CONV), lambda b, j: (b, 0, 0)),
                   any_space, step_conv, any_space, any_space,
                   pl.BlockSpec((None, d, N_MEM), lambda b, j: (b, 0, 0)),
                   pl.BlockSpec((None, N_MEM, d), lambda b, j: (b, 0, 0))],
        out_shape=[jax.ShapeDtypeStruct((bp, seq, d), F32),
                   jax.ShapeDtypeStruct((bs * tdec, d), F32),
                   jax.ShapeDtypeStruct((POOL_HIST, bp, D_POOL), F32),
                   jax.ShapeDtypeStruct((bp, CONV_HIST, D_CONV), F32),
                   jax.ShapeDtypeStruct((POOL_HIST, bs, D_POOL), F32),
                   jax.ShapeDtypeStruct((n_steps, seqs_per_step, CONV_HIST, D_CONV), F32),
                   jax.ShapeDtypeStruct((1, bp, N_MEM, N_XHEADS, XHEAD_DIM), F32),
                   jax.ShapeDtypeStruct((1, bp, N_MEM, N_XHEADS, XHEAD_DIM), F32),
                   jax.ShapeDtypeStruct((bp, d, N_MEM), BF16),
                   jax.ShapeDtypeStruct((bp, N_MEM, d), BF16)],
        scratch_shapes=[pltpu.VMEM((tm + srows, D_FF), BF16),
                        pltpu.VMEM((tm + POOL_PAD, D_POOL), F32),
                        pltpu.VMEM((tm + CONV_PAD, D_CONV), F32),
                        pltpu.VMEM((len(POOL_WINDOWS), srows, POOL_GROUP), F32),
                        pltpu.VMEM((len(POOL_WINDOWS), srows, POOL_GROUP), F32),
                        pltpu.VMEM((seqs_per_step, CONV_PAD + tdec, D_CONV), F32),
                        pltpu.VMEM((tm + srows, d), BF16)]
                       + [pltpu.VMEM(s, BF16) for s in mix_cast_shapes]
                       + [pltpu.VMEM((MIX_STAGE_SLOTS, CAST_ROWS, CAST_COLS), F32),
                          pltpu.SemaphoreType.DMA((MIX_STAGE_SLOTS,)),
                          pltpu.SemaphoreType.DMA((2, N_XHEADS)),
                          pltpu.VMEM((2, POOL_HIST, seqs_per_step, D_POOL), F32),
                          pltpu.VMEM((2, POOL_HIST, seqs_per_step, D_POOL), F32),
                          pltpu.SemaphoreType.DMA((2, 2))],
        compiler_params=_params(2),
        name="mix",
    )(x_prompt, xs, hist_pool, hist_conv, mem_prompt, g1, w_ffn1_in, w_ffn1_out, gmix, w_mix_in, poolw, pscale,
      convw, w_mix_out, gmem, w_xk, w_xv)

    assert (2 * seqs_per_step, N_MEM, d) == stage_shape
    tail_cast_shapes = [(d, d), (d, d), (d, 2 * D_FF), (D_FF, d)]
    y_prompt, y_sample = pl.pallas_call(
        _tail_kernel,
        grid=(bp, nj),
        in_specs=[tile_rows,
                  pl.BlockSpec((None, d, N_MEM), lambda b, j: (b, 0, 0)),
                  pl.BlockSpec((None, N_MEM, d), lambda b, j: (b, 0, 0)),
                  step_rows,
                  any_space, any_space,
                  _resident(gq), any_space, any_space, _resident(g2), any_space, any_space, _resident(gf)],
        out_specs=[tile_rows, step_rows],
        out_shape=[jax.ShapeDtypeStruct((bp, seq, d), F32),
                   jax.ShapeDtypeStruct((bs * tdec, d), F32)],
        scratch_shapes=[pltpu.VMEM((tm + srows, D_FF), BF16),
                        pltpu.VMEM((tm + srows, d), BF16),
                        pltpu.VMEM((srows, d), F32)]
                       + [pltpu.VMEM(s, BF16) for s in tail_cast_shapes]
                       + [pltpu.VMEM((2,) + stage_shape, F32),
                          pltpu.SemaphoreType.DMA((2, 2, N_XHEADS)),
                          pltpu.SemaphoreType.DMA((CAST_SLOTS,))],
        compiler_params=_params(2),
        name="tail",
    )(x2p, kt, vb, x2s, cache_mem_k, cache_mem_v, gq, w_xq, w_xo, g2, w_ffn2_in, w_ffn2_out, gf)

    return (y_prompt,
            y_sample.reshape(bs, tdec, d),
            jnp.transpose(npool_p, (1, 0, 2))[None],
            nconv_p[None],
            mk,
            mv,
            jnp.transpose(npool_s, (1, 0, 2))[None],
            nconv_s.reshape(1, bs, CONV_HIST, D_CONV))
```

```python
import jax
import jax.numpy as jnp
from jax import lax
from jax.experimental import pallas as pl
from jax.experimental.pallas import tpu as pltpu

D_MODEL = 1024
D_POOL = 512
D_CONV = 512
POOL_WINDOWS = (2, 4, 8, 16)
POOL_GROUP = 128
POOL_HIST = 15
CONV_HIST = 2
D_FF = 2816
N_MEM = 256
N_XHEADS = 4
XHEAD_DIM = 256
EPS = 1e-6
PAST_LEN = 16384

FF_CHUNK = 512
FF_CHUNKS = tuple((s, min(FF_CHUNK, D_FF - s)) for s in range(0, D_FF, FF_CHUNK))
POOL_PAD = 16
CONV_PAD = 8
VMEM_LIMIT_BYTES = 60 * 1024 * 1024

CAST_SLOTS, CAST_ROWS, CAST_COLS = 8, 256, 1024
EDGE_ROW_PARTS = 2
TM_PROMPT = 512

F32 = jnp.float32
BF16 = jnp.bfloat16


def _rms(x, g):
    y = x * lax.rsqrt(jnp.mean(x * x, axis=-1, keepdims=True) + EPS)
    return y * g


def _dot(a, b):
    return jnp.dot(a, b, preferred_element_type=F32)


def _dot_rows(a, b, parts):
    rows = a.shape[0] // parts
    return jnp.concatenate([_dot(a[i * rows:(i + 1) * rows], b) for i in range(parts)], axis=0)


def _ffn_half(x, g_ref, win_ref, wout_ref, act_ref):
    m = x.shape[0]
    h = _rms(x, g_ref[...]).astype(BF16)
    for i, (s0, n) in enumerate(FF_CHUNKS):
        parts = EDGE_ROW_PARTS if i == 0 else 1
        gate = _dot_rows(h, win_ref[:, s0:s0 + n], parts)
        up = _dot_rows(h, win_ref[:, D_FF + s0:D_FF + s0 + n], parts)
        silu = gate * (1.0 / (1.0 + jnp.exp(-gate)))
        act_ref[0:m, s0:s0 + n] = (silu * up).astype(BF16)
    y = _dot_rows(act_ref[0:m, :], wout_ref[...], EDGE_ROW_PARTS)
    return x + 0.5 * y


def _cast_weights_to_vmem(weights, stage_ref, sem):
    n_slots, cast_rows, cast_cols = stage_ref.shape
    blocks = []
    for w_hbm, dst in weights:
        rows, cols = dst.shape
        for r0 in range(0, rows, cast_rows):
            for c0 in range(0, cols, cast_cols):
                blocks.append((w_hbm, dst, r0, min(cast_rows, rows - r0), c0, min(cast_cols, cols - c0)))

    def copy(i):
        w_hbm, _, r0, nr, c0, nc = blocks[i]
        slot = i % n_slots
        return pltpu.make_async_copy(w_hbm.at[0, pl.ds(r0, nr), pl.ds(c0, nc)],
                                     stage_ref.at[slot, pl.ds(0, nr), pl.ds(0, nc)], sem.at[slot])

    for i in range(min(n_slots, len(blocks))):
        copy(i).start()
    for i, (_, dst, r0, nr, c0, nc) in enumerate(blocks):
        copy(i).wait()
        dst[r0:r0 + nr, c0:c0 + nc] = stage_ref[i % n_slots, 0:nr, 0:nc].astype(BF16)
        if i + n_slots < len(blocks):
            copy(i + n_slots).start()


def _window_sum(x, w):
    s = x
    sh = 1
    while sh < w:
        s = s + pltpu.roll(s, sh, 0)
        sh *= 2
    return s


def _pool_delta_rows(ext, a, pos, w, extract):
    ssum = extract(_window_sum(ext, w))
    cnt = jnp.minimum(pos + 1, w).astype(F32)
    return ssum / cnt - a


def _pool_delta_steps(hist, new, w, pos0):
    n_hist = len(hist)
    sums = dict(enumerate(hist + new))
    sh = 1
    while sh < w:
        sums = {i: sums[i] + sums[i - sh] for i in sums if (i - sh) in sums}
        sh *= 2
    return [sums[n_hist + tt] / float(min(pos0 + tt + 1, w)) - new[tt] for tt in range(len(new))]


def _conv3(ext, convw_ref, extract):
    u0 = extract(pltpu.roll(ext, 2, 0))
    u1 = extract(pltpu.roll(ext, 1, 0))
    u2 = extract(ext)
    return convw_ref[0:1, :] * u0 + convw_ref[1:2, :] * u1 + convw_ref[2:3, :] * u2


def _memkv_copies(kv_scr, mk_hbm, mv_hbm, sem, b):
    copies = []
    for h in range(N_XHEADS):
        cols = pl.ds(h * XHEAD_DIM, XHEAD_DIM)
        copies.append(pltpu.make_async_copy(kv_scr.at[0, :, cols], mk_hbm.at[0, b, :, h, :], sem.at[0, h]))
        copies.append(pltpu.make_async_copy(kv_scr.at[1, :, cols], mv_hbm.at[0, b, :, h, :], sem.at[1, h]))
    return copies


def _mix_kernel(x_ref, xs_ref, hpool_hbm, hconv_ref, mem_ref, g1_ref, w1in_hbm, w1out_hbm, gmix_ref, wmixin_hbm,
                poolw_ref, pscale_ref, convw_ref, wmixout_hbm, gmem_ref, wk_hbm, wv_hbm,
                x2_ref, x2s_ref, npool_ref, nconv_ref, npool_s_hbm, nconv_s_ref, mk_hbm, mv_hbm, kt_ref, vb_ref,
                act_ref, epool_ref, econv_ref, a_ref, d_ref, econv_s_ref, mixed_ref,
                w1in_ref, w1out_ref, wmixin_ref, wmixout_ref, wk_ref, wv_ref, stage_ref, stage_sem, memkv_sem,
                hpool2_ref, npool2_s_ref, pool_sem):
    tm = x_ref.shape[0]
    ns = hpool2_ref.shape[2]
    t = xs_ref.shape[0] // ns
    crow = CONV_PAD + t
    b = pl.program_id(0)
    j = pl.program_id(1)
    step = b * pl.num_programs(1) + j
    n_steps = pl.num_programs(0) * pl.num_programs(1)
    slot = lax.rem(step, 2)

    def hist_in(st, sl):
        return pltpu.make_async_copy(hpool_hbm.at[:, pl.ds(st * ns, ns), :], hpool2_ref.at[sl], pool_sem.at[0, sl])

    def state_out(st, sl):
        return pltpu.make_async_copy(npool2_s_ref.at[sl], npool_s_hbm.at[:, pl.ds(st * ns, ns), :],
                                     pool_sem.at[1, sl])

    @pl.when(step == 0)
    def _():
        hist_in(0, 0).start()

    @pl.when(step >= 2)
    def _():
        state_out(step - 2, slot).wait()

    hist_in(step, slot).wait()

    @pl.when(step + 1 < n_steps)
    def _():
        hist_in(step + 1, 1 - slot).start()

    hpool_ref = hpool2_ref.at[slot]
    npool_s_ref = npool2_s_ref.at[slot]

    @pl.when(jnp.logical_and(b == 0, j == 0))
    def _():
        _cast_weights_to_vmem([(wk_hbm, wk_ref), (wv_hbm, wv_ref), (w1in_hbm, w1in_ref), (w1out_hbm, w1out_ref),
                               (wmixin_hbm, wmixin_ref), (wmixout_hbm, wmixout_ref)], stage_ref, stage_sem)

    @pl.when(jnp.logical_and(j == 0, b > 0))
    def _():
        for c in _memkv_copies(stage_ref, mk_hbm, mv_hbm, memkv_sem, b - 1):
            c.wait()

    @pl.when(j == 0)
    def _():
        epool_ref[0:POOL_PAD, :] = jnp.zeros((POOL_PAD, D_POOL), F32)
        econv_ref[0:CONV_PAD, :] = jnp.zeros((CONV_PAD, D_CONV), F32)
        mn = _rms(mem_ref[...], gmem_ref[...]).astype(BF16)
        k = _dot(mn, wk_ref[...])
        v = _dot(mn, wv_ref[...])
        stage_ref[0] = k
        stage_ref[1] = v
        kt_ref[...] = k.T.astype(BF16)
        vb_ref[...] = v.astype(BF16)
        for c in _memkv_copies(stage_ref, mk_hbm, mv_hbm, memkv_sem, b):
            c.start()

    x1 = _ffn_half(jnp.concatenate([x_ref[...], xs_ref[...]], axis=0), g1_ref, w1in_ref, w1out_ref, act_ref)
    z = _dot_rows(_rms(x1, gmix_ref[...]).astype(BF16), wmixin_ref[...], EDGE_ROW_PARTS)
    a = z[:, 0:D_POOL]
    cb = z[:, D_POOL:D_POOL + D_CONV]
    u = z[:, D_POOL + D_CONV:D_POOL + 2 * D_CONV] * z[:, D_POOL + 2 * D_CONV:]

    epool_ref[POOL_PAD:, :] = a[0:tm]
    pos = j * tm + lax.broadcasted_iota(jnp.int32, (tm, 1), 0)
    for gi, w in enumerate(POOL_WINDOWS):
        sl = slice(gi * POOL_GROUP, (gi + 1) * POOL_GROUP)
        d_prompt = _pool_delta_rows(epool_ref[:, sl], a[0:tm, sl], pos, w, lambda s: s[POOL_PAD:, :])
        a_ref[gi] = a[tm:, sl]
        hist = [hpool_ref[i, :, sl] for i in range(POOL_HIST)]
        new = [a_ref[gi, pl.ds(tt, ns, stride=t), :] for tt in range(t)]
        for r, slab in enumerate((hist + new)[-POOL_HIST:]):
            npool_s_ref[r, :, sl] = slab
        for tt, slab in enumerate(_pool_delta_steps(hist, new, w, PAST_LEN)):
            d_ref[gi, pl.ds(tt, ns, stride=t), :] = slab
        d_all = jnp.concatenate([d_prompt, d_ref[gi]], axis=0).astype(BF16)
        mixed_ref[:, sl] = (_dot(d_all, poolw_ref[gi].astype(BF16)) * pscale_ref[:, sl]).astype(BF16)

    econv_ref[CONV_PAD:, :] = u[0:tm]
    yc_prompt = _conv3(econv_ref[...], convw_ref, lambda s: s[CONV_PAD:, :])
    econv_s_ref[:, 0:CONV_PAD - CONV_HIST, :] = jnp.zeros((ns, CONV_PAD - CONV_HIST, D_CONV), F32)
    econv_s_ref[:, CONV_PAD - CONV_HIST:CONV_PAD, :] = hconv_ref[...]
    econv_s_ref[:, CONV_PAD:, :] = u[tm:].reshape(ns, t, D_CONV)
    yc_sample = _conv3(econv_s_ref[...].reshape(ns * crow, D_CONV), convw_ref,
                       lambda s: s.reshape(ns, crow, D_CONV)[:, CONV_PAD:, :].reshape(ns * t, D_CONV))
    mixed_ref[:, D_POOL:] = (cb * jnp.concatenate([yc_prompt, yc_sample], axis=0)).astype(BF16)

    x2 = x1 + _dot_rows(mixed_ref[...], wmixout_ref[...], EDGE_ROW_PARTS)
    x2_ref[...] = x2[0:tm]
    x2s_ref[...] = x2[tm:]
    nconv_s_ref[...] = econv_s_ref[:, crow - CONV_HIST:crow, :]
    state_out(step, slot).start()

    epool_ref[0:POOL_PAD, :] = epool_ref[tm:tm + POOL_PAD, :]
    econv_ref[0:CONV_PAD, :] = econv_ref[tm:tm + CONV_PAD, :]

    @pl.when(j == pl.num_programs(1) - 1)
    def _():
        for r in range(POOL_HIST):
            row = POOL_PAD - POOL_HIST + r
            npool_ref[r, pl.ds(b, 1), :] = epool_ref[row:row + 1, :]
        nconv_ref[...] = econv_ref[CONV_PAD - CONV_HIST:CONV_PAD, :]

    @pl.when(jnp.logical_and(b == pl.num_programs(0) - 1, j == pl.num_programs(1) - 1))
    def _():
        for c in _memkv_copies(stage_ref, mk_hbm, mv_hbm, memkv_sem, b):
            c.wait()
        state_out(step - 1, 1 - slot).wait()
        state_out(step, slot).wait()


def _softmax(s):
    e = jnp.exp(s - jnp.max(s, axis=-1, keepdims=True))
    return e / jnp.sum(e, axis=-1, keepdims=True)


def _queries(x, gq_ref, wq_ref):
    return _dot_rows(_rms(x, gq_ref[...]).astype(BF16), wq_ref[...], EDGE_ROW_PARTS) * (XHEAD_DIM ** -0.5)


def _kv_copies(k_hbm, v_hbm, kv_ref, sem, step, slot):
    ns = kv_ref.shape[1] // 2
    seqs = pl.ds(step * ns, ns)
    copies = []
    for h in range(N_XHEADS):
        cols = pl.ds(h * XHEAD_DIM, XHEAD_DIM)
        copies.append(pltpu.make_async_copy(k_hbm.at[0, seqs, :, h, :], kv_ref.at[slot, pl.ds(0, ns), :, cols],
                                            sem.at[slot, 0, h]))
        copies.append(pltpu.make_async_copy(v_hbm.at[0, seqs, :, h, :], kv_ref.at[slot, pl.ds(ns, ns), :, cols],
                                            sem.at[slot, 1, h]))
    return copies


def _cached_scores(q, kv_ref):
    ns = kv_ref.shape[0] // 2
    t = q.shape[0] // ns
    d = q.shape[1]
    rows_per_seq = N_XHEADS * t
    row_head = lax.broadcasted_iota(jnp.int32, (rows_per_seq, d), 0) // t
    col_head = lax.broadcasted_iota(jnp.int32, (rows_per_seq, d), 1) // XHEAD_DIM
    own_head = row_head == col_head
    scores = []
    for s in range(ns):
        qe = jnp.where(own_head, jnp.concatenate([q[s * t:(s + 1) * t, :]] * N_XHEADS, axis=0), 0.0).astype(BF16)
        scores.append(lax.dot_general(qe, kv_ref[s].astype(BF16), (((1,), (1,)), ((), ())),
                                      preferred_element_type=F32))
    return jnp.concatenate(scores, axis=0)


def _cached_outputs(scores, kv_ref, o_ref):
    ns = kv_ref.shape[0] // 2
    t = o_ref.shape[0] // ns
    rows_per_seq = N_XHEADS * t
    p = _softmax(scores).astype(BF16)
    for s in range(ns):
        oa = _dot(p[s * rows_per_seq:(s + 1) * rows_per_seq, :], kv_ref[ns + s].astype(BF16))
        for h in range(N_XHEADS):
            sl = slice(h * XHEAD_DIM, (h + 1) * XHEAD_DIM)
            o_ref[s * t:(s + 1) * t, sl] = oa[h * t:(h + 1) * t, sl]


def _tail_kernel(x_ref, kt_ref, v_ref, x2s_ref, ck_hbm, cv_hbm, gq_ref, wq_hbm, wo_hbm, g2_ref,
                 w2in_hbm, w2out_hbm, gf_ref,
                 y_ref, ys_ref,
                 act_ref, o_ref, os_ref, wq_ref, wo_ref, w2in_ref, w2out_ref, kv_ref, kv_sem, stage_sem):
    step = pl.program_id(0) * pl.num_programs(1) + pl.program_id(1)
    n_steps = pl.num_programs(0) * pl.num_programs(1)
    tm = x_ref.shape[0]
    slot = lax.rem(step, 2)

    @pl.when(step == 0)
    def _():
        for c in _kv_copies(ck_hbm, cv_hbm, kv_ref, kv_sem, 0, 0):
            c.start()
        _cast_weights_to_vmem([(wq_hbm, wq_ref), (wo_hbm, wo_ref), (w2in_hbm, w2in_ref),
                               (w2out_hbm, w2out_ref)], kv_ref.at[1], stage_sem)

    for c in _kv_copies(ck_hbm, cv_hbm, kv_ref, kv_sem, step, slot):
        c.wait()

    @pl.when(step + 1 < n_steps)
    def _():
        for c in _kv_copies(ck_hbm, cv_hbm, kv_ref, kv_sem, step + 1, 1 - slot):
            c.start()

    kv = kv_ref.at[slot]
    x2 = jnp.concatenate([x_ref[...], x2s_ref[...]], axis=0)
    q_all = _queries(x2, gq_ref, wq_ref)
    q = q_all[0:tm].astype(BF16)
    heads = [slice(h * XHEAD_DIM, (h + 1) * XHEAD_DIM) for h in range(N_XHEADS)]
    scores = [_dot(q[:, heads[h]], kt_ref[heads[h], :]) for h in range(N_XHEADS)]
    sample_scores = _cached_scores(q_all[tm:], kv)
    for h in range(N_XHEADS):
        p = _softmax(scores[h]).astype(BF16)
        o_ref[0:tm, heads[h]] = _dot(p, v_ref[:, heads[h]]).astype(BF16)
        if h == 1:
            _cached_outputs(sample_scores, kv, os_ref)
    o_ref[tm:, :] = os_ref[...].astype(BF16)
    x3 = x2 + _dot_rows(o_ref[...], wo_ref[...], EDGE_ROW_PARTS)
    x4 = _ffn_half(x3, g2_ref, w2in_ref, w2out_ref, act_ref)
    y = _rms(x4, gf_ref[...])
    y_ref[...] = y[0:tm]
    ys_ref[...] = y[tm:]


def _resident(arr):
    nd = arr.ndim
    return pl.BlockSpec(arr.shape, lambda *_: (0,) * nd, pipeline_mode=pl.Buffered(1))


def _params(n_grid):
    return pltpu.CompilerParams(dimension_semantics=("arbitrary",) * n_grid,
                                vmem_limit_bytes=VMEM_LIMIT_BYTES)


def kernel(x_prompt, x_sample, mem_prompt, state_pool, state_conv, cache_mem_k, cache_mem_v, g_ffn1, w_ffn1_in, w_ffn1_out, g_mix, w_mix_in, pool_w, pool_scale, conv_w, w_mix_out, g_xq, g_mem, w_xq, w_xk, w_xv, w_xo, g_ffn2, w_ffn2_in, w_ffn2_out, g_final):
    depth = g_ffn1.shape[0]
    assert depth == 1
    bp, seq, d = x_prompt.shape
    bs, tdec, _ = x_sample.shape
    l = 0

    poolw = pool_w[l]
    any_space = pl.BlockSpec(memory_space=pl.ANY)
    stage_shape = (CAST_SLOTS, CAST_ROWS, CAST_COLS)
    g1, gmix, gq, gmem, g2 = g_ffn1[l:l + 1], g_mix[l:l + 1], g_xq[l:l + 1], g_mem[l:l + 1], g_ffn2[l:l + 1]
    gf = g_final.reshape(1, d)
    pscale = pool_scale[l:l + 1]
    convw = conv_w[l]

    tm = TM_PROMPT
    nj = seq // tm
    n_steps = bp * nj
    seqs_per_step = bs // n_steps
    assert seqs_per_step * n_steps == bs
    srows = seqs_per_step * tdec
    xs = x_sample.reshape(bs * tdec, d)
    hist_pool = jnp.transpose(state_pool[l], (1, 0, 2))
    hist_conv = state_conv[l].reshape(n_steps, seqs_per_step, CONV_HIST, D_CONV)
    tile_rows = pl.BlockSpec((None, tm, d), lambda b, j: (b, j, 0))
    step_rows = pl.BlockSpec((srows, d), lambda b, j: (b * nj + j, 0))
    step_conv = pl.BlockSpec((None, seqs_per_step, CONV_HIST, D_CONV), lambda b, j: (b * nj + j, 0, 0, 0))
    mix_cast_shapes = [(d, 2 * D_FF), (D_FF, d), (d, D_POOL + 3 * D_CONV), (D_POOL + D_CONV, d), (d, d), (d, d)]
    batch_mem = pl.BlockSpec((None, N_MEM, d), lambda b, j: (b, 0, 0))
    x2p, x2s, npool_p, nconv_p, npool_s, nconv_s, mk, mv, kt, vb = pl.pallas_call(
        _mix_kernel,
        grid=(bp, nj),
        in_specs=[tile_rows, step_rows, any_space, step_conv, batch_mem,
                  _resident(g1), any_space, any_space, _resident(gmix), any_space, _resident(poolw),
                  _resident(pscale), _resident(convw), any_space, _resident(gmem), any_space, any_space],
        out_specs=[tile_rows, step_rows,
                   pl.BlockSpec((POOL_HIST, bp, D_POOL), lambda b, j: (0, 0, 0)),
                   pl.BlockSpec((None, CONV_HIST, D_CONV), lambda b, j: (b, 0, 0)),
                   any_space, step_conv, any_space, any_space,
                   pl.BlockSpec((None, d, N_MEM), lambda b, j: (b, 0, 0)),
                   pl.BlockSpec((None, N_MEM, d), lambda b, j: (b, 0, 0))],
        out_shape=[jax.ShapeDtypeStruct((bp, seq, d), F32),
                   jax.ShapeDtypeStruct((bs * tdec, d), F32),
                   jax.ShapeDtypeStruct((POOL_HIST, bp, D_POOL), F32),
                   jax.ShapeDtypeStruct((bp, CONV_HIST, D_CONV), F32),
                   jax.ShapeDtypeStruct((POOL_HIST, bs, D_POOL), F32),
                   jax.ShapeDtypeStruct((n_steps, seqs_per_step, CONV_HIST, D_CONV), F32),
                   jax.ShapeDtypeStruct((1, bp, N_MEM, N_XHEADS, XHEAD_DIM), F32),
                   jax.ShapeDtypeStruct((1, bp, N_MEM, N_XHEADS, XHEAD_DIM), F32),
                   jax.ShapeDtypeStruct((bp, d, N_MEM), BF16),
                   jax.ShapeDtypeStruct((bp, N_MEM, d), BF16)],
        scratch_shapes=[pltpu.VMEM((tm + srows, D_FF), BF16),
                        pltpu.VMEM((tm + POOL_PAD, D_POOL), F32),
                        pltpu.VMEM((tm + CONV_PAD, D_CONV), F32),
                        pltpu.VMEM((len(POOL_WINDOWS), srows, POOL_GROUP), F32),
                        pltpu.VMEM((len(POOL_WINDOWS), srows, POOL_GROUP), F32),
                        pltpu.VMEM((seqs_per_step, CONV_PAD + tdec, D_CONV), F32),
                        pltpu.VMEM((tm + srows, d), BF16)]
                       + [pltpu.VMEM(s, BF16) for s in mix_cast_shapes]
                       + [pltpu.VMEM(stage_shape, F32),
                          pltpu.SemaphoreType.DMA((CAST_SLOTS,)),
                          pltpu.SemaphoreType.DMA((2, N_XHEADS)),
                          pltpu.VMEM((2, POOL_HIST, seqs_per_step, D_POOL), F32),
                          pltpu.VMEM((2, POOL_HIST, seqs_per_step, D_POOL), F32),
                          pltpu.SemaphoreType.DMA((2, 2))],
        compiler_params=_params(2),
        name="mix",
    )(x_prompt, xs, hist_pool, hist_conv, mem_prompt, g1, w_ffn1_in, w_ffn1_out, gmix, w_mix_in, poolw, pscale,
      convw, w_mix_out, gmem, w_xk, w_xv)

    assert (2 * seqs_per_step, N_MEM, d) == stage_shape
    tail_cast_shapes = [(d, d), (d, d), (d, 2 * D_FF), (D_FF, d)]
    y_prompt, y_sample = pl.pallas_call(
        _tail_kernel,
        grid=(bp, nj),
        in_specs=[tile_rows,
                  pl.BlockSpec((None, d, N_MEM), lambda b, j: (b, 0, 0)),
                  pl.BlockSpec((None, N_MEM, d), lambda b, j: (b, 0, 0)),
                  step_rows,
                  any_space, any_space,
                  _resident(gq), any_space, any_space, _resident(g2), any_space, any_space, _resident(gf)],
        out_specs=[tile_rows, step_rows],
        out_shape=[jax.ShapeDtypeStruct((bp, seq, d), F32),
                   jax.ShapeDtypeStruct((bs * tdec, d), F32)],
        scratch_shapes=[pltpu.VMEM((tm + srows, D_FF), BF16),
                        pltpu.VMEM((tm + srows, d), BF16),
                        pltpu.VMEM((srows, d), F32)]
                       + [pltpu.VMEM(s, BF16) for s in tail_cast_shapes]
                       + [pltpu.VMEM((2,) + stage_shape, F32),
                          pltpu.SemaphoreType.DMA((2, 2, N_XHEADS)),
                          pltpu.SemaphoreType.DMA((CAST_SLOTS,))],
        compiler_params=_params(2),
        name="tail",
    )(x2p, kt, vb, x2s, cache_mem_k, cache_mem_v, gq, w_xq, w_xo, g2, w_ffn2_in, w_ffn2_out, gf)

    return (y_prompt,
            y_sample.reshape(bs, tdec, d),
            jnp.transpose(npool_p, (1, 0, 2))[None],
            nconv_p[None],
            mk,
            mv,
            jnp.transpose(npool_s, (1, 0, 2))[None],
            nconv_s.reshape(1, bs, CONV_HIST, D_CONV))
```

```python
import jax
import jax.numpy as jnp
from jax import lax
from jax.experimental import pallas as pl
from jax.experimental.pallas import tpu as pltpu

D_MODEL = 1024
D_POOL = 512
D_CONV = 512
POOL_WINDOWS = (2, 4, 8, 16)
POOL_GROUP = 128
POOL_HIST = 15
CONV_HIST = 2
D_FF = 2816
N_MEM = 256
N_XHEADS = 4
XHEAD_DIM = 256
EPS = 1e-6
PAST_LEN = 16384

FF_CHUNK = 512
FF_CHUNKS = tuple((s, min(FF_CHUNK, D_FF - s)) for s in range(0, D_FF, FF_CHUNK))
POOL_PAD = 16
CONV_PAD = 8
VMEM_LIMIT_BYTES = 60 * 1024 * 1024

CAST_SLOTS, CAST_ROWS, CAST_COLS = 8, 256, 1024
EDGE_ROW_PARTS = 2
KV_DMA_PRIORITY = 1
TM_PROMPT = 512

F32 = jnp.float32
BF16 = jnp.bfloat16


def _rms(x, g):
    y = x * lax.rsqrt(jnp.mean(x * x, axis=-1, keepdims=True) + EPS)
    return y * g


def _dot(a, b):
    return jnp.dot(a, b, preferred_element_type=F32)


def _dot_rows(a, b, parts):
    rows = a.shape[0] // parts
    return jnp.concatenate([_dot(a[i * rows:(i + 1) * rows], b) for i in range(parts)], axis=0)


def _ffn_half(x, g_ref, win_ref, wout_ref, act_ref):
    m = x.shape[0]
    h = _rms(x, g_ref[...]).astype(BF16)
    for i, (s0, n) in enumerate(FF_CHUNKS):
        parts = EDGE_ROW_PARTS if i == 0 else 1
        gate = _dot_rows(h, win_ref[:, s0:s0 + n], parts)
        up = _dot_rows(h, win_ref[:, D_FF + s0:D_FF + s0 + n], parts)
        silu = gate * (1.0 / (1.0 + jnp.exp(-gate)))
        act_ref[0:m, s0:s0 + n] = (silu * up).astype(BF16)
    y = _dot_rows(act_ref[0:m, :], wout_ref[...], EDGE_ROW_PARTS)
    return x + 0.5 * y


def _cast_weights_to_vmem(weights, stage_ref, sem):
    n_slots, cast_rows, cast_cols = stage_ref.shape
    blocks = []
    for w_hbm, dst in weights:
        rows, cols = dst.shape
        for r0 in range(0, rows, cast_rows):
            for c0 in range(0, cols, cast_cols):
                blocks.append((w_hbm, dst, r0, min(cast_rows, rows - r0), c0, min(cast_cols, cols - c0)))

    def copy(i):
        w_hbm, _, r0, nr, c0, nc = blocks[i]
        slot = i % n_slots
        return pltpu.make_async_copy(w_hbm.at[0, pl.ds(r0, nr), pl.ds(c0, nc)],
                                     stage_ref.at[slot, pl.ds(0, nr), pl.ds(0, nc)], sem.at[slot])

    for i in range(min(n_slots, len(blocks))):
        copy(i).start(priority=i % 2)
    for i, (_, dst, r0, nr, c0, nc) in enumerate(blocks):
        copy(i).wait()
        dst[r0:r0 + nr, c0:c0 + nc] = stage_ref[i % n_slots, 0:nr, 0:nc].astype(BF16)
        if i + n_slots < len(blocks):
            copy(i + n_slots).start(priority=(i + n_slots) % 2)


def _window_sum(x, w):
    s = x
    sh = 1
    while sh < w:
        s = s + pltpu.roll(s, sh, 0)
        sh *= 2
    return s


def _pool_delta_rows(ext, a, pos, w, extract):
    ssum = extract(_window_sum(ext, w))
    cnt = jnp.minimum(pos + 1, w).astype(F32)
    return ssum / cnt - a


def _pool_delta_steps(hist, new, w, pos0):
    n_hist = len(hist)
    sums = dict(enumerate(hist + new))
    sh = 1
    while sh < w:
        sums = {i: sums[i] + sums[i - sh] for i in sums if (i - sh) in sums}
        sh *= 2
    return [sums[n_hist + tt] / float(min(pos0 + tt + 1, w)) - new[tt] for tt in range(len(new))]


def _conv3(ext, convw_ref, extract):
    u0 = extract(pltpu.roll(ext, 2, 0))
    u1 = extract(pltpu.roll(ext, 1, 0))
    u2 = extract(ext)
    return convw_ref[0:1, :] * u0 + convw_ref[1:2, :] * u1 + convw_ref[2:3, :] * u2


def _memkv_copies(kv_scr, mk_hbm, mv_hbm, sem, b):
    copies = []
    for h in range(N_XHEADS):
        cols = pl.ds(h * XHEAD_DIM, XHEAD_DIM)
        copies.append(pltpu.make_async_copy(kv_scr.at[0, :, cols], mk_hbm.at[0, b, :, h, :], sem.at[0, h]))
        copies.append(pltpu.make_async_copy(kv_scr.at[1, :, cols], mv_hbm.at[0, b, :, h, :], sem.at[1, h]))
    return copies


def _mix_kernel(x_ref, xs_ref, hpool_hbm, hconv_ref, mem_ref, g1_ref, w1in_hbm, w1out_hbm, gmix_ref, wmixin_hbm,
                poolw_ref, pscale_ref, convw_ref, wmixout_hbm, gmem_ref, wk_hbm, wv_hbm,
                x2_ref, x2s_ref, npool_ref, nconv_ref, npool_s_hbm, nconv_s_ref, mk_hbm, mv_hbm, kt_ref, vb_ref,
                act_ref, epool_ref, econv_ref, a_ref, d_ref, econv_s_ref, mixed_ref,
                w1in_ref, w1out_ref, wmixin_ref, wmixout_ref, wk_ref, wv_ref, stage_ref, stage_sem, memkv_sem,
                hpool2_ref, npool2_s_ref, pool_sem):
    tm = x_ref.shape[0]
    ns = hpool2_ref.shape[2]
    t = xs_ref.shape[0] // ns
    crow = CONV_PAD + t
    b = pl.program_id(0)
    j = pl.program_id(1)
    step = b * pl.num_programs(1) + j
    n_steps = pl.num_programs(0) * pl.num_programs(1)
    slot = lax.rem(step, 2)

    def hist_in(st, sl):
        return pltpu.make_async_copy(hpool_hbm.at[:, pl.ds(st * ns, ns), :], hpool2_ref.at[sl], pool_sem.at[0, sl])

    def state_out(st, sl):
        return pltpu.make_async_copy(npool2_s_ref.at[sl], npool_s_hbm.at[:, pl.ds(st * ns, ns), :],
                                     pool_sem.at[1, sl])

    @pl.when(step == 0)
    def _():
        hist_in(0, 0).start()

    @pl.when(step >= 2)
    def _():
        state_out(step - 2, slot).wait()

    hist_in(step, slot).wait()

    @pl.when(step + 1 < n_steps)
    def _():
        hist_in(step + 1, 1 - slot).start()

    hpool_ref = hpool2_ref.at[slot]
    npool_s_ref = npool2_s_ref.at[slot]

    @pl.when(jnp.logical_and(b == 0, j == 0))
    def _():
        _cast_weights_to_vmem([(wk_hbm, wk_ref), (wv_hbm, wv_ref), (w1in_hbm, w1in_ref), (w1out_hbm, w1out_ref),
                               (wmixin_hbm, wmixin_ref), (wmixout_hbm, wmixout_ref)], stage_ref, stage_sem)

    @pl.when(jnp.logical_and(j == 0, b > 0))
    def _():
        for c in _memkv_copies(stage_ref, mk_hbm, mv_hbm, memkv_sem, b - 1):
            c.wait()

    @pl.when(j == 0)
    def _():
        epool_ref[0:POOL_PAD, :] = jnp.zeros((POOL_PAD, D_POOL), F32)
        econv_ref[0:CONV_PAD, :] = jnp.zeros((CONV_PAD, D_CONV), F32)
        mn = _rms(mem_ref[...], gmem_ref[...]).astype(BF16)
        k = _dot(mn, wk_ref[...])
        v = _dot(mn, wv_ref[...])
        stage_ref[0] = k
        stage_ref[1] = v
        kt_ref[...] = k.T.astype(BF16)
        vb_ref[...] = v.astype(BF16)
        for c in _memkv_copies(stage_ref, mk_hbm, mv_hbm, memkv_sem, b):
            c.start()

    x1 = _ffn_half(jnp.concatenate([x_ref[...], xs_ref[...]], axis=0), g1_ref, w1in_ref, w1out_ref, act_ref)
    z = _dot_rows(_rms(x1, gmix_ref[...]).astype(BF16), wmixin_ref[...], EDGE_ROW_PARTS)
    a = z[:, 0:D_POOL]
    cb = z[:, D_POOL:D_POOL + D_CONV]
    u = z[:, D_POOL + D_CONV:D_POOL + 2 * D_CONV] * z[:, D_POOL + 2 * D_CONV:]

    epool_ref[POOL_PAD:, :] = a[0:tm]
    pos = j * tm + lax.broadcasted_iota(jnp.int32, (tm, 1), 0)
    for gi, w in enumerate(POOL_WINDOWS):
        sl = slice(gi * POOL_GROUP, (gi + 1) * POOL_GROUP)
        d_prompt = _pool_delta_rows(epool_ref[:, sl], a[0:tm, sl], pos, w, lambda s: s[POOL_PAD:, :])
        a_ref[gi] = a[tm:, sl]
        hist = [hpool_ref[i, :, sl] for i in range(POOL_HIST)]
        new = [a_ref[gi, pl.ds(tt, ns, stride=t), :] for tt in range(t)]
        for r, slab in enumerate((hist + new)[-POOL_HIST:]):
            npool_s_ref[r, :, sl] = slab
        for tt, slab in enumerate(_pool_delta_steps(hist, new, w, PAST_LEN)):
            d_ref[gi, pl.ds(tt, ns, stride=t), :] = slab
        d_all = jnp.concatenate([d_prompt, d_ref[gi]], axis=0).astype(BF16)
        mixed_ref[:, sl] = (_dot(d_all, poolw_ref[gi].astype(BF16)) * pscale_ref[:, sl]).astype(BF16)

    econv_ref[CONV_PAD:, :] = u[0:tm]
    yc_prompt = _conv3(econv_ref[...], convw_ref, lambda s: s[CONV_PAD:, :])
    econv_s_ref[:, 0:CONV_PAD - CONV_HIST, :] = jnp.zeros((ns, CONV_PAD - CONV_HIST, D_CONV), F32)
    econv_s_ref[:, CONV_PAD - CONV_HIST:CONV_PAD, :] = hconv_ref[...]
    econv_s_ref[:, CONV_PAD:, :] = u[tm:].reshape(ns, t, D_CONV)
    yc_sample = _conv3(econv_s_ref[...].reshape(ns * crow, D_CONV), convw_ref,
                       lambda s: s.reshape(ns, crow, D_CONV)[:, CONV_PAD:, :].reshape(ns * t, D_CONV))
    mixed_ref[:, D_POOL:] = (cb * jnp.concatenate([yc_prompt, yc_sample], axis=0)).astype(BF16)

    x2 = x1 + _dot_rows(mixed_ref[...], wmixout_ref[...], EDGE_ROW_PARTS)
    x2_ref[...] = x2[0:tm]
    x2s_ref[...] = x2[tm:]
    nconv_s_ref[...] = econv_s_ref[:, crow - CONV_HIST:crow, :]
    state_out(step, slot).start()

    epool_ref[0:POOL_PAD, :] = epool_ref[tm:tm + POOL_PAD, :]
    econv_ref[0:CONV_PAD, :] = econv_ref[tm:tm + CONV_PAD, :]

    @pl.when(j == pl.num_programs(1) - 1)
    def _():
        for r in range(POOL_HIST):
            row = POOL_PAD - POOL_HIST + r
            npool_ref[r, pl.ds(b, 1), :] = epool_ref[row:row + 1, :]
        nconv_ref[...] = econv_ref[CONV_PAD - CONV_HIST:CONV_PAD, :]

    @pl.when(jnp.logical_and(b == pl.num_programs(0) - 1, j == pl.num_programs(1) - 1))
    def _():
        for c in _memkv_copies(stage_ref, mk_hbm, mv_hbm, memkv_sem, b):
            c.wait()
        state_out(step - 1, 1 - slot).wait()
        state_out(step, slot).wait()


def _softmax(s):
    e = jnp.exp(s - jnp.max(s, axis=-1, keepdims=True))
    return e / jnp.sum(e, axis=-1, keepdims=True)


def _queries(x, gq_ref, wq_ref):
    return _dot_rows(_rms(x, gq_ref[...]).astype(BF16), wq_ref[...], EDGE_ROW_PARTS) * (XHEAD_DIM ** -0.5)


def _kv_copies(k_hbm, v_hbm, kv_ref, sem, step, slot):
    ns = kv_ref.shape[1] // 2
    seqs = pl.ds(step * ns, ns)
    copies = []
    for h in range(N_XHEADS):
        cols = pl.ds(h * XHEAD_DIM, XHEAD_DIM)
        copies.append(pltpu.make_async_copy(k_hbm.at[0, seqs, :, h, :], kv_ref.at[slot, pl.ds(0, ns), :, cols],
                                            sem.at[slot, 0, h]))
        copies.append(pltpu.make_async_copy(v_hbm.at[0, seqs, :, h, :], kv_ref.at[slot, pl.ds(ns, ns), :, cols],
                                            sem.at[slot, 1, h]))
    return copies


def _cached_scores(q, kv_ref):
    ns = kv_ref.shape[0] // 2
    t = q.shape[0] // ns
    d = q.shape[1]
    rows_per_seq = N_XHEADS * t
    row_head = lax.broadcasted_iota(jnp.int32, (rows_per_seq, d), 0) // t
    col_head = lax.broadcasted_iota(jnp.int32, (rows_per_seq, d), 1) // XHEAD_DIM
    own_head = row_head == col_head
    scores = []
    for s in range(ns):
        qe = jnp.where(own_head, jnp.concatenate([q[s * t:(s + 1) * t, :]] * N_XHEADS, axis=0), 0.0).astype(BF16)
        scores.append(lax.dot_general(qe, kv_ref[s].astype(BF16), (((1,), (1,)), ((), ())),
                                      preferred_element_type=F32))
    return jnp.concatenate(scores, axis=0)


def _cached_outputs(scores, kv_ref, o_ref):
    ns = kv_ref.shape[0] // 2
    t = o_ref.shape[0] // ns
    rows_per_seq = N_XHEADS * t
    p = _softmax(scores).astype(BF16)
    for s in range(ns):
        oa = _dot(p[s * rows_per_seq:(s + 1) * rows_per_seq, :], kv_ref[ns + s].astype(BF16))
        for h in range(N_XHEADS):
            sl = slice(h * XHEAD_DIM, (h + 1) * XHEAD_DIM)
            o_ref[s * t:(s + 1) * t, sl] = oa[h * t:(h + 1) * t, sl]


def _tail_kernel(x_ref, kt_ref, v_ref, x2s_ref, ck_hbm, cv_hbm, gq_ref, wq_hbm, wo_hbm, g2_ref,
                 w2in_hbm, w2out_hbm, gf_ref,
                 y_ref, ys_ref,
                 act_ref, o_ref, os_ref, wq_ref, wo_ref, w2in_ref, w2out_ref, kv_ref, kv_sem, stage_sem):
    step = pl.program_id(0) * pl.num_programs(1) + pl.program_id(1)
    n_steps = pl.num_programs(0) * pl.num_programs(1)
    tm = x_ref.shape[0]
    slot = lax.rem(step, 2)

    @pl.when(step == 0)
    def _():
        _cast_weights_to_vmem([(wq_hbm, wq_ref), (wo_hbm, wo_ref), (w2in_hbm, w2in_ref),
                               (w2out_hbm, w2out_ref)], kv_ref.at[0], stage_sem)
        for c in _kv_copies(ck_hbm, cv_hbm, kv_ref, kv_sem, 0, 0):
            c.start(priority=KV_DMA_PRIORITY)

    for c in _kv_copies(ck_hbm, cv_hbm, kv_ref, kv_sem, step, slot):
        c.wait()

    @pl.when(step + 1 < n_steps)
    def _():
        for c in _kv_copies(ck_hbm, cv_hbm, kv_ref, kv_sem, step + 1, 1 - slot):
            c.start(priority=KV_DMA_PRIORITY)

    kv = kv_ref.at[slot]
    x2 = jnp.concatenate([x_ref[...], x2s_ref[...]], axis=0)
    q_all = _queries(x2, gq_ref, wq_ref)
    q = q_all[0:tm].astype(BF16)
    heads = [slice(h * XHEAD_DIM, (h + 1) * XHEAD_DIM) for h in range(N_XHEADS)]
    scores = [_dot(q[:, heads[h]], kt_ref[heads[h], :]) for h in range(N_XHEADS)]
    sample_scores = _cached_scores(q_all[tm:], kv)
    for h in range(N_XHEADS):
        p = _softmax(scores[h]).astype(BF16)
        o_ref[0:tm, heads[h]] = _dot(p, v_ref[:, heads[h]]).astype(BF16)
        if h == 1:
            _cached_outputs(sample_scores, kv, os_ref)
    o_ref[tm:, :] = os_ref[...].astype(BF16)
    x3 = x2 + _dot_rows(o_ref[...], wo_ref[...], EDGE_ROW_PARTS)
    x4 = _ffn_half(x3, g2_ref, w2in_ref, w2out_ref, act_ref)
    y = _rms(x4, gf_ref[...])
    y_ref[...] = y[0:tm]
    ys_ref[...] = y[tm:]


def _resident(arr):
    nd = arr.ndim
    return pl.BlockSpec(arr.shape, lambda *_: (0,) * nd, pipeline_mode=pl.Buffered(1))


def _params(n_grid):
    return pltpu.CompilerParams(dimension_semantics=("arbitrary",) * n_grid,
                                vmem_limit_bytes=VMEM_LIMIT_BYTES)


def kernel(x_prompt, x_sample, mem_prompt, state_pool, state_conv, cache_mem_k, cache_mem_v, g_ffn1, w_ffn1_in, w_ffn1_out, g_mix, w_mix_in, pool_w, pool_scale, conv_w, w_mix_out, g_xq, g_mem, w_xq, w_xk, w_xv, w_xo, g_ffn2, w_ffn2_in, w_ffn2_out, g_final):
    depth = g_ffn1.shape[0]
    assert depth == 1
    bp, seq, d = x_prompt.shape
    bs, tdec, _ = x_sample.shape
    l = 0

    poolw = pool_w[l]
    any_space = pl.BlockSpec(memory_space=pl.ANY)
    stage_shape = (CAST_SLOTS, CAST_ROWS, CAST_COLS)
    g1, gmix, gq, gmem, g2 = g_ffn1[l:l + 1], g_mix[l:l + 1], g_xq[l:l + 1], g_mem[l:l + 1], g_ffn2[l:l + 1]
    gf = g_final.reshape(1, d)
    pscale = pool_scale[l:l + 1]
    convw = conv_w[l]

    tm = TM_PROMPT
    nj = seq // tm
    n_steps = bp * nj
    seqs_per_step = bs // n_steps
    assert seqs_per_step * n_steps == bs
    srows = seqs_per_step * tdec
    xs = x_sample.reshape(bs * tdec, d)
    hist_pool = jnp.transpose(state_pool[l], (1, 0, 2))
    hist_conv = state_conv[l].reshape(n_steps, seqs_per_step, CONV_HIST, D_CONV)
    tile_rows = pl.BlockSpec((None, tm, d), lambda b, j: (b, j, 0))
    step_rows = pl.BlockSpec((srows, d), lambda b, j: (b * nj + j, 0))
    step_conv = pl.BlockSpec((None, seqs_per_step, CONV_HIST, D_CONV), lambda b, j: (b * nj + j, 0, 0, 0))
    mix_cast_shapes = [(d, 2 * D_FF), (D_FF, d), (d, D_POOL + 3 * D_CONV), (D_POOL + D_CONV, d), (d, d), (d, d)]
    batch_mem = pl.BlockSpec((None, N_MEM, d), lambda b, j: (b, 0, 0))
    x2p, x2s, npool_p, nconv_p, npool_s, nconv_s, mk, mv, kt, vb = pl.pallas_call(
        _mix_kernel,
        grid=(bp, nj),
        in_specs=[tile_rows, step_rows, any_space, step_conv, batch_mem,
                  _resident(g1), any_space, any_space, _resident(gmix), any_space, _resident(poolw),
                  _resident(pscale), _resident(convw), any_space, _resident(gmem), any_space, any_space],
        out_specs=[tile_rows, step_rows,
                   pl.BlockSpec((POOL_HIST, bp, D_POOL), lambda b, j: (0, 0, 0)),
                   pl.BlockSpec((None, CONV_HIST, D_CONV), lambda b, j: (b, 0, 0)),
                   any_space, step_conv, any_space, any_space,
                   pl.BlockSpec((None, d, N_MEM), lambda b, j: (b, 0, 0)),
                   pl.BlockSpec((None, N_MEM, d), lambda b, j: (b, 0, 0))],
        out_shape=[jax.ShapeDtypeStruct((bp, seq, d), F32),
                   jax.ShapeDtypeStruct((bs * tdec, d), F32),
                   jax.ShapeDtypeStruct((POOL_HIST, bp, D_POOL), F32),
                   jax.ShapeDtypeStruct((bp, CONV_HIST, D_CONV), F32),
                   jax.ShapeDtypeStruct((POOL_HIST, bs, D_POOL), F32),
                   jax.ShapeDtypeStruct((n_steps, seqs_per_step, CONV_HIST, D_CONV), F32),
                   jax.ShapeDtypeStruct((1, bp, N_MEM, N_XHEADS, XHEAD_DIM), F32),
                   jax.ShapeDtypeStruct((1, bp, N_MEM, N_XHEADS, XHEAD_DIM), F32),
                   jax.ShapeDtypeStruct((bp, d, N_MEM), BF16),
                   jax.ShapeDtypeStruct((bp, N_MEM, d), BF16)],
        scratch_shapes=[pltpu.VMEM((tm + srows, D_FF), BF16),
                        pltpu.VMEM((tm + POOL_PAD, D_POOL), F32),
                        pltpu.VMEM((tm + CONV_PAD, D_CONV), F32),
                        pltpu.VMEM((len(POOL_WINDOWS), srows, POOL_GROUP), F32),
                        pltpu.VMEM((len(POOL_WINDOWS), srows, POOL_GROUP), F32),
                        pltpu.VMEM((seqs_per_step, CONV_PAD + tdec, D_CONV), F32),
                        pltpu.VMEM((tm + srows, d), BF16)]
                       + [pltpu.VMEM(s, BF16) for s in mix_cast_shapes]
                       + [pltpu.VMEM(stage_shape, F32),
                          pltpu.SemaphoreType.DMA((CAST_SLOTS,)),
                          pltpu.SemaphoreType.DMA((2, N_XHEADS)),
                          pltpu.VMEM((2, POOL_HIST, seqs_per_step, D_POOL), F32),
                          pltpu.VMEM((2, POOL_HIST, seqs_per_step, D_POOL), F32),
                          pltpu.SemaphoreType.DMA((2, 2))],
        compiler_params=_params(2),
        name="mix",
    )(x_prompt, xs, hist_pool, hist_conv, mem_prompt, g1, w_ffn1_in, w_ffn1_out, gmix, w_mix_in, poolw, pscale,
      convw, w_mix_out, gmem, w_xk, w_xv)

    assert (2 * seqs_per_step, N_MEM, d) == stage_shape
    tail_cast_shapes = [(d, d), (d, d), (d, 2 * D_FF), (D_FF, d)]
    y_prompt, y_sample = pl.pallas_call(
        _tail_kernel,
        grid=(bp, nj),
        in_specs=[tile_rows,
                  pl.BlockSpec((None, d, N_MEM), lambda b, j: (b, 0, 0)),
                  pl.BlockSpec((None, N_MEM, d), lambda b, j: (b, 0, 0)),
                  step_rows,
                  any_space, any_space,
                  _resident(gq), any_space, any_space, _resident(g2), any_space, any_space, _resident(gf)],
        out_specs=[tile_rows, step_rows],
        out_shape=[jax.ShapeDtypeStruct((bp, seq, d), F32),
                   jax.ShapeDtypeStruct((bs * tdec, d), F32)],
        scratch_shapes=[pltpu.VMEM((tm + srows, D_FF), BF16),
                        pltpu.VMEM((tm + srows, d), BF16),
                        pltpu.VMEM((srows, d), F32)]
                       + [pltpu.VMEM(s, BF16) for s in tail_cast_shapes]
                       + [pltpu.VMEM((2,) + stage_shape, F32),
                          pltpu.SemaphoreType.DMA((2, 2, N_XHEADS)),
                          pltpu.SemaphoreType.DMA((CAST_SLOTS,))],
        compiler_params=_params(2),
        name="tail",
    )(x2p, kt, vb, x2s, cache_mem_k, cache_mem_v, gq, w_xq, w_xo, g2, w_ffn2_in, w_ffn2_out, gf)

    return (y_prompt,
            y_sample.reshape(bs, tdec, d),
            jnp.transpose(npool_p, (1, 0, 2))[None],
            nconv_p[None],
            mk,
            mv,
            jnp.transpose(npool_s, (1, 0, 2))[None],
            nconv_s.reshape(1, bs, CONV_HIST, D_CONV))
```

```python
import jax
import jax.numpy as jnp
from jax import lax
from jax.experimental import pallas as pl
from jax.experimental.pallas import tpu as pltpu

D_MODEL = 1024
D_POOL = 512
D_CONV = 512
POOL_WINDOWS = (2, 4, 8, 16)
POOL_GROUP = 128
POOL_HIST = 15
CONV_HIST = 2
D_FF = 2816
N_MEM = 256
N_XHEADS = 4
XHEAD_DIM = 256
EPS = 1e-6
PAST_LEN = 16384

FF_CHUNK = 512
FF_CHUNKS = tuple((s, min(FF_CHUNK, D_FF - s)) for s in range(0, D_FF, FF_CHUNK))
POOL_PAD = 16
CONV_PAD = 8
VMEM_LIMIT_BYTES = 62 * 1024 * 1024

CAST_SLOTS, CAST_ROWS, CAST_COLS = 8, 256, 1024
N_KV_BLOCKS = 8
EDGE_ROW_PARTS = 2
TM_PROMPT = 512

F32 = jnp.float32
BF16 = jnp.bfloat16


def _rms(x, g):
    y = x * lax.rsqrt(jnp.mean(x * x, axis=-1, keepdims=True) + EPS)
    return y * g


def _dot(a, b):
    return jnp.dot(a, b, preferred_element_type=F32)


def _dot_rows(a, b, parts):
    rows = a.shape[0] // parts
    return jnp.concatenate([_dot(a[i * rows:(i + 1) * rows], b) for i in range(parts)], axis=0)


def _ffn_half(x, g_ref, win_ref, wout_ref, act_ref):
    m = x.shape[0]
    h = _rms(x, g_ref[...]).astype(BF16)
    for i, (s0, n) in enumerate(FF_CHUNKS):
        parts = EDGE_ROW_PARTS if i == 0 else 1
        gate = _dot_rows(h, win_ref[:, s0:s0 + n], parts)
        up = _dot_rows(h, win_ref[:, D_FF + s0:D_FF + s0 + n], parts)
        silu = gate * (1.0 / (1.0 + jnp.exp(-gate)))
        act_ref[0:m, s0:s0 + n] = (silu * up).astype(BF16)
    y = _dot_rows(act_ref[0:m, :], wout_ref[...], EDGE_ROW_PARTS)
    return x + 0.5 * y


def _cast_weights_to_vmem(weights, stage_ref, sem, first_slot=0, after_blocks=None):
    _, cast_rows, cast_cols = stage_ref.shape
    n_slots = stage_ref.shape[0] - first_slot
    blocks = []
    for w_hbm, dst in weights:
        rows, cols = dst.shape
        for r0 in range(0, rows, cast_rows):
            for c0 in range(0, cols, cast_cols):
                blocks.append((w_hbm, dst, r0, min(cast_rows, rows - r0), c0, min(cast_cols, cols - c0)))

    def copy(i):
        w_hbm, _, r0, nr, c0, nc = blocks[i]
        slot = first_slot + i % n_slots
        return pltpu.make_async_copy(w_hbm.at[0, pl.ds(r0, nr), pl.ds(c0, nc)],
                                     stage_ref.at[slot, pl.ds(0, nr), pl.ds(0, nc)], sem.at[slot])

    for i in range(min(n_slots, len(blocks))):
        copy(i).start()
    for i, (_, dst, r0, nr, c0, nc) in enumerate(blocks):
        if after_blocks and i in after_blocks:
            after_blocks[i]()
        copy(i).wait()
        dst[r0:r0 + nr, c0:c0 + nc] = stage_ref[first_slot + i % n_slots, 0:nr, 0:nc].astype(BF16)
        if i + n_slots < len(blocks):
            copy(i + n_slots).start()


def _window_sum(x, w):
    s = x
    sh = 1
    while sh < w:
        s = s + pltpu.roll(s, sh, 0)
        sh *= 2
    return s


def _pool_delta_rows(ext, a, pos, w, extract):
    ssum = extract(_window_sum(ext, w))
    cnt = jnp.minimum(pos + 1, w).astype(F32)
    return ssum / cnt - a


def _pool_delta_steps(hist, new, w, pos0):
    n_hist = len(hist)
    sums = dict(enumerate(hist + new))
    sh = 1
    while sh < w:
        sums = {i: sums[i] + sums[i - sh] for i in sums if (i - sh) in sums}
        sh *= 2
    return [sums[n_hist + tt] / float(min(pos0 + tt + 1, w)) - new[tt] for tt in range(len(new))]


def _conv3(ext, convw_ref, extract):
    u0 = extract(pltpu.roll(ext, 2, 0))
    u1 = extract(pltpu.roll(ext, 1, 0))
    u2 = extract(ext)
    return convw_ref[0:1, :] * u0 + convw_ref[1:2, :] * u1 + convw_ref[2:3, :] * u2


def _memkv_copies(kv_scr, mk_hbm, mv_hbm, sem, b):
    copies = []
    for h in range(N_XHEADS):
        cols = pl.ds(h * XHEAD_DIM, XHEAD_DIM)
        copies.append(pltpu.make_async_copy(kv_scr.at[0, :, cols], mk_hbm.at[0, b, :, h, :], sem.at[0, h]))
        copies.append(pltpu.make_async_copy(kv_scr.at[1, :, cols], mv_hbm.at[0, b, :, h, :], sem.at[1, h]))
    return copies


def _mix_kernel(x_ref, xs_ref, hpool_hbm, hconv_ref, mem_ref, g1_ref, w1in_hbm, w1out_hbm, gmix_ref, wmixin_hbm,
                poolw_ref, pscale_ref, convw_ref, wmixout_hbm, gmem_ref, wk_hbm, wv_hbm,
                x2_ref, x2s_ref, npool_ref, nconv_ref, npool_s_hbm, nconv_s_ref, mk_hbm, mv_hbm, kt_ref, vb_ref,
                act_ref, epool_ref, econv_ref, a_ref, d_ref, econv_s_ref, mixed_ref,
                w1in_ref, w1out_ref, wmixin_ref, wmixout_ref, wk_ref, wv_ref, stage_ref, stage_sem, memkv_sem,
                hpool2_ref, npool2_s_ref, pool_sem):
    tm = x_ref.shape[0]
    ns = hpool2_ref.shape[2]
    t = xs_ref.shape[0] // ns
    crow = CONV_PAD + t
    b = pl.program_id(0)
    j = pl.program_id(1)
    step = b * pl.num_programs(1) + j
    n_steps = pl.num_programs(0) * pl.num_programs(1)
    slot = lax.rem(step, 2)

    def hist_in(st, sl):
        return pltpu.make_async_copy(hpool_hbm.at[:, pl.ds(st * ns, ns), :], hpool2_ref.at[sl], pool_sem.at[0, sl])

    def state_out(st, sl):
        return pltpu.make_async_copy(npool2_s_ref.at[sl], npool_s_hbm.at[:, pl.ds(st * ns, ns), :],
                                     pool_sem.at[1, sl])

    @pl.when(step == 0)
    def _():
        hist_in(0, 0).start()

    @pl.when(step >= 2)
    def _():
        state_out(step - 2, slot).wait()

    hist_in(step, slot).wait()

    @pl.when(step + 1 < n_steps)
    def _():
        hist_in(step + 1, 1 - slot).start()

    hpool_ref = hpool2_ref.at[slot]
    npool_s_ref = npool2_s_ref.at[slot]

    def project_memory():
        mn = _rms(mem_ref[...], gmem_ref[...]).astype(BF16)
        k = _dot(mn, wk_ref[...])
        v = _dot(mn, wv_ref[...])
        stage_ref[0] = k
        stage_ref[1] = v
        kt_ref[...] = k.T.astype(BF16)
        vb_ref[...] = v.astype(BF16)
        for c in _memkv_copies(stage_ref, mk_hbm, mv_hbm, memkv_sem, b):
            c.start()

    @pl.when(jnp.logical_and(b == 0, j == 0))
    def _():
        _cast_weights_to_vmem([(wk_hbm, wk_ref), (wv_hbm, wv_ref), (w1in_hbm, w1in_ref), (w1out_hbm, w1out_ref),
                               (wmixin_hbm, wmixin_ref), (wmixout_hbm, wmixout_ref)], stage_ref, stage_sem,
                              first_slot=2, after_blocks={N_KV_BLOCKS: project_memory})

    @pl.when(j == 0)
    def _():
        epool_ref[0:POOL_PAD, :] = jnp.zeros((POOL_PAD, D_POOL), F32)
        econv_ref[0:CONV_PAD, :] = jnp.zeros((CONV_PAD, D_CONV), F32)

    @pl.when(jnp.logical_and(j == 0, b > 0))
    def _():
        for c in _memkv_copies(stage_ref, mk_hbm, mv_hbm, memkv_sem, b - 1):
            c.wait()
        project_memory()

    x1 = _ffn_half(jnp.concatenate([x_ref[...], xs_ref[...]], axis=0), g1_ref, w1in_ref, w1out_ref, act_ref)
    z = _dot_rows(_rms(x1, gmix_ref[...]).astype(BF16), wmixin_ref[...], EDGE_ROW_PARTS)
    a = z[:, 0:D_POOL]
    cb = z[:, D_POOL:D_POOL + D_CONV]
    u = z[:, D_POOL + D_CONV:D_POOL + 2 * D_CONV] * z[:, D_POOL + 2 * D_CONV:]

    epool_ref[POOL_PAD:, :] = a[0:tm]
    pos = j * tm + lax.broadcasted_iota(jnp.int32, (tm, 1), 0)
    for gi, w in enumerate(POOL_WINDOWS):
        sl = slice(gi * POOL_GROUP, (gi + 1) * POOL_GROUP)
        d_prompt = _pool_delta_rows(epool_ref[:, sl], a[0:tm, sl], pos, w, lambda s: s[POOL_PAD:, :])
        a_ref[gi] = a[tm:, sl]
        hist = [hpool_ref[i, :, sl] for i in range(POOL_HIST)]
        new = [a_ref[gi, pl.ds(tt, ns, stride=t), :] for tt in range(t)]
        for r, slab in enumerate((hist + new)[-POOL_HIST:]):
            npool_s_ref[r, :, sl] = slab
        for tt, slab in enumerate(_pool_delta_steps(hist, new, w, PAST_LEN)):
            d_ref[gi, pl.ds(tt, ns, stride=t), :] = slab
        d_all = jnp.concatenate([d_prompt, d_ref[gi]], axis=0).astype(BF16)
        mixed_ref[:, sl] = (_dot(d_all, poolw_ref[gi].astype(BF16)) * pscale_ref[:, sl]).astype(BF16)

    econv_ref[CONV_PAD:, :] = u[0:tm]
    yc_prompt = _conv3(econv_ref[...], convw_ref, lambda s: s[CONV_PAD:, :])
    econv_s_ref[:, 0:CONV_PAD - CONV_HIST, :] = jnp.zeros((ns, CONV_PAD - CONV_HIST, D_CONV), F32)
    econv_s_ref[:, CONV_PAD - CONV_HIST:CONV_PAD, :] = hconv_ref[...]
    econv_s_ref[:, CONV_PAD:, :] = u[tm:].reshape(ns, t, D_CONV)
    yc_sample = _conv3(econv_s_ref[...].reshape(ns * crow, D_CONV), convw_ref,
                       lambda s: s.reshape(ns, crow, D_CONV)[:, CONV_PAD:, :].reshape(ns * t, D_CONV))
    mixed_ref[:, D_POOL:] = (cb * jnp.concatenate([yc_prompt, yc_sample], axis=0)).astype(BF16)

    x2 = x1 + _dot_rows(mixed_ref[...], wmixout_ref[...], EDGE_ROW_PARTS)
    x2_ref[...] = x2[0:tm]
    x2s_ref[...] = x2[tm:]
    nconv_s_ref[...] = econv_s_ref[:, crow - CONV_HIST:crow, :]
    state_out(step, slot).start()

    epool_ref[0:POOL_PAD, :] = epool_ref[tm:tm + POOL_PAD, :]
    econv_ref[0:CONV_PAD, :] = econv_ref[tm:tm + CONV_PAD, :]

    @pl.when(j == pl.num_programs(1) - 1)
    def _():
        for r in range(POOL_HIST):
            row = POOL_PAD - POOL_HIST + r
            npool_ref[r, pl.ds(b, 1), :] = epool_ref[row:row + 1, :]
        nconv_ref[...] = econv_ref[CONV_PAD - CONV_HIST:CONV_PAD, :]

    @pl.when(jnp.logical_and(b == pl.num_programs(0) - 1, j == pl.num_programs(1) - 1))
    def _():
        for c in _memkv_copies(stage_ref, mk_hbm, mv_hbm, memkv_sem, b):
            c.wait()
        state_out(step - 1, 1 - slot).wait()
        state_out(step, slot).wait()


def _softmax(s):
    e = jnp.exp(s - jnp.max(s, axis=-1, keepdims=True))
    return e / jnp.sum(e, axis=-1, keepdims=True)


def _queries(x, gq_ref, wq_ref):
    return _dot_rows(_rms(x, gq_ref[...]).astype(BF16), wq_ref[...], EDGE_ROW_PARTS) * (XHEAD_DIM ** -0.5)


def _kv_copies(k_hbm, v_hbm, kv_ref, sem, step, slot):
    ns = kv_ref.shape[1] // 2
    seqs = pl.ds(step * ns, ns)
    copies = []
    for h in range(N_XHEADS):
        cols = pl.ds(h * XHEAD_DIM, XHEAD_DIM)
        copies.append(pltpu.make_async_copy(k_hbm.at[0, seqs, :, h, :], kv_ref.at[slot, pl.ds(0, ns), :, cols],
                                            sem.at[slot, 0, h]))
        copies.append(pltpu.make_async_copy(v_hbm.at[0, seqs, :, h, :], kv_ref.at[slot, pl.ds(ns, ns), :, cols],
                                            sem.at[slot, 1, h]))
    return copies


def _cached_scores(q, kv_ref):
    ns = kv_ref.shape[0] // 2
    t = q.shape[0] // ns
    d = q.shape[1]
    rows_per_seq = N_XHEADS * t
    row_head = lax.broadcasted_iota(jnp.int32, (rows_per_seq, d), 0) // t
    col_head = lax.broadcasted_iota(jnp.int32, (rows_per_seq, d), 1) // XHEAD_DIM
    own_head = row_head == col_head
    scores = []
    for s in range(ns):
        qe = jnp.where(own_head, jnp.concatenate([q[s * t:(s + 1) * t, :]] * N_XHEADS, axis=0), 0.0).astype(BF16)
        scores.append(lax.dot_general(qe, kv_ref[s].astype(BF16), (((1,), (1,)), ((), ())),
                                      preferred_element_type=F32))
    return jnp.concatenate(scores, axis=0)


def _cached_outputs(scores, kv_ref, o_ref):
    ns = kv_ref.shape[0] // 2
    t = o_ref.shape[0] // ns
    rows_per_seq = N_XHEADS * t
    p = _softmax(scores).astype(BF16)
    for s in range(ns):
        oa = _dot(p[s * rows_per_seq:(s + 1) * rows_per_seq, :], kv_ref[ns + s].astype(BF16))
        for h in range(N_XHEADS):
            sl = slice(h * XHEAD_DIM, (h + 1) * XHEAD_DIM)
            o_ref[s * t:(s + 1) * t, sl] = oa[h * t:(h + 1) * t, sl]


def _tail_kernel(x_ref, kt_ref, v_ref, x2s_ref, ck_hbm, cv_hbm, gq_ref, wq_hbm, wo_hbm, g2_ref,
                 w2in_hbm, w2out_hbm, gf_ref,
                 y_ref, ys_ref,
                 act_ref, o_ref, os_ref, wq_ref, wo_ref, w2in_ref, w2out_ref, kv_ref, kv_sem, stage_sem):
    step = pl.program_id(0) * pl.num_programs(1) + pl.program_id(1)
    n_steps = pl.num_programs(0) * pl.num_programs(1)
    tm = x_ref.shape[0]
    slot = lax.rem(step, 2)

    @pl.when(step == 0)
    def _():
        _cast_weights_to_vmem([(wq_hbm, wq_ref), (wo_hbm, wo_ref), (w2in_hbm, w2in_ref),
                               (w2out_hbm, w2out_ref)], kv_ref.at[0], stage_sem)
        for c in _kv_copies(ck_hbm, cv_hbm, kv_ref, kv_sem, 0, 0):
            c.start()

    for c in _kv_copies(ck_hbm, cv_hbm, kv_ref, kv_sem, step, slot):
        c.wait()

    @pl.when(step + 1 < n_steps)
    def _():
        for c in _kv_copies(ck_hbm, cv_hbm, kv_ref, kv_sem, step + 1, 1 - slot):
            c.start()

    kv = kv_ref.at[slot]
    x2 = jnp.concatenate([x_ref[...], x2s_ref[...]], axis=0)
    q_all = _queries(x2, gq_ref, wq_ref)
    q = q_all[0:tm].astype(BF16)
    heads = [slice(h * XHEAD_DIM, (h + 1) * XHEAD_DIM) for h in range(N_XHEADS)]
    scores = [_dot(q[:, heads[h]], kt_ref[heads[h], :]) for h in range(N_XHEADS)]
    sample_scores = _cached_scores(q_all[tm:], kv)
    for h in range(N_XHEADS):
        p = _softmax(scores[h]).astype(BF16)
        o_ref[0:tm, heads[h]] = _dot(p, v_ref[:, heads[h]]).astype(BF16)
        if h == 1:
            _cached_outputs(sample_scores, kv, os_ref)
    o_ref[tm:, :] = os_ref[...].astype(BF16)
    x3 = x2 + _dot_rows(o_ref[...], wo_ref[...], EDGE_ROW_PARTS)
    x4 = _ffn_half(x3, g2_ref, w2in_ref, w2out_ref, act_ref)
    y = _rms(x4, gf_ref[...])
    y_ref[...] = y[0:tm]
    ys_ref[...] = y[tm:]


def _resident(arr):
    nd = arr.ndim
    return pl.BlockSpec(arr.shape, lambda *_: (0,) * nd, pipeline_mode=pl.Buffered(1))


def _params(n_grid):
    return pltpu.CompilerParams(dimension_semantics=("arbitrary",) * n_grid,
                                vmem_limit_bytes=VMEM_LIMIT_BYTES)


def kernel(x_prompt, x_sample, mem_prompt, state_pool, state_conv, cache_mem_k, cache_mem_v, g_ffn1, w_ffn1_in, w_ffn1_out, g_mix, w_mix_in, pool_w, pool_scale, conv_w, w_mix_out, g_xq, g_mem, w_xq, w_xk, w_xv, w_xo, g_ffn2, w_ffn2_in, w_ffn2_out, g_final):
    depth = g_ffn1.shape[0]
    assert depth == 1
    bp, seq, d = x_prompt.shape
    bs, tdec, _ = x_sample.shape
    l = 0

    poolw = pool_w[l]
    any_space = pl.BlockSpec(memory_space=pl.ANY)
    stage_shape = (CAST_SLOTS, CAST_ROWS, CAST_COLS)
    g1, gmix, gq, gmem, g2 = g_ffn1[l:l + 1], g_mix[l:l + 1], g_xq[l:l + 1], g_mem[l:l + 1], g_ffn2[l:l + 1]
    gf = g_final.reshape(1, d)
    pscale = pool_scale[l:l + 1]
    convw = conv_w[l]

    tm = TM_PROMPT
    nj = seq // tm
    n_steps = bp * nj
    seqs_per_step = bs // n_steps
    assert seqs_per_step * n_steps == bs
    srows = seqs_per_step * tdec
    xs = x_sample.reshape(bs * tdec, d)
    hist_pool = jnp.transpose(state_pool[l], (1, 0, 2))
    hist_conv = state_conv[l].reshape(n_steps, seqs_per_step, CONV_HIST, D_CONV)
    tile_rows = pl.BlockSpec((None, tm, d), lambda b, j: (b, j, 0))
    step_rows = pl.BlockSpec((srows, d), lambda b, j: (b * nj + j, 0))
    step_conv = pl.BlockSpec((None, seqs_per_step, CONV_HIST, D_CONV), lambda b, j: (b * nj + j, 0, 0, 0))
    mix_cast_shapes = [(d, 2 * D_FF), (D_FF, d), (d, D_POOL + 3 * D_CONV), (D_POOL + D_CONV, d), (d, d), (d, d)]
    batch_mem = pl.BlockSpec((None, N_MEM, d), lambda b, j: (b, 0, 0))
    x2p, x2s, npool_p, nconv_p, npool_s, nconv_s, mk, mv, kt, vb = pl.pallas_call(
        _mix_kernel,
        grid=(bp, nj),
        in_specs=[tile_rows, step_rows, any_space, step_conv, batch_mem,
                  _resident(g1), any_space, any_space, _resident(gmix), any_space, _resident(poolw),
                  _resident(pscale), _resident(convw), any_space, _resident(gmem), any_space, any_space],
        out_specs=[tile_rows, step_rows,
                   pl.BlockSpec((POOL_HIST, bp, D_POOL), lambda b, j: (0, 0, 0)),
                   pl.BlockSpec((None, CONV_HIST, D_CONV), lambda b, j: (b, 0, 0)),
                   any_space, step_conv, any_space, any_space,
                   pl.BlockSpec((None, d, N_MEM), lambda b, j: (b, 0, 0)),
                   pl.BlockSpec((None, N_MEM, d), lambda b, j: (b, 0, 0))],
        out_shape=[jax.ShapeDtypeStruct((bp, seq, d), F32),
                   jax.ShapeDtypeStruct((bs * tdec, d), F32),
                   jax.ShapeDtypeStruct((POOL_HIST, bp, D_POOL), F32),
                   jax.ShapeDtypeStruct((bp, CONV_HIST, D_CONV), F32),
                   jax.ShapeDtypeStruct((POOL_HIST, bs, D_POOL), F32),
                   jax.ShapeDtypeStruct((n_steps, seqs_per_step, CONV_HIST, D_CONV), F32),
                   jax.ShapeDtypeStruct((1, bp, N_MEM, N_XHEADS, XHEAD_DIM), F32),
                   jax.ShapeDtypeStruct((1, bp, N_MEM, N_XHEADS, XHEAD_DIM), F32),
                   jax.ShapeDtypeStruct((bp, d, N_MEM), BF16),
                   jax.ShapeDtypeStruct((bp, N_MEM, d), BF16)],
        scratch_shapes=[pltpu.VMEM((tm + srows, D_FF), BF16),
                        pltpu.VMEM((tm + POOL_PAD, D_POOL), F32),
                        pltpu.VMEM((tm + CONV_PAD, D_CONV), F32),
                        pltpu.VMEM((len(POOL_WINDOWS), srows, POOL_GROUP), F32),
                        pltpu.VMEM((len(POOL_WINDOWS), srows, POOL_GROUP), F32),
                        pltpu.VMEM((seqs_per_step, CONV_PAD + tdec, D_CONV), F32),
                        pltpu.VMEM((tm + srows, d), BF16)]
                       + [pltpu.VMEM(s, BF16) for s in mix_cast_shapes]
                       + [pltpu.VMEM(stage_shape, F32),
                          pltpu.SemaphoreType.DMA((CAST_SLOTS,)),
                          pltpu.SemaphoreType.DMA((2, N_XHEADS)),
                          pltpu.VMEM((2, POOL_HIST, seqs_per_step, D_POOL), F32),
                          pltpu.VMEM((2, POOL_HIST, seqs_per_step, D_POOL), F32),
                          pltpu.SemaphoreType.DMA((2, 2))],
        compiler_params=_params(2),
        name="mix",
    )(x_prompt, xs, hist_pool, hist_conv, mem_prompt, g1, w_ffn1_in, w_ffn1_out, gmix, w_mix_in, poolw, pscale,
      convw, w_mix_out, gmem, w_xk, w_xv)

    assert (2 * seqs_per_step, N_MEM, d) == stage_shape
    tail_cast_shapes = [(d, d), (d, d), (d, 2 * D_FF), (D_FF, d)]
    y_prompt, y_sample = pl.pallas_call(
        _tail_kernel,
        grid=(bp, nj),
        in_specs=[tile_rows,
                  pl.BlockSpec((None, d, N_MEM), lambda b, j: (b, 0, 0)),
                  pl.BlockSpec((None, N_MEM, d), lambda b, j: (b, 0, 0)),
                  step_rows,
                  any_space, any_space,
                  _resident(gq), any_space, any_space, _resident(g2), any_space, any_space, _resident(gf)],
        out_specs=[tile_rows, step_rows],
        out_shape=[jax.ShapeDtypeStruct((bp, seq, d), F32),
                   jax.ShapeDtypeStruct((bs * tdec, d), F32)],
        scratch_shapes=[pltpu.VMEM((tm + srows, D_FF), BF16),
                        pltpu.VMEM((tm + srows, d), BF16),
                        pltpu.VMEM((srows, d), F32)]
                       + [pltpu.VMEM(s, BF16) for s in tail_cast_shapes]
                       + [pltpu.VMEM((2,) + stage_shape, F32),
                          pltpu.SemaphoreType.DMA((2, 2, N_XHEADS)),
                          pltpu.SemaphoreType.DMA((CAST_SLOTS,))],
        compiler_params=_params(2),
        name="tail",
    )(x2p, kt, vb, x2s, cache_mem_k, cache_mem_v, gq, w_xq, w_xo, g2, w_ffn2_in, w_ffn2_out, gf)

    return (y_prompt,
            y_sample.reshape(bs, tdec, d),
            jnp.transpose(npool_p, (1, 0, 2))[None],
            nconv_p[None],
            mk,
            mv,
            jnp.transpose(npool_s, (1, 0, 2))[None],
            nconv_s.reshape(1, bs, CONV_HIST, D_CONV))
```

```python
import jax
import jax.numpy as jnp
from jax import lax
from jax.experimental import pallas as pl
from jax.experimental.pallas import tpu as pltpu

D_MODEL = 1024
D_POOL = 512
D_CONV = 512
POOL_WINDOWS = (2, 4, 8, 16)
POOL_GROUP = 128
POOL_HIST = 15
CONV_HIST = 2
D_FF = 2816
N_MEM = 256
N_XHEADS = 4
XHEAD_DIM = 256
EPS = 1e-6
PAST_LEN = 16384

FF_CHUNK = 512
FF_CHUNKS = tuple((s, min(FF_CHUNK, D_FF - s)) for s in range(0, D_FF, FF_CHUNK))
POOL_PAD = 16
CONV_PAD = 8
VMEM_LIMIT_BYTES = 60 * 1024 * 1024

CAST_SLOTS, CAST_ROWS, CAST_COLS = 8, 256, 1024
EDGE_ROW_PARTS = 2
TM_PROMPT = 512

F32 = jnp.float32
BF16 = jnp.bfloat16


def _rms(x, g):
    y = x * lax.rsqrt(jnp.mean(x * x, axis=-1, keepdims=True) + EPS)
    return y * g


def _dot(a, b):
    return jnp.dot(a, b, preferred_element_type=F32)


def _dot_rows(a, b, parts):
    rows = a.shape[0] // parts
    return jnp.concatenate([_dot(a[i * rows:(i + 1) * rows], b) for i in range(parts)], axis=0)


def _ffn_half(x, g_ref, win_ref, wout_ref, act_ref):
    m = x.shape[0]
    h = _rms(x, g_ref[...]).astype(BF16)
    for i, (s0, n) in enumerate(FF_CHUNKS):
        parts = EDGE_ROW_PARTS if i == 0 else 1
        gate = _dot_rows(h, win_ref[:, s0:s0 + n], parts)
        up = _dot_rows(h, win_ref[:, D_FF + s0:D_FF + s0 + n], parts)
        silu = gate * (1.0 / (1.0 + jnp.exp(-gate)))
        act_ref[0:m, s0:s0 + n] = (silu * up).astype(BF16)
    y = _dot_rows(act_ref[0:m, :], wout_ref[...], EDGE_ROW_PARTS)
    return x + 0.5 * y


def _cast_weights_to_vmem(weights, stage_ref, sem):
    n_slots, cast_rows, cast_cols = stage_ref.shape
    blocks = []
    for w_hbm, dst in weights:
        rows, cols = dst.shape
        for r0 in range(0, rows, cast_rows):
            for c0 in range(0, cols, cast_cols):
                blocks.append((w_hbm, dst, r0, min(cast_rows, rows - r0), c0, min(cast_cols, cols - c0)))

    def copy(i):
        w_hbm, _, r0, nr, c0, nc = blocks[i]
        slot = i % n_slots
        return pltpu.make_async_copy(w_hbm.at[0, pl.ds(r0, nr), pl.ds(c0, nc)],
                                     stage_ref.at[slot, pl.ds(0, nr), pl.ds(0, nc)], sem.at[slot])

    for i in range(min(n_slots, len(blocks))):
        copy(i).start()
    for i, (_, dst, r0, nr, c0, nc) in enumerate(blocks):
        copy(i).wait()
        dst[r0:r0 + nr, c0:c0 + nc] = stage_ref[i % n_slots, 0:nr, 0:nc].astype(BF16)
        if i + n_slots < len(blocks):
            copy(i + n_slots).start()


def _window_sum(x, w):
    s = x
    sh = 1
    while sh < w:
        s = s + pltpu.roll(s, sh, 0)
        sh *= 2
    return s


def _pool_delta_rows(ext, a, pos, w, extract):
    ssum = extract(_window_sum(ext, w))
    cnt = jnp.minimum(pos + 1, w).astype(F32)
    return ssum / cnt - a


def _pool_delta_steps(hist, new, w, pos0):
    n_hist = len(hist)
    sums = dict(enumerate(hist + new))
    sh = 1
    while sh < w:
        sums = {i: sums[i] + sums[i - sh] for i in sums if (i - sh) in sums}
        sh *= 2
    return [sums[n_hist + tt] / float(min(pos0 + tt + 1, w)) - new[tt] for tt in range(len(new))]


def _conv3(ext, convw_ref, extract):
    u0 = extract(pltpu.roll(ext, 2, 0))
    u1 = extract(pltpu.roll(ext, 1, 0))
    u2 = extract(ext)
    return convw_ref[0:1, :] * u0 + convw_ref[1:2, :] * u1 + convw_ref[2:3, :] * u2


def _memkv_copies(kv_scr, mk_hbm, mv_hbm, sem, b):
    copies = []
    for h in range(N_XHEADS):
        cols = pl.ds(h * XHEAD_DIM, XHEAD_DIM)
        copies.append(pltpu.make_async_copy(kv_scr.at[0, :, cols], mk_hbm.at[0, b, :, h, :], sem.at[0, h]))
        copies.append(pltpu.make_async_copy(kv_scr.at[1, :, cols], mv_hbm.at[0, b, :, h, :], sem.at[1, h]))
    return copies


def _mix_kernel(x_ref, xs_ref, hpool_hbm, hconv_ref, mem_ref, g1_ref, w1in_hbm, w1out_hbm, gmix_ref, wmixin_hbm,
                poolw_ref, pscale_ref, convw_hbm, wmixout_hbm, gmem_ref, wk_hbm, wv_hbm,
                x2_ref, x2s_ref, npool_ref, nconv_ref, npool_s_hbm, nconv_s_ref, mk_hbm, mv_hbm, kt_ref, vb_ref,
                act_ref, epool_ref, econv_ref, a_ref, d_ref, econv_s_ref, mixed_ref,
                w1in_ref, w1out_ref, wmixin_ref, wmixout_ref, wk_ref, wv_ref, stage_ref, stage_sem, memkv_sem,
                hpool2_ref, npool2_s_ref, pool_sem, convw_ref, convw_sem):
    tm = x_ref.shape[0]
    ns = hpool2_ref.shape[2]
    t = xs_ref.shape[0] // ns
    crow = CONV_PAD + t
    b = pl.program_id(0)
    j = pl.program_id(1)
    step = b * pl.num_programs(1) + j
    n_steps = pl.num_programs(0) * pl.num_programs(1)
    slot = lax.rem(step, 2)

    def hist_in(st, sl):
        return pltpu.make_async_copy(hpool_hbm.at[:, pl.ds(st * ns, ns), :], hpool2_ref.at[sl], pool_sem.at[0, sl])

    def state_out(st, sl):
        return pltpu.make_async_copy(npool2_s_ref.at[sl], npool_s_hbm.at[:, pl.ds(st * ns, ns), :],
                                     pool_sem.at[1, sl])

    @pl.when(step == 0)
    def _():
        hist_in(0, 0).start()

    @pl.when(step >= 2)
    def _():
        state_out(step - 2, slot).wait()

    hist_in(step, slot).wait()

    @pl.when(step + 1 < n_steps)
    def _():
        hist_in(step + 1, 1 - slot).start()

    hpool_ref = hpool2_ref.at[slot]
    npool_s_ref = npool2_s_ref.at[slot]

    @pl.when(jnp.logical_and(b == 0, j == 0))
    def _():
        taps = pltpu.make_async_copy(convw_hbm.at[0], convw_ref, convw_sem.at[0])
        taps.start()
        _cast_weights_to_vmem([(wk_hbm, wk_ref), (wv_hbm, wv_ref), (w1in_hbm, w1in_ref), (w1out_hbm, w1out_ref),
                               (wmixin_hbm, wmixin_ref), (wmixout_hbm, wmixout_ref)], stage_ref, stage_sem)
        taps.wait()

    @pl.when(jnp.logical_and(j == 0, b > 0))
    def _():
        for c in _memkv_copies(stage_ref, mk_hbm, mv_hbm, memkv_sem, b - 1):
            c.wait()

    @pl.when(j == 0)
    def _():
        epool_ref[0:POOL_PAD, :] = jnp.zeros((POOL_PAD, D_POOL), F32)
        econv_ref[0:CONV_PAD, :] = jnp.zeros((CONV_PAD, D_CONV), F32)
        mn = _rms(mem_ref[...], gmem_ref[...]).astype(BF16)
        k = _dot(mn, wk_ref[...])
        v = _dot(mn, wv_ref[...])
        stage_ref[0] = k
        stage_ref[1] = v
        kt_ref[...] = k.T.astype(BF16)
        vb_ref[...] = v.astype(BF16)
        for c in _memkv_copies(stage_ref, mk_hbm, mv_hbm, memkv_sem, b):
            c.start()

    x1 = _ffn_half(jnp.concatenate([x_ref[...], xs_ref[...]], axis=0), g1_ref, w1in_ref, w1out_ref, act_ref)
    z = _dot_rows(_rms(x1, gmix_ref[...]).astype(BF16), wmixin_ref[...], EDGE_ROW_PARTS)
    a = z[:, 0:D_POOL]
    cb = z[:, D_POOL:D_POOL + D_CONV]
    u = z[:, D_POOL + D_CONV:D_POOL + 2 * D_CONV] * z[:, D_POOL + 2 * D_CONV:]

    epool_ref[POOL_PAD:, :] = a[0:tm]
    pos = j * tm + lax.broadcasted_iota(jnp.int32, (tm, 1), 0)
    for gi, w in enumerate(POOL_WINDOWS):
        sl = slice(gi * POOL_GROUP, (gi + 1) * POOL_GROUP)
        d_prompt = _pool_delta_rows(epool_ref[:, sl], a[0:tm, sl], pos, w, lambda s: s[POOL_PAD:, :])
        a_ref[gi] = a[tm:, sl]
        hist = [hpool_ref[i, :, sl] for i in range(POOL_HIST)]
        new = [a_ref[gi, pl.ds(tt, ns, stride=t), :] for tt in range(t)]
        for r, slab in enumerate((hist + new)[-POOL_HIST:]):
            npool_s_ref[r, :, sl] = slab
        for tt, slab in enumerate(_pool_delta_steps(hist, new, w, PAST_LEN)):
            d_ref[gi, pl.ds(tt, ns, stride=t), :] = slab
        d_all = jnp.concatenate([d_prompt, d_ref[gi]], axis=0).astype(BF16)
        mixed_ref[:, sl] = (_dot(d_all, poolw_ref[gi].astype(BF16)) * pscale_ref[:, sl]).astype(BF16)

    econv_ref[CONV_PAD:, :] = u[0:tm]
    yc_prompt = _conv3(econv_ref[...], convw_ref, lambda s: s[CONV_PAD:, :])
    econv_s_ref[:, 0:CONV_PAD - CONV_HIST, :] = jnp.zeros((ns, CONV_PAD - CONV_HIST, D_CONV), F32)
    econv_s_ref[:, CONV_PAD - CONV_HIST:CONV_PAD, :] = hconv_ref[...]
    econv_s_ref[:, CONV_PAD:, :] = u[tm:].reshape(ns, t, D_CONV)
    yc_sample = _conv3(econv_s_ref[...].reshape(ns * crow, D_CONV), convw_ref,
                       lambda s: s.reshape(ns, crow, D_CONV)[:, CONV_PAD:, :].reshape(ns * t, D_CONV))
    mixed_ref[:, D_POOL:] = (cb * jnp.concatenate([yc_prompt, yc_sample], axis=0)).astype(BF16)

    x2 = x1 + _dot_rows(mixed_ref[...], wmixout_ref[...], EDGE_ROW_PARTS)
    x2_ref[...] = x2[0:tm]
    x2s_ref[...] = x2[tm:]
    nconv_s_ref[...] = econv_s_ref[:, crow - CONV_HIST:crow, :]
    state_out(step, slot).start()

    epool_ref[0:POOL_PAD, :] = epool_ref[tm:tm + POOL_PAD, :]
    econv_ref[0:CONV_PAD, :] = econv_ref[tm:tm + CONV_PAD, :]

    @pl.when(j == pl.num_programs(1) - 1)
    def _():
        for r in range(POOL_HIST):
            row = POOL_PAD - POOL_HIST + r
            npool_ref[r, pl.ds(b, 1), :] = epool_ref[row:row + 1, :]
        nconv_ref[...] = econv_ref[CONV_PAD - CONV_HIST:CONV_PAD, :]

    @pl.when(jnp.logical_and(b == pl.num_programs(0) - 1, j == pl.num_programs(1) - 1))
    def _():
        for c in _memkv_copies(stage_ref, mk_hbm, mv_hbm, memkv_sem, b):
            c.wait()
        state_out(step - 1, 1 - slot).wait()
        state_out(step, slot).wait()


def _softmax(s):
    e = jnp.exp(s - jnp.max(s, axis=-1, keepdims=True))
    return e / jnp.sum(e, axis=-1, keepdims=True)


def _queries(x, gq_ref, wq_ref):
    return _dot_rows(_rms(x, gq_ref[...]).astype(BF16), wq_ref[...], EDGE_ROW_PARTS) * (XHEAD_DIM ** -0.5)


def _kv_copies(k_hbm, v_hbm, kv_ref, sem, step, slot):
    ns = kv_ref.shape[1] // 2
    seqs = pl.ds(step * ns, ns)
    copies = []
    for h in range(N_XHEADS):
        cols = pl.ds(h * XHEAD_DIM, XHEAD_DIM)
        copies.append(pltpu.make_async_copy(k_hbm.at[0, seqs, :, h, :], kv_ref.at[slot, pl.ds(0, ns), :, cols],
                                            sem.at[slot, 0, h]))
        copies.append(pltpu.make_async_copy(v_hbm.at[0, seqs, :, h, :], kv_ref.at[slot, pl.ds(ns, ns), :, cols],
                                            sem.at[slot, 1, h]))
    return copies


def _cached_scores(q, kv_ref):
    ns = kv_ref.shape[0] // 2
    t = q.shape[0] // ns
    d = q.shape[1]
    rows_per_seq = N_XHEADS * t
    row_head = lax.broadcasted_iota(jnp.int32, (rows_per_seq, d), 0) // t
    col_head = lax.broadcasted_iota(jnp.int32, (rows_per_seq, d), 1) // XHEAD_DIM
    own_head = row_head == col_head
    scores = []
    for s in range(ns):
        qe = jnp.where(own_head, jnp.concatenate([q[s * t:(s + 1) * t, :]] * N_XHEADS, axis=0), 0.0).astype(BF16)
        scores.append(lax.dot_general(qe, kv_ref[s].astype(BF16), (((1,), (1,)), ((), ())),
                                      preferred_element_type=F32))
    return jnp.concatenate(scores, axis=0)


def _cached_outputs(scores, kv_ref, o_ref):
    ns = kv_ref.shape[0] // 2
    t = o_ref.shape[0] // ns
    rows_per_seq = N_XHEADS * t
    p = _softmax(scores).astype(BF16)
    for s in range(ns):
        oa = _dot(p[s * rows_per_seq:(s + 1) * rows_per_seq, :], kv_ref[ns + s].astype(BF16))
        for h in range(N_XHEADS):
            sl = slice(h * XHEAD_DIM, (h + 1) * XHEAD_DIM)
            o_ref[s * t:(s + 1) * t, sl] = oa[h * t:(h + 1) * t, sl]


def _tail_kernel(x_ref, kt_ref, v_ref, x2s_ref, ck_hbm, cv_hbm, gq_ref, wq_hbm, wo_hbm, g2_ref,
                 w2in_hbm, w2out_hbm, gf_ref,
                 y_ref, ys_ref,
                 act_ref, o_ref, os_ref, wq_ref, wo_ref, w2in_ref, w2out_ref, kv_ref, kv_sem, stage_sem):
    step = pl.program_id(0) * pl.num_programs(1) + pl.program_id(1)
    n_steps = pl.num_programs(0) * pl.num_programs(1)
    tm = x_ref.shape[0]
    slot = lax.rem(step, 2)

    @pl.when(step == 0)
    def _():
        _cast_weights_to_vmem([(wq_hbm, wq_ref), (wo_hbm, wo_ref), (w2in_hbm, w2in_ref),
                               (w2out_hbm, w2out_ref)], kv_ref.at[0], stage_sem)
        for c in _kv_copies(ck_hbm, cv_hbm, kv_ref, kv_sem, 0, 0):
            c.start()

    for c in _kv_copies(ck_hbm, cv_hbm, kv_ref, kv_sem, step, slot):
        c.wait()

    @pl.when(step + 1 < n_steps)
    def _():
        for c in _kv_copies(ck_hbm, cv_hbm, kv_ref, kv_sem, step + 1, 1 - slot):
            c.start()

    kv = kv_ref.at[slot]
    x2 = jnp.concatenate([x_ref[...], x2s_ref[...]], axis=0)
    q_all = _queries(x2, gq_ref, wq_ref)
    q = q_all[0:tm].astype(BF16)
    heads = [slice(h * XHEAD_DIM, (h + 1) * XHEAD_DIM) for h in range(N_XHEADS)]
    scores = [_dot(q[:, heads[h]], kt_ref[heads[h], :]) for h in range(N_XHEADS)]
    sample_scores = _cached_scores(q_all[tm:], kv)
    for h in range(N_XHEADS):
        p = _softmax(scores[h]).astype(BF16)
        o_ref[0:tm, heads[h]] = _dot(p, v_ref[:, heads[h]]).astype(BF16)
        if h == 1:
            _cached_outputs(sample_scores, kv, os_ref)
    o_ref[tm:, :] = os_ref[...].astype(BF16)
    x3 = x2 + _dot_rows(o_ref[...], wo_ref[...], EDGE_ROW_PARTS)
    x4 = _ffn_half(x3, g2_ref, w2in_ref, w2out_ref, act_ref)
    y = _rms(x4, gf_ref[...])
    y_ref[...] = y[0:tm]
    ys_ref[...] = y[tm:]


def _resident(arr):
    nd = arr.ndim
    return pl.BlockSpec(arr.shape, lambda *_: (0,) * nd, pipeline_mode=pl.Buffered(1))


def _params(n_grid):
    return pltpu.CompilerParams(dimension_semantics=("arbitrary",) * n_grid,
                                vmem_limit_bytes=VMEM_LIMIT_BYTES)


def kernel(x_prompt, x_sample, mem_prompt, state_pool, state_conv, cache_mem_k, cache_mem_v, g_ffn1, w_ffn1_in, w_ffn1_out, g_mix, w_mix_in, pool_w, pool_scale, conv_w, w_mix_out, g_xq, g_mem, w_xq, w_xk, w_xv, w_xo, g_ffn2, w_ffn2_in, w_ffn2_out, g_final):
    depth = g_ffn1.shape[0]
    assert depth == 1
    bp, seq, d = x_prompt.shape
    bs, tdec, _ = x_sample.shape
    l = 0

    poolw = pool_w[l]
    any_space = pl.BlockSpec(memory_space=pl.ANY)
    stage_shape = (CAST_SLOTS, CAST_ROWS, CAST_COLS)
    g1, gmix, gq, gmem, g2 = g_ffn1[l:l + 1], g_mix[l:l + 1], g_xq[l:l + 1], g_mem[l:l + 1], g_ffn2[l:l + 1]
    gf = g_final.reshape(1, d)
    pscale = pool_scale[l:l + 1]
    convw = conv_w[l:l + 1]

    tm = TM_PROMPT
    nj = seq // tm
    n_steps = bp * nj
    seqs_per_step = bs // n_steps
    assert seqs_per_step * n_steps == bs
    srows = seqs_per_step * tdec
    xs = x_sample.reshape(bs * tdec, d)
    hist_pool = jnp.transpose(state_pool[l], (1, 0, 2))
    hist_conv = state_conv[l].reshape(n_steps, seqs_per_step, CONV_HIST, D_CONV)
    tile_rows = pl.BlockSpec((None, tm, d), lambda b, j: (b, j, 0))
    step_rows = pl.BlockSpec((srows, d), lambda b, j: (b * nj + j, 0))
    step_conv = pl.BlockSpec((None, seqs_per_step, CONV_HIST, D_CONV), lambda b, j: (b * nj + j, 0, 0, 0))
    mix_cast_shapes = [(d, 2 * D_FF), (D_FF, d), (d, D_POOL + 3 * D_CONV), (D_POOL + D_CONV, d), (d, d), (d, d)]
    batch_mem = pl.BlockSpec((None, N_MEM, d), lambda b, j: (b, 0, 0))
    x2p, x2s, npool_p, nconv_p, npool_s, nconv_s, mk, mv, kt, vb = pl.pallas_call(
        _mix_kernel,
        grid=(bp, nj),
        in_specs=[tile_rows, step_rows, any_space, step_conv, batch_mem,
                  _resident(g1), any_space, any_space, _resident(gmix), any_space, _resident(poolw),
                  _resident(pscale), any_space, any_space, _resident(gmem), any_space, any_space],
        out_specs=[tile_rows, step_rows,
                   pl.BlockSpec((POOL_HIST, bp, D_POOL), lambda b, j: (0, 0, 0)),
                   pl.BlockSpec((None, CONV_HIST, D_CONV), lambda b, j: (b, 0, 0)),
                   any_space, step_conv, any_space, any_space,
                   pl.BlockSpec((None, d, N_MEM), lambda b, j: (b, 0, 0)),
                   pl.BlockSpec((None, N_MEM, d), lambda b, j: (b, 0, 0))],
        out_shape=[jax.ShapeDtypeStruct((bp, seq, d), F32),
                   jax.ShapeDtypeStruct((bs * tdec, d), F32),
                   jax.ShapeDtypeStruct((POOL_HIST, bp, D_POOL), F32),
                   jax.ShapeDtypeStruct((bp, CONV_HIST, D_CONV), F32),
                   jax.ShapeDtypeStruct((POOL_HIST, bs, D_POOL), F32),
                   jax.ShapeDtypeStruct((n_steps, seqs_per_step, CONV_HIST, D_CONV), F32),
                   jax.ShapeDtypeStruct((1, bp, N_MEM, N_XHEADS, XHEAD_DIM), F32),
                   jax.ShapeDtypeStruct((1, bp, N_MEM, N_XHEADS, XHEAD_DIM), F32),
                   jax.ShapeDtypeStruct((bp, d, N_MEM), BF16),
                   jax.ShapeDtypeStruct((bp, N_MEM, d), BF16)],
        scratch_shapes=[pltpu.VMEM((tm + srows, D_FF), BF16),
                        pltpu.VMEM((tm + POOL_PAD, D_POOL), F32),
                        pltpu.VMEM((tm + CONV_PAD, D_CONV), F32),
                        pltpu.VMEM((len(POOL_WINDOWS), srows, POOL_GROUP), F32),
                        pltpu.VMEM((len(POOL_WINDOWS), srows, POOL_GROUP), F32),
                        pltpu.VMEM((seqs_per_step, CONV_PAD + tdec, D_CONV), F32),
                        pltpu.VMEM((tm + srows, d), BF16)]
                       + [pltpu.VMEM(s, BF16) for s in mix_cast_shapes]
                       + [pltpu.VMEM(stage_shape, F32),
                          pltpu.SemaphoreType.DMA((CAST_SLOTS,)),
                          pltpu.SemaphoreType.DMA((2, N_XHEADS)),
                          pltpu.VMEM((2, POOL_HIST, seqs_per_step, D_POOL), F32),
                          pltpu.VMEM((2, POOL_HIST, seqs_per_step, D_POOL), F32),
                          pltpu.SemaphoreType.DMA((2, 2)),
                          pltpu.VMEM(convw.shape[1:], F32),
                          pltpu.SemaphoreType.DMA((1,))],
        compiler_params=_params(2),
        name="mix",
    )(x_prompt, xs, hist_pool, hist_conv, mem_prompt, g1, w_ffn1_in, w_ffn1_out, gmix, w_mix_in, poolw, pscale,
      convw, w_mix_out, gmem, w_xk, w_xv)

    assert (2 * seqs_per_step, N_MEM, d) == stage_shape
    tail_cast_shapes = [(d, d), (d, d), (d, 2 * D_FF), (D_FF, d)]
    y_prompt, y_sample = pl.pallas_call(
        _tail_kernel,
        grid=(bp, nj),
        in_specs=[tile_rows,
                  pl.BlockSpec((None, d, N_MEM), lambda b, j: (b, 0, 0)),
                  pl.BlockSpec((None, N_MEM, d), lambda b, j: (b, 0, 0)),
                  step_rows,
                  any_space, any_space,
                  _resident(gq), any_space, any_space, _resident(g2), any_space, any_space, _resident(gf)],
        out_specs=[tile_rows, step_rows],
        out_shape=[jax.ShapeDtypeStruct((bp, seq, d), F32),
                   jax.ShapeDtypeStruct((bs * tdec, d), F32)],
        scratch_shapes=[pltpu.VMEM((tm + srows, D_FF), BF16),
                        pltpu.VMEM((tm + srows, d), BF16),
                        pltpu.VMEM((srows, d), F32)]
                       + [pltpu.VMEM(s, BF16) for s in tail_cast_shapes]
                       + [pltpu.VMEM((2,) + stage_shape, F32),
                          pltpu.SemaphoreType.DMA((2, 2, N_XHEADS)),
                          pltpu.SemaphoreType.DMA((CAST_SLOTS,))],
        compiler_params=_params(2),
        name="tail",
    )(x2p, kt, vb, x2s, cache_mem_k, cache_mem_v, gq, w_xq, w_xo, g2, w_ffn2_in, w_ffn2_out, gf)

    return (y_prompt,
            y_sample.reshape(bs, tdec, d),
            jnp.transpose(npool_p, (1, 0, 2))[None],
            nconv_p[None],
            mk,
            mv,
            jnp.transpose(npool_s, (1, 0, 2))[None],
            nconv_s.reshape(1, bs, CONV_HIST, D_CONV))
```

```python
import jax
import jax.numpy as jnp
from jax import lax
from jax.experimental import pallas as pl
from jax.experimental.pallas import tpu as pltpu

D_MODEL = 1024
D_POOL = 512
D_CONV = 512
POOL_WINDOWS = (2, 4, 8, 16)
POOL_GROUP = 128
POOL_HIST = 15
CONV_HIST = 2
D_FF = 2816
N_MEM = 256
N_XHEADS = 4
XHEAD_DIM = 256
EPS = 1e-6
PAST_LEN = 16384

FF_CHUNK = 512
FF_CHUNKS = tuple((s, min(FF_CHUNK, D_FF - s)) for s in range(0, D_FF, FF_CHUNK))
POOL_PAD = 16
CONV_PAD = 8
VMEM_LIMIT_BYTES = 60 * 1024 * 1024

CAST_SLOTS, CAST_ROWS, CAST_COLS = 8, 256, 1024
EDGE_ROW_PARTS = 2
TM_PROMPT = 512

F32 = jnp.float32
BF16 = jnp.bfloat16


def _rms(x, g):
    y = x * lax.rsqrt(jnp.mean(x * x, axis=-1, keepdims=True) + EPS)
    return y * g


def _dot(a, b):
    return jnp.dot(a, b, preferred_element_type=F32)


def _dot_rows(a, b, parts):
    rows = a.shape[0] // parts
    return jnp.concatenate([_dot(a[i * rows:(i + 1) * rows], b) for i in range(parts)], axis=0)


def _ffn_half(x, g_ref, win_ref, wout_ref, act_ref):
    m = x.shape[0]
    h = _rms(x, g_ref[...]).astype(BF16)
    for i, (s0, n) in enumerate(FF_CHUNKS):
        parts = EDGE_ROW_PARTS if i == 0 else 1
        gate = _dot_rows(h, win_ref[:, s0:s0 + n], parts)
        up = _dot_rows(h, win_ref[:, D_FF + s0:D_FF + s0 + n], parts)
        silu = gate * (1.0 / (1.0 + jnp.exp(-gate)))
        act_ref[0:m, s0:s0 + n] = (silu * up).astype(BF16)
    y = _dot_rows(act_ref[0:m, :], wout_ref[...], EDGE_ROW_PARTS)
    return x + 0.5 * y


def _cast_weights_to_vmem(weights, stage_ref, sem):
    n_slots, cast_rows, cast_cols = stage_ref.shape
    blocks = []
    for w_hbm, dst in weights:
        rows, cols = dst.shape
        for r0 in range(0, rows, cast_rows):
            for c0 in range(0, cols, cast_cols):
                blocks.append((w_hbm, dst, r0, min(cast_rows, rows - r0), c0, min(cast_cols, cols - c0)))

    def copy(i):
        w_hbm, _, r0, nr, c0, nc = blocks[i]
        slot = i % n_slots
        return pltpu.make_async_copy(w_hbm.at[0, pl.ds(r0, nr), pl.ds(c0, nc)],
                                     stage_ref.at[slot, pl.ds(0, nr), pl.ds(0, nc)], sem.at[slot])

    for i in range(min(n_slots, len(blocks))):
        copy(i).start()
    for i, (_, dst, r0, nr, c0, nc) in enumerate(blocks):
        copy(i).wait()
        dst[r0:r0 + nr, c0:c0 + nc] = stage_ref[i % n_slots, 0:nr, 0:nc].astype(BF16)
        if i + n_slots < len(blocks):
            copy(i + n_slots).start()


def _window_sum(x, w):
    s = x
    sh = 1
    while sh < w:
        s = s + pltpu.roll(s, sh, 0)
        sh *= 2
    return s


def _pool_delta_rows(ext, a, pos, w, extract):
    ssum = extract(_window_sum(ext, w))
    cnt = jnp.minimum(pos + 1, w).astype(F32)
    return ssum / cnt - a


def _pool_delta_steps(hist, new, w, pos0):
    n_hist = len(hist)
    sums = dict(enumerate(hist + new))
    sh = 1
    while sh < w:
        sums = {i: sums[i] + sums[i - sh] for i in sums if (i - sh) in sums}
        sh *= 2
    return [sums[n_hist + tt] / float(min(pos0 + tt + 1, w)) - new[tt] for tt in range(len(new))]


def _conv3(ext, convw_ref, extract):
    u0 = extract(pltpu.roll(ext, 2, 0))
    u1 = extract(pltpu.roll(ext, 1, 0))
    u2 = extract(ext)
    return convw_ref[0:1, :] * u0 + convw_ref[1:2, :] * u1 + convw_ref[2:3, :] * u2


def _memkv_copies(kv_scr, mk_hbm, mv_hbm, sem, b):
    copies = []
    for h in range(N_XHEADS):
        cols = pl.ds(h * XHEAD_DIM, XHEAD_DIM)
        copies.append(pltpu.make_async_copy(kv_scr.at[0, :, cols], mk_hbm.at[0, b, :, h, :], sem.at[0, h]))
        copies.append(pltpu.make_async_copy(kv_scr.at[1, :, cols], mv_hbm.at[0, b, :, h, :], sem.at[1, h]))
    return copies


def _mix_kernel(x_ref, xs_ref, hpool_hbm, hconv_ref, mem_ref, g1_ref, w1in_hbm, w1out_hbm, gmix_ref, wmixin_hbm,
                poolw_ref, pscale_ref, convw_hbm, wmixout_hbm, gmem_ref, wk_hbm, wv_hbm,
                x2_ref, x2s_ref, npool_ref, nconv_ref, npool_s_hbm, nconv_s_ref, mk_hbm, mv_hbm, kt_ref, vb_ref,
                act_ref, epool_ref, econv_ref, a_ref, d_ref, econv_s_ref, mixed_ref,
                w1in_ref, w1out_ref, wmixin_ref, wmixout_ref, wk_ref, wv_ref, stage_ref, stage_sem, memkv_sem,
                hpool2_ref, npool2_s_ref, pool_sem, convw_ref, convw_sem):
    tm = x_ref.shape[0]
    ns = hpool2_ref.shape[2]
    t = xs_ref.shape[0] // ns
    crow = CONV_PAD + t
    b = pl.program_id(0)
    j = pl.program_id(1)
    step = b * pl.num_programs(1) + j
    n_steps = pl.num_programs(0) * pl.num_programs(1)
    slot = lax.rem(step, 2)

    def hist_in(st, sl):
        return pltpu.make_async_copy(hpool_hbm.at[:, pl.ds(st * ns, ns), :], hpool2_ref.at[sl], pool_sem.at[0, sl])

    def state_out(st, sl):
        return pltpu.make_async_copy(npool2_s_ref.at[sl], npool_s_hbm.at[:, pl.ds(st * ns, ns), :],
                                     pool_sem.at[1, sl])

    @pl.when(step == 0)
    def _():
        hist_in(0, 0).start()

    @pl.when(step >= 2)
    def _():
        state_out(step - 2, slot).wait()

    hist_in(step, slot).wait()

    @pl.when(step + 1 < n_steps)
    def _():
        hist_in(step + 1, 1 - slot).start()

    hpool_ref = hpool2_ref.at[slot]
    npool_s_ref = npool2_s_ref.at[slot]

    @pl.when(jnp.logical_and(b == 0, j == 0))
    def _():
        taps = pltpu.make_async_copy(convw_hbm.at[0], convw_ref, convw_sem.at[0])
        taps.start()
        _cast_weights_to_vmem([(wk_hbm, wk_ref), (wv_hbm, wv_ref), (w1in_hbm, w1in_ref), (w1out_hbm, w1out_ref),
                               (wmixin_hbm, wmixin_ref), (wmixout_hbm, wmixout_ref)], stage_ref, stage_sem)
        taps.wait()

    @pl.when(jnp.logical_and(j == 0, b > 0))
    def _():
        for c in _memkv_copies(stage_ref, mk_hbm, mv_hbm, memkv_sem, b - 1):
            c.wait()

    @pl.when(j == 0)
    def _():
        epool_ref[0:POOL_PAD, :] = jnp.zeros((POOL_PAD, D_POOL), F32)
        econv_ref[0:CONV_PAD, :] = jnp.zeros((CONV_PAD, D_CONV), F32)
        mn = _rms(mem_ref[...], gmem_ref[...]).astype(BF16)
        k = _dot(mn, wk_ref[...])
        v = _dot(mn, wv_ref[...])
        stage_ref[0] = k
        stage_ref[1] = v
        kt_ref[...] = k.astype(BF16).T
        vb_ref[...] = v.astype(BF16)
        for c in _memkv_copies(stage_ref, mk_hbm, mv_hbm, memkv_sem, b):
            c.start()

    x1 = _ffn_half(jnp.concatenate([x_ref[...], xs_ref[...]], axis=0), g1_ref, w1in_ref, w1out_ref, act_ref)
    z = _dot_rows(_rms(x1, gmix_ref[...]).astype(BF16), wmixin_ref[...], EDGE_ROW_PARTS)
    a = z[:, 0:D_POOL]
    cb = z[:, D_POOL:D_POOL + D_CONV]
    u = z[:, D_POOL + D_CONV:D_POOL + 2 * D_CONV] * z[:, D_POOL + 2 * D_CONV:]

    epool_ref[POOL_PAD:, :] = a[0:tm]
    pos = j * tm + lax.broadcasted_iota(jnp.int32, (tm, 1), 0)
    for gi, w in enumerate(POOL_WINDOWS):
        sl = slice(gi * POOL_GROUP, (gi + 1) * POOL_GROUP)
        d_prompt = _pool_delta_rows(epool_ref[:, sl], a[0:tm, sl], pos, w, lambda s: s[POOL_PAD:, :])
        a_ref[gi] = a[tm:, sl]
        hist = [hpool_ref[i, :, sl] for i in range(POOL_HIST)]
        new = [a_ref[gi, pl.ds(tt, ns, stride=t), :] for tt in range(t)]
        for r, slab in enumerate((hist + new)[-POOL_HIST:]):
            npool_s_ref[r, :, sl] = slab
        for tt, slab in enumerate(_pool_delta_steps(hist, new, w, PAST_LEN)):
            d_ref[gi, pl.ds(tt, ns, stride=t), :] = slab
        d_all = jnp.concatenate([d_prompt, d_ref[gi]], axis=0).astype(BF16)
        mixed_ref[:, sl] = (_dot(d_all, poolw_ref[gi].astype(BF16)) * pscale_ref[:, sl]).astype(BF16)

    econv_ref[CONV_PAD:, :] = u[0:tm]
    yc_prompt = _conv3(econv_ref[...], convw_ref, lambda s: s[CONV_PAD:, :])
    econv_s_ref[:, 0:CONV_PAD - CONV_HIST, :] = jnp.zeros((ns, CONV_PAD - CONV_HIST, D_CONV), F32)
    econv_s_ref[:, CONV_PAD - CONV_HIST:CONV_PAD, :] = hconv_ref[...]
    econv_s_ref[:, CONV_PAD:, :] = u[tm:].reshape(ns, t, D_CONV)
    yc_sample = _conv3(econv_s_ref[...].reshape(ns * crow, D_CONV), convw_ref,
                       lambda s: s.reshape(ns, crow, D_CONV)[:, CONV_PAD:, :].reshape(ns * t, D_CONV))
    mixed_ref[:, D_POOL:] = (cb * jnp.concatenate([yc_prompt, yc_sample], axis=0)).astype(BF16)

    x2 = x1 + _dot_rows(mixed_ref[...], wmixout_ref[...], EDGE_ROW_PARTS)
    x2_ref[...] = x2[0:tm]
    x2s_ref[...] = x2[tm:]
    nconv_s_ref[...] = econv_s_ref[:, crow - CONV_HIST:crow, :]
    state_out(step, slot).start()

    epool_ref[0:POOL_PAD, :] = epool_ref[tm:tm + POOL_PAD, :]
    econv_ref[0:CONV_PAD, :] = econv_ref[tm:tm + CONV_PAD, :]

    @pl.when(j == pl.num_programs(1) - 1)
    def _():
        for r in range(POOL_HIST):
            row = POOL_PAD - POOL_HIST + r
            npool_ref[r, pl.ds(b, 1), :] = epool_ref[row:row + 1, :]
        nconv_ref[...] = econv_ref[CONV_PAD - CONV_HIST:CONV_PAD, :]

    @pl.when(jnp.logical_and(b == pl.num_programs(0) - 1, j == pl.num_programs(1) - 1))
    def _():
        for c in _memkv_copies(stage_ref, mk_hbm, mv_hbm, memkv_sem, b):
            c.wait()
        state_out(step - 1, 1 - slot).wait()
        state_out(step, slot).wait()


def _softmax(s):
    e = jnp.exp(s - jnp.max(s, axis=-1, keepdims=True))
    return e / jnp.sum(e, axis=-1, keepdims=True)


def _queries(x, gq_ref, wq_ref):
    return _dot_rows(_rms(x, gq_ref[...]).astype(BF16), wq_ref[...], EDGE_ROW_PARTS) * (XHEAD_DIM ** -0.5)


def _kv_copies(k_hbm, v_hbm, kv_ref, sem, step, slot):
    ns = kv_ref.shape[1] // 2
    seqs = pl.ds(step * ns, ns)
    copies = []
    for h in range(N_XHEADS):
        cols = pl.ds(h * XHEAD_DIM, XHEAD_DIM)
        copies.append(pltpu.make_async_copy(k_hbm.at[0, seqs, :, h, :], kv_ref.at[slot, pl.ds(0, ns), :, cols],
                                            sem.at[slot, 0, h]))
        copies.append(pltpu.make_async_copy(v_hbm.at[0, seqs, :, h, :], kv_ref.at[slot, pl.ds(ns, ns), :, cols],
                                            sem.at[slot, 1, h]))
    return copies


def _cached_scores(q, kv_ref):
    ns = kv_ref.shape[0] // 2
    t = q.shape[0] // ns
    d = q.shape[1]
    rows_per_seq = N_XHEADS * t
    row_head = lax.broadcasted_iota(jnp.int32, (rows_per_seq, d), 0) // t
    col_head = lax.broadcasted_iota(jnp.int32, (rows_per_seq, d), 1) // XHEAD_DIM
    own_head = row_head == col_head
    scores = []
    for s in range(ns):
        qe = jnp.where(own_head, jnp.concatenate([q[s * t:(s + 1) * t, :]] * N_XHEADS, axis=0), 0.0).astype(BF16)
        scores.append(lax.dot_general(qe, kv_ref[s].astype(BF16), (((1,), (1,)), ((), ())),
                                      preferred_element_type=F32))
    return jnp.concatenate(scores, axis=0)


def _cached_outputs(scores, kv_ref, o_ref):
    ns = kv_ref.shape[0] // 2
    t = o_ref.shape[0] // ns
    rows_per_seq = N_XHEADS * t
    p = _softmax(scores).astype(BF16)
    for s in range(ns):
        oa = _dot(p[s * rows_per_seq:(s + 1) * rows_per_seq, :], kv_ref[ns + s].astype(BF16))
        for h in range(N_XHEADS):
            sl = slice(h * XHEAD_DIM, (h + 1) * XHEAD_DIM)
            o_ref[s * t:(s + 1) * t, sl] = oa[h * t:(h + 1) * t, sl]


def _tail_kernel(x_ref, kt_ref, v_ref, x2s_ref, ck_hbm, cv_hbm, gq_ref, wq_hbm, wo_hbm, g2_ref,
                 w2in_hbm, w2out_hbm, gf_ref,
                 y_ref, ys_ref,
                 act_ref, o_ref, os_ref, wq_ref, wo_ref, w2in_ref, w2out_ref, kv_ref, kv_sem, stage_sem):
    step = pl.program_id(0) * pl.num_programs(1) + pl.program_id(1)
    n_steps = pl.num_programs(0) * pl.num_programs(1)
    tm = x_ref.shape[0]
    slot = lax.rem(step, 2)

    @pl.when(step == 0)
    def _():
        _cast_weights_to_vmem([(wq_hbm, wq_ref), (wo_hbm, wo_ref), (w2in_hbm, w2in_ref),
                               (w2out_hbm, w2out_ref)], kv_ref.at[0], stage_sem)
        for c in _kv_copies(ck_hbm, cv_hbm, kv_ref, kv_sem, 0, 0):
            c.start()

    for c in _kv_copies(ck_hbm, cv_hbm, kv_ref, kv_sem, step, slot):
        c.wait()

    @pl.when(step + 1 < n_steps)
    def _():
        for c in _kv_copies(ck_hbm, cv_hbm, kv_ref, kv_sem, step + 1, 1 - slot):
            c.start()

    kv = kv_ref.at[slot]
    x2 = jnp.concatenate([x_ref[...], x2s_ref[...]], axis=0)
    q_all = _queries(x2, gq_ref, wq_ref)
    q = q_all[0:tm].astype(BF16)
    heads = [slice(h * XHEAD_DIM, (h + 1) * XHEAD_DIM) for h in range(N_XHEADS)]
    scores = [_dot(q[:, heads[h]], kt_ref[heads[h], :]) for h in range(N_XHEADS)]
    sample_scores = _cached_scores(q_all[tm:], kv)
    for h in range(N_XHEADS):
        p = _softmax(scores[h]).astype(BF16)
        o_ref[0:tm, heads[h]] = _dot(p, v_ref[:, heads[h]]).astype(BF16)
        if h == 1:
            _cached_outputs(sample_scores, kv, os_ref)
    o_ref[tm:, :] = os_ref[...].astype(BF16)
    x3 = x2 + _dot_rows(o_ref[...], wo_ref[...], EDGE_ROW_PARTS)
    x4 = _ffn_half(x3, g2_ref, w2in_ref, w2out_ref, act_ref)
    y = _rms(x4, gf_ref[...])
    y_ref[...] = y[0:tm]
    ys_ref[...] = y[tm:]


def _resident(arr):
    nd = arr.ndim
    return pl.BlockSpec(arr.shape, lambda *_: (0,) * nd, pipeline_mode=pl.Buffered(1))


def _params(n_grid):
    return pltpu.CompilerParams(dimension_semantics=("arbitrary",) * n_grid,
                                vmem_limit_bytes=VMEM_LIMIT_BYTES)


def kernel(x_prompt, x_sample, mem_prompt, state_pool, state_conv, cache_mem_k, cache_mem_v, g_ffn1, w_ffn1_in, w_ffn1_out, g_mix, w_mix_in, pool_w, pool_scale, conv_w, w_mix_out, g_xq, g_mem, w_xq, w_xk, w_xv, w_xo, g_ffn2, w_ffn2_in, w_ffn2_out, g_final):
    depth = g_ffn1.shape[0]
    assert depth == 1
    bp, seq, d = x_prompt.shape
    bs, tdec, _ = x_sample.shape
    l = 0

    poolw = pool_w[l]
    any_space = pl.BlockSpec(memory_space=pl.ANY)
    stage_shape = (CAST_SLOTS, CAST_ROWS, CAST_COLS)
    g1, gmix, gq, gmem, g2 = g_ffn1[l:l + 1], g_mix[l:l + 1], g_xq[l:l + 1], g_mem[l:l + 1], g_ffn2[l:l + 1]
    gf = g_final.reshape(1, d)
    pscale = pool_scale[l:l + 1]
    convw = conv_w[l:l + 1]

    tm = TM_PROMPT
    nj = seq // tm
    n_steps = bp * nj
    seqs_per_step = bs // n_steps
    assert seqs_per_step * n_steps == bs
    srows = seqs_per_step * tdec
    xs = x_sample.reshape(bs * tdec, d)
    hist_pool = jnp.transpose(state_pool[l], (1, 0, 2))
    hist_conv = state_conv[l].reshape(n_steps, seqs_per_step, CONV_HIST, D_CONV)
    tile_rows = pl.BlockSpec((None, tm, d), lambda b, j: (b, j, 0))
    step_rows = pl.BlockSpec((srows, d), lambda b, j: (b * nj + j, 0))
    step_conv = pl.BlockSpec((None, seqs_per_step, CONV_HIST, D_CONV), lambda b, j: (b * nj + j, 0, 0, 0))
    mix_cast_shapes = [(d, 2 * D_FF), (D_FF, d), (d, D_POOL + 3 * D_CONV), (D_POOL + D_CONV, d), (d, d), (d, d)]
    batch_mem = pl.BlockSpec((None, N_MEM, d), lambda b, j: (b, 0, 0))
    x2p, x2s, npool_p, nconv_p, npool_s, nconv_s, mk, mv, kt, vb = pl.pallas_call(
        _mix_kernel,
        grid=(bp, nj),
        in_specs=[tile_rows, step_rows, any_space, step_conv, batch_mem,
                  _resident(g1), any_space, any_space, _resident(gmix), any_space, _resident(poolw),
                  _resident(pscale), any_space, any_space, _resident(gmem), any_space, any_space],
        out_specs=[tile_rows, step_rows,
                   pl.BlockSpec((POOL_HIST, bp, D_POOL), lambda b, j: (0, 0, 0)),
                   pl.BlockSpec((None, CONV_HIST, D_CONV), lambda b, j: (b, 0, 0)),
                   any_space, step_conv, any_space, any_space,
                   pl.BlockSpec((None, d, N_MEM), lambda b, j: (b, 0, 0)),
                   pl.BlockSpec((None, N_MEM, d), lambda b, j: (b, 0, 0))],
        out_shape=[jax.ShapeDtypeStruct((bp, seq, d), F32),
                   jax.ShapeDtypeStruct((bs * tdec, d), F32),
                   jax.ShapeDtypeStruct((POOL_HIST, bp, D_POOL), F32),
                   jax.ShapeDtypeStruct((bp, CONV_HIST, D_CONV), F32),
                   jax.ShapeDtypeStruct((POOL_HIST, bs, D_POOL), F32),
                   jax.ShapeDtypeStruct((n_steps, seqs_per_step, CONV_HIST, D_CONV), F32),
                   jax.ShapeDtypeStruct((1, bp, N_MEM, N_XHEADS, XHEAD_DIM), F32),
                   jax.ShapeDtypeStruct((1, bp, N_MEM, N_XHEADS, XHEAD_DIM), F32),
                   jax.ShapeDtypeStruct((bp, d, N_MEM), BF16),
                   jax.ShapeDtypeStruct((bp, N_MEM, d), BF16)],
        scratch_shapes=[pltpu.VMEM((tm + srows, D_FF), BF16),
                        pltpu.VMEM((tm + POOL_PAD, D_POOL), F32),
                        pltpu.VMEM((tm + CONV_PAD, D_CONV), F32),
                        pltpu.VMEM((len(POOL_WINDOWS), srows, POOL_GROUP), F32),
                        pltpu.VMEM((len(POOL_WINDOWS), srows, POOL_GROUP), F32),
                        pltpu.VMEM((seqs_per_step, CONV_PAD + tdec, D_CONV), F32),
                        pltpu.VMEM((tm + srows, d), BF16)]
                       + [pltpu.VMEM(s, BF16) for s in mix_cast_shapes]
                       + [pltpu.VMEM(stage_shape, F32),
                          pltpu.SemaphoreType.DMA((CAST_SLOTS,)),
                          pltpu.SemaphoreType.DMA((2, N_XHEADS)),
                          pltpu.VMEM((2, POOL_HIST, seqs_per_step, D_POOL), F32),
                          pltpu.VMEM((2, POOL_HIST, seqs_per_step, D_POOL), F32),
                          pltpu.SemaphoreType.DMA((2, 2)),
                          pltpu.VMEM(convw.shape[1:], F32),
                          pltpu.SemaphoreType.DMA((1,))],
        compiler_params=_params(2),
        name="mix",
    )(x_prompt, xs, hist_pool, hist_conv, mem_prompt, g1, w_ffn1_in, w_ffn1_out, gmix, w_mix_in, poolw, pscale,
      convw, w_mix_out, gmem, w_xk, w_xv)

    assert (2 * seqs_per_step, N_MEM, d) == stage_shape
    tail_cast_shapes = [(d, d), (d, d), (d, 2 * D_FF), (D_FF, d)]
    y_prompt, y_sample = pl.pallas_call(
        _tail_kernel,
        grid=(bp, nj),
        in_specs=[tile_rows,
                  pl.BlockSpec((None, d, N_MEM), lambda b, j: (b, 0, 0)),
                  pl.BlockSpec((None, N_MEM, d), lambda b, j: (b, 0, 0)),
                  step_rows,
                  any_space, any_space,
                  _resident(gq), any_space, any_space, _resident(g2), any_space, any_space, _resident(gf)],
        out_specs=[tile_rows, step_rows],
        out_shape=[jax.ShapeDtypeStruct((bp, seq, d), F32),
                   jax.ShapeDtypeStruct((bs * tdec, d), F32)],
        scratch_shapes=[pltpu.VMEM((tm + srows, D_FF), BF16),
                        pltpu.VMEM((tm + srows, d), BF16),
                        pltpu.VMEM((srows, d), F32)]
                       + [pltpu.VMEM(s, BF16) for s in tail_cast_shapes]
                       + [pltpu.VMEM((2,) + stage_shape, F32),
                          pltpu.SemaphoreType.DMA((2, 2, N_XHEADS)),
                          pltpu.SemaphoreType.DMA((CAST_SLOTS,))],
        compiler_params=_params(2),
        name="tail",
    )(x2p, kt, vb, x2s, cache_mem_k, cache_mem_v, gq, w_xq, w_xo, g2, w_ffn2_in, w_ffn2_out, gf)

    return (y_prompt,
            y_sample.reshape(bs, tdec, d),
            jnp.transpose(npool_p, (1, 0, 2))[None],
            nconv_p[None],
            mk,
            mv,
            jnp.transpose(npool_s, (1, 0, 2))[None],
            nconv_s.reshape(1, bs, CONV_HIST, D_CONV))
```
